```python
import jax, jax.numpy as jnp
from jax import lax
import numpy as np

D_MODEL = 1024
BATCH = 1
SEQ = 16384
DEPTH = 1
DEC_BATCH = 128
DEC_SEQ = 1
PAST_LEN = 8192
PAGE_SIZE = 128

A_HEADS = 8
A_HEAD_DIM = 64
C_A = A_HEADS * A_HEAD_DIM
R_W = 64
R_A = 64
R_G = 128
C_SHIFT = 3 * C_A + R_W + R_A + R_G
LNX_EPS = 64e-5
Q_HEADS = 8
KV_HEADS = 2
GROUP = Q_HEADS // KV_HEADS
HEAD_DIM = 64
C_B = Q_HEADS * HEAD_DIM
C_KV = KV_HEADS * 2 * HEAD_DIM
BLK = 64
N_SEL = 16
WINDOW = 512
D_CMP = 128
Q_BLOCK = 128
FORCE_SCORE = 1e4
N_IN = C_SHIFT + C_B + 3 * C_KV + 3 * Q_HEADS + 2 * D_MODEL
D_FF = 2816
CONV_W = 3
D_PLE = 256
NORM_EPS = 1e-6

kernel_name = 'rwkv7_nsa_gated_hybrid_step'


def _rmsnorm(x, g):
    xf = x.astype(jnp.float32)
    y = xf * lax.rsqrt(jnp.mean(xf * xf, axis=-1, keepdims=True) + NORM_EPS)
    return (y * g.astype(jnp.float32)).astype(x.dtype)


def _alibi_slopes():
    h = jnp.arange(1, Q_HEADS + 1, dtype=jnp.float32)
    return jnp.exp2(-8.0 * h / Q_HEADS).reshape(KV_HEADS, GROUP)


def _masked_softmax(s, mask):
    s = jnp.where(mask, s, -jnp.inf)
    m = jnp.max(s, axis=-1, keepdims=True)
    m = jnp.where(jnp.isfinite(m), m, 0.0)
    e = jnp.exp(s - m)
    return e / jnp.maximum(jnp.sum(e, axis=-1, keepdims=True), 1e-30)


def _split_proj(proj):
    b, t = proj.shape[0], proj.shape[1]
    pa = proj[..., :C_SHIFT]
    o = C_SHIFT
    q = proj[..., o:o + C_B].reshape(b, t, KV_HEADS, GROUP, HEAD_DIM)
    o += C_B
    kv_c = proj[..., o:o + C_KV].reshape(b, t, KV_HEADS, 2, HEAD_DIM)
    o += C_KV
    kv_s = proj[..., o:o + C_KV].reshape(b, t, KV_HEADS, 2, HEAD_DIM)
    o += C_KV
    kv_w = proj[..., o:o + C_KV].reshape(b, t, KV_HEADS, 2, HEAD_DIM)
    o += C_KV
    gates = proj[..., o:o + 3 * Q_HEADS].reshape(b, t, KV_HEADS, GROUP, 3)
    o += 3 * Q_HEADS
    mg = proj[..., o:]
    return pa, q, kv_c, kv_s, kv_w, gates, mg


def _wkv_scan(r, decay, k, v, a, b, s0):
    f32 = jnp.float32

    def step(S, inp):
        r_t, w_t, k_t, v_t, a_t, b_t = inp
        sa = jnp.einsum('bhij,bhj->bhi', S, a_t)
        S = S * w_t[:, :, None, :] + sa[..., None] * b_t[:, :, None, :] + v_t[..., None] * k_t[:, :, None, :]
        return S, jnp.einsum('bhij,bhj->bhi', S, r_t)

    xs = tuple(jnp.moveaxis(z.astype(f32), 1, 0) for z in (r, decay, k, v, a, b))
    s_new, ys = lax.scan(step, s0.astype(f32), xs)
    return jnp.moveaxis(ys, 0, 1), s_new


def _rwkv_branch(pa, prev, s0, mu, w0, w2, a0, a2, g2, k_k, k_a, r_k, lnx_g, lnx_b):
    b, t = pa.shape[0], pa.shape[1]
    pf = pa.astype(jnp.float32)
    prev_rows = jnp.concatenate([prev[:, None].astype(jnp.float32), pf[:, :-1]], axis=1)
    xs = pf + (prev_rows - pf) * mu
    r = xs[..., :C_A]
    k = xs[..., C_A:2 * C_A]
    v = xs[..., 2 * C_A:3 * C_A]
    o = 3 * C_A
    wd = xs[..., o:o + R_W]
    ad = xs[..., o + R_W:o + R_W + R_A]
    gd = xs[..., o + R_W + R_A:]
    w = -jax.nn.softplus(-(w0 + jnp.tanh(wd) @ w2)) - 0.5
    decay = jnp.exp(-jnp.exp(w))
    a = jax.nn.sigmoid(a0 + ad @ a2)
    g = jax.nn.sigmoid(gd) @ g2
    hd = lambda z: z.reshape(b, t, A_HEADS, A_HEAD_DIM)
    kk = hd(k * k_k)
    kk = kk / jnp.maximum(jnp.sqrt(jnp.sum(kk * kk, axis=-1, keepdims=True)), 1e-12)
    k = k * (1.0 + (a - 1.0) * k_a)
    rh, kh, vh, ah = hd(r), hd(k), hd(v), hd(a)
    y, s_new = _wkv_scan(rh, hd(decay), kh, vh, -kk, kk * ah, s0)
    mean = jnp.mean(y, axis=-1, keepdims=True)
    var = jnp.mean(jnp.square(y - mean), axis=-1, keepdims=True)
    yn = ((y - mean) * lax.rsqrt(var + LNX_EPS)).reshape(b, t, C_A) * lnx_g + lnx_b
    bonus = (jnp.sum(rh * kh * r_k, axis=-1, keepdims=True) * vh).reshape(b, t, C_A)
    return (yn + bonus) * g, pa[:, -1], s_new


def _compress(blocks, pe, w1, b1, w2):
    f32 = jnp.float32
    h = jnp.einsum('nbrhcd,crdf->nbhcf', blocks.astype(f32), w1) + (jnp.einsum('crd,crdf->cf', pe, w1) + b1)
    return jnp.einsum('nbhcf,cfd->nbhcd', jax.nn.gelu(h), w2)


def _select_blocks(score, qpos, n_all):
    blk = jnp.arange(n_all, dtype=jnp.int32)[None, :]
    cur = (qpos // BLK)[:, None]
    forced = (blk == 0) | (blk == cur) | (blk == cur - 1)
    s = jnp.where(forced[None, :, None, :], FORCE_SCORE, score)
    s = jnp.where((blk > cur)[None, :, None, :], -1.0, s)
    _, idx = lax.top_k(s, min(N_SEL, n_all))
    return idx, idx <= cur[None, :, :, None]


def _nsa_core(q, gates, qpos, kvc, n_all, gather, kw, kpos):
    f32 = jnp.float32
    slope = _alibi_slopes()
    qf = q.astype(f32) * (HEAD_DIM ** -0.5)
    qp = qpos.astype(f32)
    nbc = kvc.shape[1]
    cpos = jnp.arange(nbc, dtype=jnp.int32) * BLK + (BLK - 1)
    kc = kvc.astype(f32)
    s = jnp.einsum('bqhgd,bnhd->bqhgn', qf, kc[..., 0, :])
    s = s - slope[None, :, :, None] * (qp[:, None] - cpos.astype(f32))[:, None, None, :]
    p_c = _masked_softmax(s, (cpos[None, :] <= qpos[:, None])[:, None, None, :])
    o_c = jnp.einsum('bqhgn,bnhd->bqhgd', p_c, kc[..., 1, :])
    score = jnp.sum(p_c, axis=3)
    if nbc < n_all:
        score = jnp.pad(score, ((0, 0), (0, 0), (0, 0), (0, n_all - nbc)))
    idx, valid = _select_blocks(score, qpos, n_all)
    nb_, nq_, nh_, ns_ = idx.shape
    ks = gather(idx).astype(f32).reshape(nb_, nq_, nh_, ns_ * BLK, 2, HEAD_DIM)
    spos5 = idx[..., None] * BLK + jnp.arange(BLK, dtype=jnp.int32)
    smask = (valid[..., None] & (spos5 <= qpos[None, :, None, None, None])).reshape(nb_, nq_, nh_, ns_ * BLK)
    spos = spos5.reshape(nb_, nq_, nh_, ns_ * BLK).astype(f32)
    s = jnp.einsum('bqhgd,bqhkd->bqhgk', qf, ks[..., 0, :])
    s = s - slope[None, None, :, :, None] * (qp[None, :, None, None] - spos)[:, :, :, None, :]
    p_s = _masked_softmax(s, smask[:, :, :, None, :])
    o_s = jnp.einsum('bqhgk,bqhkd->bqhgd', p_s, ks[..., 1, :])
    kwf = kw.astype(f32)
    wmask = (kpos[None, :] <= qpos[:, None]) & (qpos[:, None] - kpos[None, :] < WINDOW) & (kpos[None, :] >= 0)
    s = jnp.einsum('bqhgd,bkhd->bqhgk', qf, kwf[..., 0, :])
    s = s - slope[None, :, :, None] * (qp[:, None] - kpos.astype(f32))[:, None, None, :]
    p_w = _masked_softmax(s, wmask[:, None, None, :])
    o_w = jnp.einsum('bqhgk,bkhd->bqhgd', p_w, kwf[..., 1, :])
    g = jax.nn.sigmoid(gates.astype(f32))
    return g[..., 0:1] * o_c + g[..., 1:2] * o_s + g[..., 2:3] * o_w


def _nsa_prompt(q, kv_c, kv_s, kv_w, gates, pe, w1, b1, w2):
    b, t = q.shape[0], q.shape[1]
    nb = t // BLK
    kvc = _compress(kv_c.reshape(b, nb, BLK, KV_HEADS, 2, HEAD_DIM), pe, w1, b1, w2)
    kvw_pad = jnp.pad(kv_w, ((0, 0), (WINDOW, 0), (0, 0), (0, 0), (0, 0)))
    b5 = jnp.arange(b)[:, None, None, None, None]
    h5 = jnp.arange(KV_HEADS)[None, None, :, None, None]
    offs = jnp.arange(BLK, dtype=jnp.int32)

    def gather(idx):
        rows = idx[..., None] * BLK + offs
        return kv_s[b5, rows, h5]

    nqb = t // Q_BLOCK

    def body(args):
        qb, gb, i = args
        t0 = i * Q_BLOCK
        qpos = t0 + jnp.arange(Q_BLOCK, dtype=jnp.int32)
        kw = lax.dynamic_slice_in_dim(kvw_pad, t0, WINDOW + Q_BLOCK, axis=1)
        kpos = t0 - WINDOW + jnp.arange(WINDOW + Q_BLOCK, dtype=jnp.int32)
        return _nsa_core(qb, gb, qpos, kvc, nb, gather, kw, kpos)

    to_blocks = lambda z: jnp.moveaxis(z.reshape((b, nqb, Q_BLOCK) + z.shape[2:]), 1, 0)
    o = lax.map(body, (to_blocks(q), to_blocks(gates), jnp.arange(nqb, dtype=jnp.int32)))
    return jnp.moveaxis(o, 0, 1).reshape(b, t, C_B)


def _nsa_sample(q, kv_c, kv_s, kv_w, gates, cache_cmp_kv, cache_sel_kv, win_buf, page_table, layer, pe, w1, b1, w2):
    db, t = q.shape[0], q.shape[1]
    n_pages = page_table.shape[1]
    page = cache_cmp_kv.shape[2]
    past = n_pages * page
    pb = page // BLK
    nb_past = past // BLK
    nb_new = -(-t // BLK)
    past_blocks = cache_cmp_kv[layer, page_table].reshape(db, nb_past, BLK, KV_HEADS, 2, HEAD_DIM)
    kvc = _compress(past_blocks, pe, w1, b1, w2)
    n_full = t // BLK
    if n_full > 0:
        new_c = _compress(kv_c[:, :n_full * BLK].reshape(db, n_full, BLK, KV_HEADS, 2, HEAD_DIM), pe, w1, b1, w2)
        kvc = jnp.concatenate([kvc, new_c], axis=1)
    new_rows = jnp.pad(kv_s, ((0, 0), (0, nb_new * BLK - t), (0, 0), (0, 0), (0, 0)))
    n4 = jnp.arange(db)[:, None, None, None]
    n5 = n4[..., None]
    h5 = jnp.arange(KV_HEADS)[None, None, :, None, None]
    offs = jnp.arange(BLK, dtype=jnp.int32)

    def gather(idx):
        ip = jnp.minimum(idx, nb_past - 1)
        phys = page_table[n4, ip // pb]
        rows = (ip % pb)[..., None] * BLK + offs
        g_past = cache_sel_kv[layer, phys[..., None], rows, h5]
        rn = jnp.clip(idx - nb_past, 0, nb_new - 1)[..., None] * BLK + offs
        g_new = new_rows[n5, rn, h5]
        return jnp.where((idx >= nb_past)[..., None, None, None], g_new, g_past)

    wb = win_buf.shape[1]
    kw = jnp.concatenate([win_buf.astype(kv_w.dtype), kv_w], axis=1)
    kpos = past - wb + jnp.arange(wb + t, dtype=jnp.int32)
    qpos = past + jnp.arange(t, dtype=jnp.int32)
    o = _nsa_core(q, gates, qpos, kvc, nb_past + nb_new, gather, kw, kpos)
    return o.reshape(db, t, C_B), kw[:, t:]


def _merge_channel_ple(x, ya, yb, mg, conv_buf, p, w_oa, w_ob, w_out, norm2_g, w_up, conv_w, conv_b, w_down, norm3_g, w_pe, w_pg):
    ga, gb = jnp.split(mg, 2, axis=-1)
    m = jax.nn.sigmoid(ga) * (ya @ w_oa) + jax.nn.sigmoid(gb) * (yb @ w_ob)
    h = x + m @ w_out
    up = _rmsnorm(h, norm2_g) @ w_up
    t = up.shape[1]
    ext = jnp.concatenate([conv_buf.astype(up.dtype), up], axis=1)
    c = conv_b + sum(conv_w[j] * ext[:, j:j + t] for j in range(CONV_W))
    a, gate = jnp.split(c, 2, axis=-1)
    h = h + (jax.nn.silu(a) * gate) @ w_down
    h = h + (p @ w_pe) * jax.nn.sigmoid(_rmsnorm(h, norm3_g) @ w_pg)
    return h, ext[:, t:]


def setup_inputs(seed: int = 0) -> dict:
    key = jax.random.key(seed)
    ks = jax.random.split(key, 48)
    f32 = jnp.float32
    nrm = lambda j, shape, scale: jax.random.normal(ks[j], shape, f32) * scale
    n_pages = PAST_LEN // PAGE_SIZE
    used = DEC_BATCH * n_pages
    n_pool = used + max(1, used // 4)
    win_len = min(WINDOW, PAST_LEN)
    page_table = jax.random.permutation(ks[0], n_pool)[:used].reshape(DEC_BATCH, n_pages).astype(jnp.int32)
    F2 = 2 * D_FF
    return {
        'x_prompt': nrm(1, (BATCH, SEQ, D_MODEL), 1.0),
        'x_sample': nrm(2, (DEC_BATCH, DEC_SEQ, D_MODEL), 1.0),
        'p_prompt': nrm(3, (DEPTH, BATCH, SEQ, D_PLE), 1.0),
        'p_sample': nrm(4, (DEPTH, DEC_BATCH, DEC_SEQ, D_PLE), 1.0),
        'cache_cmp_kv': nrm(5, (DEPTH, n_pool, PAGE_SIZE, KV_HEADS, 2, HEAD_DIM), 1.0),
        'cache_sel_kv': nrm(6, (DEPTH, n_pool, PAGE_SIZE, KV_HEADS, 2, HEAD_DIM), 1.0),
        'page_table': page_table,
        'state_win_kv': nrm(7, (DEPTH, DEC_BATCH, win_len, KV_HEADS, 2, HEAD_DIM), 1.0),
        'state_wkv': nrm(8, (DEPTH, DEC_BATCH, A_HEADS, A_HEAD_DIM, A_HEAD_DIM), 0.3),
        'state_shift': nrm(9, (DEPTH, DEC_BATCH, C_SHIFT), 1.0),
        'state_ffn_conv': nrm(10, (DEPTH, DEC_BATCH, CONV_W - 1, F2), 1.0),
        'norm1_g': 1.0 + nrm(11, (DEPTH, D_MODEL), 0.05),
        'w_in': nrm(12, (DEPTH, D_MODEL, N_IN), D_MODEL ** -0.5),
        'shift_mu': jax.random.uniform(ks[13], (DEPTH, C_SHIFT), f32),
        'rwkv_w0': -2.5 + nrm(14, (DEPTH, C_A), 0.5),
        'rwkv_w2': nrm(15, (DEPTH, R_W, C_A), R_W ** -0.5),
        'rwkv_a0': nrm(16, (DEPTH, C_A), 0.1),
        'rwkv_a2': nrm(17, (DEPTH, R_A, C_A), R_A ** -0.5),
        'rwkv_g2': nrm(18, (DEPTH, R_G, C_A), R_G ** -0.5),
        'rwkv_k_k': 0.85 + nrm(19, (DEPTH, C_A), 0.05),
        'rwkv_k_a': 1.0 + nrm(20, (DEPTH, C_A), 0.05),
        'rwkv_r_k': nrm(21, (DEPTH, A_HEADS, A_HEAD_DIM), 0.1),
        'lnx_g': 1.0 + nrm(22, (DEPTH, C_A), 0.05),
        'lnx_b': nrm(23, (DEPTH, C_A), 0.02),
        'cmp_pe': nrm(24, (DEPTH, 2, BLK, HEAD_DIM), 0.1),
        'cmp_w1': nrm(25, (DEPTH, 2, BLK, HEAD_DIM, D_CMP), (BLK * HEAD_DIM) ** -0.5),
        'cmp_b1': nrm(26, (DEPTH, 2, D_CMP), 0.02),
        'cmp_w2': nrm(27, (DEPTH, 2, D_CMP, HEAD_DIM), D_CMP ** -0.5),
        'w_oa': nrm(28, (DEPTH, C_A, D_MODEL), C_A ** -0.5),
        'w_ob': nrm(29, (DEPTH, C_B, D_MODEL), C_B ** -0.5),
        'w_out': nrm(30, (DEPTH, D_MODEL, D_MODEL), D_MODEL ** -0.5),
        'norm2_g': 1.0 + nrm(31, (DEPTH, D_MODEL), 0.05),
        'w_up': nrm(32, (DEPTH, D_MODEL, F2), D_MODEL ** -0.5),
        'conv_w': nrm(33, (DEPTH, CONV_W, F2), CONV_W ** -0.5),
        'conv_b': nrm(34, (DEPTH, F2), 0.02),
        'w_down': nrm(35, (DEPTH, D_FF, D_MODEL), D_FF ** -0.5),
        'norm3_g': 1.0 + nrm(36, (DEPTH, D_MODEL), 0.05),
        'w_pe': nrm(37, (DEPTH, D_PLE, D_MODEL), D_PLE ** -0.5),
        'w_pg': nrm(38, (DEPTH, D_MODEL, D_MODEL), D_MODEL ** -0.5),
        'final_g': 1.0 + nrm(39, (D_MODEL,), 0.05),
    }


def reference(x_prompt, x_sample, p_prompt, p_sample, cache_cmp_kv, cache_sel_kv, page_table, state_win_kv, state_wkv, state_shift, state_ffn_conv,
              norm1_g, w_in, shift_mu, rwkv_w0, rwkv_w2, rwkv_a0, rwkv_a2, rwkv_g2, rwkv_k_k, rwkv_k_a, rwkv_r_k, lnx_g, lnx_b,
              cmp_pe, cmp_w1, cmp_b1, cmp_w2, w_oa, w_ob, w_out, norm2_g, w_up, conv_w, conv_b, w_down, norm3_g, w_pe, w_pg, final_g):
    hp, hs = x_prompt, x_sample
    b, t = x_prompt.shape[0], x_prompt.shape[1]
    cmp_p, cmp_s, sel_p, sel_s, win_p, win_s = [], [], [], [], [], []
    wkv_p, wkv_s, sh_p, sh_s, cv_p, cv_s = [], [], [], [], [], []
    for i in range(DEPTH):
        rw = (shift_mu[i], rwkv_w0[i], rwkv_w2[i], rwkv_a0[i], rwkv_a2[i], rwkv_g2[i], rwkv_k_k[i], rwkv_k_a[i], rwkv_r_k[i], lnx_g[i], lnx_b[i])
        cw = (cmp_pe[i], cmp_w1[i], cmp_b1[i], cmp_w2[i])
        mw = (w_oa[i], w_ob[i], w_out[i], norm2_g[i], w_up[i], conv_w[i], conv_b[i], w_down[i], norm3_g[i], w_pe[i], w_pg[i])
        proj = _rmsnorm(hp, norm1_g[i]) @ w_in[i]
        pa, q, kvc, kvs, kvw, gates, mg = _split_proj(proj)
        ya, shift_new, s_new = _rwkv_branch(pa, jnp.zeros((b, C_SHIFT), pa.dtype), jnp.zeros((b, A_HEADS, A_HEAD_DIM, A_HEAD_DIM), jnp.float32), *rw)
        yb = _nsa_prompt(q, kvc, kvs, kvw, gates, *cw)
        hp, conv_new = _merge_channel_ple(hp, ya, yb, mg, jnp.zeros((b, CONV_W - 1, 2 * D_FF), hp.dtype), p_prompt[i], *mw)
        cmp_p.append(kvc)
        sel_p.append(kvs)
        win_p.append(kvw[:, t - min(WINDOW, t):])
        wkv_p.append(s_new)
        sh_p.append(shift_new)
        cv_p.append(conv_new)
        proj = _rmsnorm(hs, norm1_g[i]) @ w_in[i]
        pa, q, kvc, kvs, kvw, gates, mg = _split_proj(proj)
        ya, shift_new, s_new = _rwkv_branch(pa, state_shift[i], state_wkv[i], *rw)
        yb, win_new = _nsa_sample(q, kvc, kvs, kvw, gates, cache_cmp_kv, cache_sel_kv, state_win_kv[i], page_table, i, *cw)
        hs, conv_new = _merge_channel_ple(hs, ya, yb, mg, state_ffn_conv[i], p_sample[i], *mw)
        cmp_s.append(kvc)
        sel_s.append(kvs)
        win_s.append(win_new)
        wkv_s.append(s_new)
        sh_s.append(shift_new)
        cv_s.append(conv_new)
    y_prompt = _rmsnorm(hp, final_g)
    y_sample = _rmsnorm(hs, final_g)
    return (y_prompt, y_sample, jnp.stack(cmp_p), jnp.stack(cmp_s), jnp.stack(sel_p), jnp.stack(sel_s), jnp.stack(win_p), jnp.stack(win_s), jnp.stack(wkv_p), jnp.stack(wkv_s), jnp.stack(sh_p), jnp.stack(sh_s), jnp.stack(cv_p), jnp.stack(cv_s))
```

```python
import functools

import jax
import jax.numpy as jnp
from jax import lax
from jax.experimental import pallas as pl
from jax.experimental.pallas import tpu as pltpu

F32 = jnp.float32
BF16 = jnp.bfloat16
HI = lax.Precision.HIGHEST

A_HEADS = 8
A_HEAD_DIM = 64
C_A = A_HEADS * A_HEAD_DIM
R_W = 64
R_A = 64
R_G = 128
C_SHIFT = 3 * C_A + R_W + R_A + R_G
LNX_EPS = 64e-5
Q_HEADS = 8
KV_HEADS = 2
GROUP = Q_HEADS // KV_HEADS
HEAD_DIM = 64
C_B = Q_HEADS * HEAD_DIM
C_KV = KV_HEADS * 2 * HEAD_DIM
BLK = 64
N_SEL = 16
WINDOW = 512
D_CMP = 128
FORCE_SCORE = 1e4
CONV_W = 3
NORM_EPS = 1e-6

LANES = 128
SUBLANES = 8
VMEM_BYTES_V7X = 64 * 1024 * 1024

NEG = -1e30
M_FLOOR = -1e29
KV_LANES = 2 * HEAD_DIM

GATE_PAD = LANES


def _vmem_limit(nbytes):
    return int(min(max(nbytes, 16 * 1024 * 1024), VMEM_BYTES_V7X - 8 * 1024 * 1024))


def _const_spec(shape):
    nd = len(shape)
    return pl.BlockSpec(shape, lambda *_: (0,) * nd, pipeline_mode=pl.Buffered(1))


def _rms(x, g):
    return x * lax.rsqrt(jnp.mean(x * x, axis=-1, keepdims=True) + NORM_EPS) * g


def _dot(a, b, **kw):
    return jnp.dot(a, b, preferred_element_type=F32, **kw)


def _dot_nt(a, b):
    return lax.dot_general(a, b, (((1,), (1,)), ((), ())), preferred_element_type=F32)


def _masked_softmax(s, mask):
    s = jnp.where(mask, s, NEG)
    m = jnp.max(s, axis=-1, keepdims=True)
    m = jnp.where(m > M_FLOOR, m, 0.0)
    e = jnp.exp(s - m)
    return e / jnp.maximum(jnp.sum(e, axis=-1, keepdims=True), 1e-30)


def _slope_col(hk, rows_per_head, pad_rows=0):
    cols = [jnp.full((rows_per_head, 1), 2.0 ** (-(hk * GROUP + g + 1)), F32) for g in range(GROUP)]
    if pad_rows:
        cols.append(jnp.zeros((pad_rows, 1), F32))
    return jnp.concatenate(cols, axis=0)


def _proj_body(x_ref, g_ref, w_ref, *o_refs, segs):
    xb = _rms(x_ref[...], g_ref[...]).astype(BF16)
    it = iter(o_refs)
    for off, width, outs in segs:
        r = _dot(xb, w_ref[:, off:off + width])
        for scale in outs:
            o_ref = next(it)
            o_ref[...] = (r * scale if scale != 1.0 else r).astype(o_ref.dtype)


def _proj(x, g, w_perm, tm):
    rows, d = x.shape
    d_model = d
    o = 0
    segs, shapes = [], []
    for width, outs in ((C_SHIFT, ((F32, 1.0),)), (C_B, ((BF16, HEAD_DIM ** -0.5),)), (C_KV, ((F32, 1.0),)),
                        (C_KV, ((F32, 1.0), (BF16, 1.0))), (C_KV, ((F32, 1.0), (BF16, 1.0))),
                        (GATE_PAD, ((F32, 1.0),)), (2 * d_model, ((F32, 1.0),))):
        segs.append((o, width, tuple(s for _, s in outs)))
        shapes += [(width, dt) for dt, _ in outs]
        o += width
    n_tot = o
    out_bytes = sum(tm * w * jnp.dtype(dt).itemsize for w, dt in shapes)
    vmem = 2 * tm * d * 4 + d * n_tot * 2 + 2 * out_bytes + (8 << 20)
    return pl.pallas_call(
        functools.partial(_proj_body, segs=tuple(segs)),
        grid=(rows // tm,),
        in_specs=[pl.BlockSpec((tm, d), lambda i: (i, 0)), _const_spec((1, d)), _const_spec((d, n_tot))],
        out_specs=[pl.BlockSpec((tm, w), lambda i: (i, 0)) for w, _ in shapes],
        out_shape=[jax.ShapeDtypeStruct((rows, w), dt) for w, dt in shapes],
        compiler_params=pltpu.CompilerParams(dimension_semantics=("arbitrary",), vmem_limit_bytes=_vmem_limit(vmem)),
        name="proj",
    )(x, g, w_perm)


def _prep_body(pa_ref, prev_ref, mu_ref, w0_ref, w2_ref, a0_ref, a2_ref, g2_ref, kk_ref, ka_ref, rk_ref, bd_ref,
               r_o, w_o, k_o, v_o, a_o, b_o, g_o, bonus_o, *, seq_mode):
    pf = pa_ref[...]
    if seq_mode:
        i = pl.program_id(0)
        first = jnp.where(i > 0, prev_ref[SUBLANES - 1:SUBLANES, :], 0.0)
        rid = lax.broadcasted_iota(jnp.int32, pf.shape, 0)
        prev = jnp.where(rid == 0, first, pltpu.roll(pf, 1, 0))
    else:
        prev = prev_ref[...]
    xs = pf + (prev - pf) * mu_ref[...]
    r = xs[:, 0:C_A]
    k = xs[:, C_A:2 * C_A]
    v = xs[:, 2 * C_A:3 * C_A]
    o = 3 * C_A
    wd = xs[:, o:o + R_W]
    ad = xs[:, o + R_W:o + R_W + R_A]
    gd = xs[:, o + R_W + R_A:]
    nz = -(w0_ref[...] + _dot(jnp.tanh(wd), w2_ref[...], precision=HI))
    softplus = jnp.maximum(nz, 0.0) + jnp.log(1.0 + jnp.exp(-jnp.abs(nz)))
    w = -softplus - 0.5
    a = jax.nn.sigmoid(a0_ref[...] + _dot(ad, a2_ref[...], precision=HI))
    g = _dot(jax.nn.sigmoid(gd), g2_ref[...], precision=HI)
    kk = k * kk_ref[...]
    ss = _dot(kk * kk, bd_ref[...], precision=HI)
    kkn = kk / jnp.maximum(jnp.sqrt(ss), 1e-12)
    k2 = k * (1.0 + (a - 1.0) * ka_ref[...])
    r_o[...] = r
    w_o[...] = jnp.exp(-jnp.exp(w))
    k_o[...] = k2
    v_o[...] = v
    a_o[...] = -kkn
    b_o[...] = kkn * a
    g_o[...] = g
    bonus_o[...] = _dot(r * k2 * rk_ref[...], bd_ref[...], precision=HI) * v


def _rwkv_prep(pa, prev, rw, bd, tm, seq_mode):
    rows = pa.shape[0]
    mu, w0, w2, a0, a2, g2, k_k, k_a, r_k = rw
    if seq_mode:
        tb = tm // SUBLANES
        prev_spec = pl.BlockSpec((SUBLANES, C_SHIFT), lambda i: (jnp.maximum(i * tb - 1, 0), 0))
        prev = pa
    else:
        prev_spec = pl.BlockSpec((tm, C_SHIFT), lambda i: (i, 0))
    row = lambda z: z.reshape(1, -1)
    consts = [row(mu), row(w0), w2, row(a0), a2, g2, row(k_k), row(k_a), row(r_k), bd]
    vmem = 4 * tm * C_SHIFT * 4 + 2 * 8 * tm * C_A * 4 + 16 * tm * C_A * 4 + (8 << 20)
    return pl.pallas_call(
        functools.partial(_prep_body, seq_mode=seq_mode),
        grid=(rows // tm,),
        in_specs=[pl.BlockSpec((tm, C_SHIFT), lambda i: (i, 0)), prev_spec] + [_const_spec(c.shape) for c in consts],
        out_specs=[pl.BlockSpec((tm, C_A), lambda i: (i, 0))] * 8,
        out_shape=[jax.ShapeDtypeStruct((rows, C_A), F32)] * 8,
        compiler_params=pltpu.CompilerParams(dimension_semantics=("arbitrary",), vmem_limit_bytes=_vmem_limit(vmem)),
        name="rwkv_prep",
    )(pa, prev, *consts)


N_PAIR = A_HEADS // 2


def _scan_body(r_ref, w_ref, k_ref, v_ref, a_ref, b_ref, s0_ref, y_ref, so_ref, st_ref, *, tc, nc):
    c = pl.program_id(1)

    @pl.when(c == 0)
    def _():
        for p in range(N_PAIR):
            st_ref[p] = jnp.concatenate([s0_ref[0, 2 * p], s0_ref[0, 2 * p + 1]], axis=-1)

    shape = (A_HEAD_DIM, LANES)
    lane = lax.broadcasted_iota(jnp.int32, shape, 1)
    sub = lax.broadcasted_iota(jnp.int32, shape, 0)
    lo = lane < A_HEAD_DIM
    diag = (lane & (A_HEAD_DIM - 1)) == sub

    def seg_sum(x):
        s_lo = jnp.sum(jnp.where(lo, x, 0.0), axis=-1, keepdims=True)
        s_hi = jnp.sum(jnp.where(lo, 0.0, x), axis=-1, keepdims=True)
        return jnp.where(lo, s_lo, s_hi)

    grp = min(SUBLANES, tc)

    def token_group(gi, carry):
        base = pl.multiple_of(gi * grp, grp)
        for p in range(N_PAIR):
            sl = slice(LANES * p, LANES * (p + 1))
            rt, wt, kt, vt, at, bt = (ref[0, pl.ds(base, grp), sl] for ref in (r_ref, w_ref, k_ref, v_ref, a_ref, b_ref))
            s = st_ref[p]
            ys = []
            for j in range(grp):
                row = lambda z: z[j:j + 1, :]
                sa = seg_sum(s * row(at))
                v_col = seg_sum(jnp.where(diag, row(vt), 0.0))
                s = s * row(wt) + sa * row(bt) + v_col * row(kt)
                y_col = seg_sum(s * row(rt))
                ys.append(jnp.sum(jnp.where(diag, y_col, 0.0), axis=0, keepdims=True))
            st_ref[p] = s
            y_ref[0, pl.ds(base, grp), sl] = jnp.concatenate(ys, axis=0) if grp > 1 else ys[0]
        return carry

    lax.fori_loop(0, tc // grp, token_group, 0)

    @pl.when(c == nc - 1)
    def _():
        for p in range(N_PAIR):
            s = st_ref[p]
            so_ref[0, 2 * p] = s[:, :A_HEAD_DIM]
            so_ref[0, 2 * p + 1] = s[:, A_HEAD_DIM:]


def _rwkv_scan(seqs, s0, tc):
    b, t, _ = seqs[0].shape
    nc = t // tc
    seq_spec = pl.BlockSpec((1, tc, C_A), lambda i, c: (i, c, 0))
    st_spec = pl.BlockSpec((1, A_HEADS, A_HEAD_DIM, A_HEAD_DIM), lambda i, c: (i, 0, 0, 0))
    return pl.pallas_call(
        functools.partial(_scan_body, tc=tc, nc=nc),
        grid=(b, nc),
        in_specs=[seq_spec] * 6 + [st_spec],
        out_specs=[seq_spec, st_spec],
        out_shape=[jax.ShapeDtypeStruct((b, t, C_A), F32), jax.ShapeDtypeStruct(s0.shape, F32)],
        scratch_shapes=[pltpu.VMEM((N_PAIR, A_HEAD_DIM, LANES), F32)],
        compiler_params=pltpu.CompilerParams(dimension_semantics=("arbitrary", "arbitrary")),
        name="rwkv_scan",
    )(*seqs, s0)


def _gelu_tanh(x):
    return 0.5 * x * (1.0 + jnp.tanh(0.7978845608028654 * (x + 0.044715 * (x * x * x))))


def _compress_body(x_ref, w_ref, b_ref, w2_ref, o_ref, acc_ref, *, nk, mlp):
    kk = pl.program_id(1)

    @pl.when(kk == 0)
    def _():
        acc_ref[...] = jnp.zeros_like(acc_ref)

    acc_ref[...] += _dot(x_ref[...].astype(BF16), w_ref[...])

    @pl.when(kk == nk - 1)
    def _():
        h = acc_ref[...] + b_ref[...]
        if mlp:
            o_ref[...] = _dot(_gelu_tanh(h).astype(BF16), w2_ref[...])
        else:
            o_ref[...] = h


def _compress(x2, w_big, bias, w2_big, tm, tk, mlp=True):
    m, kdim = x2.shape
    nh = w_big.shape[1]
    n_out = w2_big.shape[1] if mlp else nh
    nk = kdim // tk
    vmem = 2 * tm * tk * 4 + 2 * tk * nh * 2 + 3 * tm * nh * 4 + 2 * tm * n_out * 4 + (8 << 20)
    return pl.pallas_call(
        functools.partial(_compress_body, nk=nk, mlp=mlp),
        grid=(m // tm, nk),
        in_specs=[pl.BlockSpec((tm, tk), lambda i, k: (i, k)), pl.BlockSpec((tk, nh), lambda i, k: (k, 0)),
                  _const_spec(bias.shape), _const_spec(w2_big.shape)],
        out_specs=pl.BlockSpec((tm, n_out), lambda i, k: (i, 0)),
        out_shape=jax.ShapeDtypeStruct((m, n_out), F32),
        scratch_shapes=[pltpu.VMEM((tm, nh), F32)],
        compiler_params=pltpu.CompilerParams(dimension_semantics=("arbitrary", "arbitrary"),
                                             vmem_limit_bytes=_vmem_limit(vmem)),
        name="nsa_compress",
    )(x2, w_big, bias, w2_big)


def _topk_mask(s, ids, k):
    sel = jnp.zeros(s.shape, F32)
    for _ in range(k):
        m = jnp.max(s, axis=-1, keepdims=True)
        pick = jnp.min(jnp.where(s == m, ids, 1e9), axis=-1, keepdims=True)
        hit = ids == pick
        sel = jnp.where(hit, 1.0, sel)
        s = jnp.where(hit, -jnp.inf, s)
    return sel


def _head_rows(q_ref, hk, pad_rows=0):
    parts = [q_ref[:, (hk * GROUP + g) * HEAD_DIM:(hk * GROUP + g + 1) * HEAD_DIM] for g in range(GROUP)]
    if pad_rows:
        parts.append(jnp.zeros((pad_rows, HEAD_DIM), parts[0].dtype))
    qh = jnp.concatenate(parts, axis=0)
    return jnp.concatenate([qh, jnp.zeros_like(qh)], axis=-1)


def _nsa_prompt_body(q_ref, gt_ref, kc_ref, ks_ref, kw_ref, em_ref, o_ref, *, tq, tk, nbc):
    i = pl.program_id(0)
    t0 = i * tq
    rows = GROUP * tq
    qpos1 = t0 + lax.broadcasted_iota(jnp.int32, (tq, 1), 0)
    qpos = jnp.concatenate([qpos1] * GROUP, axis=0)
    qposf = qpos.astype(F32)
    blk = lax.broadcasted_iota(jnp.int32, (1, nbc), 1)
    blkf = blk.astype(F32)
    cur = qpos1 // BLK
    gt = jax.nn.sigmoid(gt_ref[...])
    bpt = tk // BLK
    pieces = []
    for hk in range(KV_HEADS):
        hl = slice(hk * KV_LANES, (hk + 1) * KV_LANES)
        qh = _head_rows(q_ref, hk)
        slope = _slope_col(hk, tq)
        kcb = kc_ref[:, hl].astype(BF16)
        cpos = blk * BLK + (BLK - 1)
        s = _dot_nt(qh, kcb) - slope * (qposf - cpos.astype(F32))
        p_c = _masked_softmax(s, cpos <= qpos)
        o_c = _dot(p_c.astype(BF16), kcb)[:, HEAD_DIM:]
        score = p_c[0:tq]
        for g in range(1, GROUP):
            score = score + p_c[g * tq:(g + 1) * tq]
        forced = (blk == 0) | (blk == cur) | (blk == cur - 1)
        sc = jnp.where(forced, FORCE_SCORE, score)
        sc = jnp.where(blk > cur, -1.0, sc)
        sel = _topk_mask(sc, blkf, min(N_SEL, nbc))
        not_sel = jnp.where(blk <= cur, 1.0 - sel, 1.0)

        def sel_step(j, carry):
            m, l, acc = carry
            k0 = pl.multiple_of(j * tk, tk)
            kv = ks_ref[pl.ds(k0, tk), hl]
            kpos = k0 + lax.broadcasted_iota(jnp.int32, (1, tk), 1)
            ns_j = jnp.where(blk // bpt == j, not_sel, 0.0).astype(BF16)
            mb = _dot(ns_j, em_ref[...])
            bias = jnp.concatenate([mb] * GROUP, axis=0)
            s = _dot_nt(qh, kv) + slope * (kpos - t0).astype(F32) + bias
            s = jnp.where(kpos <= qpos, s, NEG)
            m_new = jnp.maximum(m, jnp.max(s, axis=-1, keepdims=True))
            alpha = jnp.exp(m - m_new)
            p = jnp.exp(s - m_new)
            l = alpha * l + jnp.sum(p, axis=-1, keepdims=True)
            acc = alpha * acc + _dot(p.astype(BF16), kv)
            return m_new, l, acc

        n_tiles = (t0 + tq + tk - 1) // tk
        init = (jnp.full((rows, 1), M_FLOOR, F32), jnp.zeros((rows, 1), F32), jnp.zeros((rows, KV_LANES), F32))
        _, l, acc = lax.fori_loop(0, n_tiles, sel_step, init)
        o_s = (acc / jnp.maximum(l, 1e-30))[:, HEAD_DIM:]

        wl = WINDOW + tq
        ws = pl.multiple_of(jnp.maximum(t0 - WINDOW, 0), tq)
        kvw = kw_ref[pl.ds(ws, wl), hl]
        kposw = ws + lax.broadcasted_iota(jnp.int32, (1, wl), 1)
        s = _dot_nt(qh, kvw) - slope * (qposf - kposw.astype(F32))
        p_w = _masked_softmax(s, (kposw <= qpos) & (qpos - kposw < WINDOW))
        o_w = _dot(p_w.astype(BF16), kvw)[:, HEAD_DIM:]

        for g in range(GROUP):
            c0 = (hk * GROUP + g) * 3
            rs = slice(g * tq, (g + 1) * tq)
            pieces.append(gt[:, c0:c0 + 1] * o_c[rs] + gt[:, c0 + 1:c0 + 2] * o_s[rs] + gt[:, c0 + 2:c0 + 3] * o_w[rs])
    o_ref[...] = jnp.concatenate(pieces, axis=-1)


def _nsa_prompt(q, gates, kvc, ks, kw, tq, tk):
    t = q.shape[0]
    nbc = kvc.shape[0]
    bpt = tk // BLK
    em = jnp.where((jnp.arange(nbc)[:, None] % bpt) == (jnp.arange(tk)[None, :] // BLK), NEG, 0.0).astype(BF16)
    vmem = 2 * t * C_KV * 2 + nbc * C_KV * 4 + 24 * GROUP * tq * max(tk, WINDOW + tq) * 4 + (8 << 20)
    return pl.pallas_call(
        functools.partial(_nsa_prompt_body, tq=tq, tk=tk, nbc=nbc),
        grid=(t // tq,),
        in_specs=[pl.BlockSpec((tq, C_B), lambda i: (i, 0)), pl.BlockSpec((tq, GATE_PAD), lambda i: (i, 0)),
                  _const_spec(kvc.shape), _const_spec(ks.shape), _const_spec(kw.shape), _const_spec(em.shape)],
        out_specs=pl.BlockSpec((tq, C_B), lambda i: (i, 0)),
        out_shape=jax.ShapeDtypeStruct((t, C_B), F32),
        compiler_params=pltpu.CompilerParams(dimension_semantics=("arbitrary",), vmem_limit_bytes=_vmem_limit(vmem)),
        name="nsa_prompt",
    )(q, gates, kvc, ks, kw, em)


PAGE_BLOCKS = 2
Q_ROWS = SUBLANES


def _samp_cmp_body(pt_ref, q_ref, pool_ref, oc_ref, idx_ref, kbuf, sem, *, n_pages, past):
    n = pl.program_id(0)

    def page_copy(jp):
        return pltpu.make_async_copy(pool_ref.at[pl.ds(pt_ref[n, jp], 1), :], kbuf.at[pl.ds(jp, 1), :], sem)

    def start(jp, c):
        page_copy(jp).start()
        return c

    def wait(jp, c):
        page_copy(jp).wait()
        return c

    lax.fori_loop(0, n_pages, start, 0)
    lax.fori_loop(0, n_pages, wait, 0)

    nb_past = n_pages * PAGE_BLOCKS
    n_all = nb_past + 1
    qpos = past
    cur = qpos // BLK
    width = PAGE_BLOCKS * n_pages
    lane = lax.broadcasted_iota(jnp.int32, (1, width), 1)
    bid = jnp.where(lane < n_pages, PAGE_BLOCKS * lane, PAGE_BLOCKS * (lane - n_pages) + 1)
    cpos = bid * BLK + (BLK - 1)
    wide = 2 * width
    lane2 = lax.broadcasted_iota(jnp.int32, (1, wide), 1)
    bid2 = jnp.concatenate([bid, nb_past + lane], axis=-1)
    bid2f = bid2.astype(F32)
    for hk in range(KV_HEADS):
        qh = _head_rows(q_ref.at[0], hk, pad_rows=Q_ROWS - GROUP)
        slope = _slope_col(hk, 1, pad_rows=Q_ROWS - GROUP)
        kc_e = kbuf[:, hk * KV_LANES:(hk + 1) * KV_LANES].astype(BF16)
        kc_o = kbuf[:, C_KV + hk * KV_LANES:C_KV + (hk + 1) * KV_LANES].astype(BF16)
        s = jnp.concatenate([_dot_nt(qh, kc_e), _dot_nt(qh, kc_o)], axis=-1)
        s = s - slope * (qpos - cpos).astype(F32)
        p_c = _masked_softmax(s, cpos <= qpos)
        oc_ref[0, hk] = _dot(p_c[:, :n_pages].astype(BF16), kc_e) + _dot(p_c[:, n_pages:].astype(BF16), kc_o)
        score = jnp.sum(p_c[0:GROUP], axis=0, keepdims=True)
        sc = jnp.concatenate([score, jnp.zeros((1, width), F32)], axis=-1)
        forced = (bid2 == 0) | (bid2 == cur) | (bid2 == cur - 1)
        sc = jnp.where(forced, FORCE_SCORE, sc)
        sc = jnp.where(bid2 > cur, -1.0, sc)
        sc = jnp.where(bid2 < n_all, sc, -jnp.inf)
        res = jnp.zeros((1, wide), F32)
        for it in range(N_SEL):
            m = jnp.max(sc, axis=-1, keepdims=True)
            pick = jnp.min(jnp.where(sc == m, bid2f, 1e9), axis=-1, keepdims=True)
            res = jnp.where(lane2 == it, pick, res)
            sc = jnp.where(bid2f == pick, -jnp.inf, sc)
        idx_ref[0, hk] = jnp.broadcast_to(res[:, :LANES], (SUBLANES, LANES)).astype(jnp.int32)


def _samp_cmp(page_table, q3, pool2, past):
    db, n_pages = page_table.shape
    grid_spec = pltpu.PrefetchScalarGridSpec(
        num_scalar_prefetch=1,
        grid=(db,),
        in_specs=[pl.BlockSpec((1, 1, C_B), lambda n, pt: (n, 0, 0)), pl.BlockSpec(memory_space=pl.ANY)],
        out_specs=[pl.BlockSpec((1, KV_HEADS, Q_ROWS, KV_LANES), lambda n, pt: (n, 0, 0, 0)),
                   pl.BlockSpec((1, KV_HEADS, SUBLANES, LANES), lambda n, pt: (n, 0, 0, 0))],
        scratch_shapes=[pltpu.VMEM((n_pages, PAGE_BLOCKS * C_KV), F32), pltpu.SemaphoreType.DMA(())],
    )
    return pl.pallas_call(
        functools.partial(_samp_cmp_body, n_pages=n_pages, past=past),
        grid_spec=grid_spec,
        out_shape=[jax.ShapeDtypeStruct((db, KV_HEADS, Q_ROWS, KV_LANES), F32),
                   jax.ShapeDtypeStruct((db, KV_HEADS, SUBLANES, LANES), jnp.int32)],
        compiler_params=pltpu.CompilerParams(dimension_semantics=("arbitrary",)),
        name="nsa_sample_cmp",
    )(page_table, q3, pool2)


WIN_PAD = LANES


def _samp_sel_body(pt_ref, idx_ref, q_ref, gt_ref, oc_ref, ks_ref, kwn_ref, win_ref, cache_ref,
                   o_ref, wino_ref, sbuf, wbuf, sems, *, nb_past, past, wb):
    n = pl.program_id(0)
    qpos = past

    def block_copy(hk, s, ib):
        page = pt_ref[n, ib // PAGE_BLOCKS]
        r0 = pl.multiple_of((ib % PAGE_BLOCKS) * BLK, BLK)
        return pltpu.make_async_copy(cache_ref.at[page, pl.ds(r0, BLK), pl.ds(hk * KV_LANES, KV_LANES)],
                                     sbuf.at[hk, s], sems.at[hk * N_SEL + s])

    for hk in range(KV_HEADS):
        for s in range(N_SEL):
            ib = idx_ref[n, hk * N_SEL + s]

            @pl.when(ib < nb_past)
            def _():
                block_copy(hk, s, ib).start()

            @pl.when(ib >= nb_past)
            def _():
                sbuf[hk, s] = jnp.zeros((BLK, KV_LANES), F32)
                sbuf[hk, s, 0:1, :] = ks_ref[0, :, hk * KV_LANES:(hk + 1) * KV_LANES]

    wbuf[0:wb, :] = win_ref[0]
    wbuf[wb:wb + WIN_PAD, :] = jnp.zeros((WIN_PAD, C_KV), F32)
    wbuf[wb:wb + 1, :] = kwn_ref[0]
    wino_ref[0] = wbuf[pl.ds(1, wb), :]

    for hk in range(KV_HEADS):
        for s in range(N_SEL):
            ib = idx_ref[n, hk * N_SEL + s]

            @pl.when(ib < nb_past)
            def _():
                block_copy(hk, s, ib).wait()

    gt = jax.nn.sigmoid(gt_ref[0])
    nk = N_SEL * BLK
    lane = lax.broadcasted_iota(jnp.int32, (1, nk), 1)
    wl = wb + WIN_PAD
    wrow = lax.broadcasted_iota(jnp.int32, (1, wl), 1)
    kposw = past - wb + wrow
    wmask = (wrow <= wb) & (kposw <= qpos) & (qpos - kposw < WINDOW) & (kposw >= 0)
    pieces = []
    for hk in range(KV_HEADS):
        hl = slice(hk * KV_LANES, (hk + 1) * KV_LANES)
        qh = _head_rows(q_ref.at[0], hk, pad_rows=Q_ROWS - GROUP)
        slope = _slope_col(hk, 1, pad_rows=Q_ROWS - GROUP)
        base = jnp.zeros((1, nk), jnp.int32)
        for s in range(N_SEL):
            base = jnp.where(lane // BLK == s, idx_ref[n, hk * N_SEL + s] * BLK, base)
        spos = base + lane % BLK
        kv = sbuf[hk].reshape(nk, KV_LANES).astype(BF16)
        s_s = _dot_nt(qh, kv) - slope * (qpos - spos).astype(F32)
        p_s = _masked_softmax(s_s, spos <= qpos)
        o_s = _dot(p_s.astype(BF16), kv)[:, HEAD_DIM:]
        kvw = wbuf[:, hl].astype(BF16)
        s_w = _dot_nt(qh, kvw) - slope * (qpos - kposw).astype(F32)
        p_w = _masked_softmax(s_w, wmask)
        o_w = _dot(p_w.astype(BF16), kvw)[:, HEAD_DIM:]
        o_c = oc_ref[0, hk][:, HEAD_DIM:]
        for g in range(GROUP):
            c0 = (hk * GROUP + g) * 3
            pieces.append(gt[:, c0:c0 + 1] * o_c[g:g + 1] + gt[:, c0 + 1:c0 + 2] * o_s[g:g + 1]
                          + gt[:, c0 + 2:c0 + 3] * o_w[g:g + 1])
    o_ref[0] = jnp.concatenate(pieces, axis=-1)


def _samp_sel(page_table, idx, q3, gates3, oc, ks3, kw3, win, cache_sel, past):
    db, n_pages = page_table.shape
    wb = win.shape[1]
    nb_past = n_pages * PAGE_BLOCKS
    row3 = lambda w: pl.BlockSpec((1, 1, w), lambda n, pt, ix: (n, 0, 0))
    grid_spec = pltpu.PrefetchScalarGridSpec(
        num_scalar_prefetch=2,
        grid=(db,),
        in_specs=[row3(C_B), row3(GATE_PAD),
                  pl.BlockSpec((1, KV_HEADS, Q_ROWS, KV_LANES), lambda n, pt, ix: (n, 0, 0, 0)),
                  row3(C_KV), row3(C_KV),
                  pl.BlockSpec((1, wb, C_KV), lambda n, pt, ix: (n, 0, 0)),
                  pl.BlockSpec(memory_space=pl.ANY)],
        out_specs=[row3(C_B), pl.BlockSpec((1, wb, C_KV), lambda n, pt, ix: (n, 0, 0))],
        scratch_shapes=[pltpu.VMEM((KV_HEADS, N_SEL, BLK, KV_LANES), F32), pltpu.VMEM((wb + WIN_PAD, C_KV), F32),
                        pltpu.SemaphoreType.DMA((KV_HEADS * N_SEL,))],
    )
    return pl.pallas_call(
        functools.partial(_samp_sel_body, nb_past=nb_past, past=past, wb=wb),
        grid_spec=grid_spec,
        out_shape=[jax.ShapeDtypeStruct((db, 1, C_B), F32), jax.ShapeDtypeStruct((db, wb, C_KV), F32)],
        compiler_params=pltpu.CompilerParams(dimension_semantics=("arbitrary",)),
        name="nsa_sample_sel",
    )(page_table, idx, q3, gates3, oc, ks3, kw3, win, cache_sel)


FF_CHUNK = 256


def _merge_body(x_ref, y_ref, bonus_ref, g_ref, yb_ref, mg_ref, p_ref, cbuf_ref,
                lng_ref, lnb_ref, bd_ref, woa_ref, wob_ref, wout_ref, n2_ref, wup_ref, cw_ref, cb_ref, wdn_ref,
                n3_ref, wpe_ref, wpg_ref, fg_ref, o_ref, cnew_ref, carry_ref, *, seq_mode, final, d_ff, tm):
    i = pl.program_id(0)
    d_model = x_ref.shape[1]
    y = y_ref[...]
    inv = 1.0 / A_HEAD_DIM
    mean = _dot(y, bd_ref[...], precision=HI) * inv
    d = y - mean
    var = _dot(d * d, bd_ref[...], precision=HI) * inv
    ya = (d * lax.rsqrt(var + LNX_EPS) * lng_ref[...] + lnb_ref[...] + bonus_ref[...]) * g_ref[...]
    mg = mg_ref[...]
    m = (jax.nn.sigmoid(mg[:, :d_model]) * _dot(ya.astype(BF16), woa_ref[...])
         + jax.nn.sigmoid(mg[:, d_model:]) * _dot(yb_ref[...].astype(BF16), wob_ref[...]))
    h = x_ref[...] + _dot(m.astype(BF16), wout_ref[...])
    xn = _rms(h, n2_ref[...]).astype(BF16)

    if seq_mode:
        @pl.when(i == 0)
        def _():
            carry_ref[...] = jnp.zeros_like(carry_ref)
        rid = lax.broadcasted_iota(jnp.int32, (tm, FF_CHUNK), 0)

    acc = jnp.zeros((tm, d_model), F32)
    for c in range(d_ff // FF_CHUNK):
        parts = []
        for half in range(2):
            cs = slice(half * d_ff + c * FF_CHUNK, half * d_ff + (c + 1) * FF_CHUNK)
            up = _dot(xn, wup_ref[:, cs])
            if seq_mode:
                t1 = carry_ref[SUBLANES - 1:SUBLANES, cs]
                t2 = carry_ref[SUBLANES - 2:SUBLANES - 1, cs]
                up1 = jnp.where(rid == 0, t1, pltpu.roll(up, 1, 0))
                up2 = jnp.where(rid == 0, t2, jnp.where(rid == 1, t1, pltpu.roll(up, 2, 0)))
                carry_ref[:, cs] = up[tm - SUBLANES:, :]
            else:
                up2 = cbuf_ref[:, cs]
                up1 = cbuf_ref[:, 2 * d_ff + cs.start:2 * d_ff + cs.stop]
                cnew_ref[:, cs] = up1
                cnew_ref[:, 2 * d_ff + cs.start:2 * d_ff + cs.stop] = up
            parts.append(cb_ref[:, cs] + cw_ref[0:1, cs] * up2 + cw_ref[1:2, cs] * up1 + cw_ref[2:3, cs] * up)
        a, gate = parts
        act = (a * jax.nn.sigmoid(a) * gate).astype(BF16)
        acc = acc + _dot(act, wdn_ref[c * FF_CHUNK:(c + 1) * FF_CHUNK, :])
    if seq_mode:
        cnew_ref[...] = carry_ref[...]
    h = h + acc
    pe = _dot(p_ref[...].astype(BF16), wpe_ref[...])
    h = h + pe * jax.nn.sigmoid(_dot(_rms(h, n3_ref[...]).astype(BF16), wpg_ref[...]))
    o_ref[...] = _rms(h, fg_ref[...]) if final else h


def _merge(x, y, bonus, g, yb, mg, p, cbuf, consts, tm, seq_mode, final):
    rows, d_model = x.shape
    d_ff = consts[10].shape[0]
    f2 = 2 * d_ff
    rowspec = lambda w: pl.BlockSpec((tm, w), lambda i: (i, 0))
    if seq_mode:
        cbuf_spec = _const_spec(cbuf.shape)
        cnew_shape, cnew_spec = (SUBLANES, f2), pl.BlockSpec((SUBLANES, f2), lambda i: (0, 0))
    else:
        cbuf_spec = rowspec(2 * f2)
        cnew_shape, cnew_spec = (rows, 2 * f2), rowspec(2 * f2)
    wbytes = sum(int(c.size) * c.dtype.itemsize for c in consts)
    act = tm * (d_model * 3 + C_A * 4 + 256) * 4 + (0 if seq_mode else 2 * tm * 2 * f2 * 4)
    vmem = wbytes + 2 * act + 8 * tm * d_model * 4 + (8 << 20)
    return pl.pallas_call(
        functools.partial(_merge_body, seq_mode=seq_mode, final=final, d_ff=d_ff, tm=tm),
        grid=(rows // tm,),
        in_specs=[rowspec(d_model), rowspec(C_A), rowspec(C_A), rowspec(C_A), rowspec(C_B), rowspec(2 * d_model),
                  rowspec(p.shape[1]), cbuf_spec] + [_const_spec(c.shape) for c in consts],
        out_specs=[rowspec(d_model), cnew_spec],
        out_shape=[jax.ShapeDtypeStruct((rows, d_model), F32), jax.ShapeDtypeStruct(cnew_shape, F32)],
        scratch_shapes=[pltpu.VMEM((SUBLANES, f2), F32)],
        compiler_params=pltpu.CompilerParams(dimension_semantics=("arbitrary",), vmem_limit_bytes=_vmem_limit(vmem)),
        name="merge_ffn",
    )(x, y, bonus, g, yb, mg, p, cbuf, *consts)


def _pick_tile(n, target):
    t = min(n, target)
    while n % t:
        t //= 2
    return t


def _head_block_diag():
    h = jnp.arange(C_A) // A_HEAD_DIM
    return (h[:, None] == h[None, :]).astype(F32)


def _compress_weights(pe, w1, b1, w2):
    eye = jnp.eye(KV_HEADS, dtype=F32)
    eye_c = jnp.eye(2, dtype=F32)
    w_big = jnp.einsum('crdf,hg,ce->rhcdgef', w1, eye, eye_c).reshape(BLK * C_KV, KV_HEADS * 2 * D_CMP)
    w2_big = jnp.einsum('cfd,hg,ce->hcfged', w2, eye, eye_c).reshape(KV_HEADS * 2 * D_CMP, C_KV)
    pe_row = jnp.broadcast_to(jnp.transpose(pe, (1, 0, 2))[:, None], (BLK, KV_HEADS, 2, HEAD_DIM)).reshape(1, -1)
    b1_row = jnp.broadcast_to(b1[None], (KV_HEADS, 2, D_CMP)).reshape(1, -1)
    return w_big.astype(BF16), w2_big.astype(BF16), pe_row, b1_row


def kernel(x_prompt, x_sample, p_prompt, p_sample, cache_cmp_kv, cache_sel_kv, page_table, state_win_kv, state_wkv, state_shift, state_ffn_conv, norm1_g, w_in, shift_mu, rwkv_w0, rwkv_w2, rwkv_a0, rwkv_a2, rwkv_g2, rwkv_k_k, rwkv_k_a, rwkv_r_k, lnx_g, lnx_b, cmp_pe, cmp_w1, cmp_b1, cmp_w2, w_oa, w_ob, w_out, norm2_g, w_up, conv_w, conv_b, w_down, norm3_g, w_pe, w_pg, final_g):
    depth = w_in.shape[0]
    b, t, d_model = x_prompt.shape
    db, dt, _ = x_sample.shape
    n_pool, page = cache_cmp_kv.shape[1], cache_cmp_kv.shape[2]
    n_pages = page_table.shape[1]
    past = n_pages * page
    d_ff = w_down.shape[1]
    f2 = 2 * d_ff
    wb = state_win_kv.shape[2]
    assert b == 1 and dt == 1 and page == PAGE_BLOCKS * BLK
    assert t % BLK == 0 and t >= WINDOW + 128 and d_ff % FF_CHUNK == 0 and wb == WINDOW and past >= WINDOW
    assert N_SEL <= LANES and n_pages * PAGE_BLOCKS + 1 >= N_SEL

    bd = _head_block_diag()
    hp = x_prompt.reshape(t, d_model)
    hs = x_sample.reshape(db, d_model)
    outs = [[] for _ in range(12)]
    for i in range(depth):
        o_g = C_SHIFT + C_B + 3 * C_KV
        w_perm = jnp.concatenate(
            [w_in[i][:, :o_g], jnp.pad(w_in[i][:, o_g:o_g + 3 * Q_HEADS], ((0, 0), (0, GATE_PAD - 3 * Q_HEADS))),
             w_in[i][:, o_g + 3 * Q_HEADS:]], axis=1).astype(BF16)
        g1 = norm1_g[i].reshape(1, -1)
        rw = (shift_mu[i], rwkv_w0[i], rwkv_w2[i], rwkv_a0[i], rwkv_a2[i], rwkv_g2[i], rwkv_k_k[i], rwkv_k_a[i],
              rwkv_r_k[i].reshape(-1))
        w_big, w2_big, pe_row, b1_row = _compress_weights(cmp_pe[i], cmp_w1[i], cmp_b1[i], cmp_w2[i])
        zero_bias = jnp.zeros_like(b1_row)
        cmp_bias = _compress(jnp.broadcast_to(pe_row, (SUBLANES, pe_row.shape[1])), w_big, zero_bias, w2_big,
                             SUBLANES, 2048, mlp=False)[0:1] + b1_row
        row = lambda z: z.reshape(1, -1)
        mconsts = [row(lnx_g[i]), row(lnx_b[i]), bd, w_oa[i].astype(BF16), w_ob[i].astype(BF16),
                   w_out[i].astype(BF16), row(norm2_g[i]), w_up[i].astype(BF16), conv_w[i], row(conv_b[i]),
                   w_down[i].astype(BF16), row(norm3_g[i]), w_pe[i].astype(BF16), w_pg[i].astype(BF16), row(final_g)]
        final = i == depth - 1

        pa, q, kvc, kvs, kvs16, kvw, kvw16, gates, mg = _proj(hp, g1, w_perm, _pick_tile(t, 256))
        seqs = _rwkv_prep(pa, None, rw, bd, _pick_tile(t, 512), True)
        r_, w_, k_, v_, a_, b_, g_, bonus = seqs
        y, s_new = _rwkv_scan([z.reshape(1, t, C_A) for z in (r_, w_, k_, v_, a_, b_)],
                              jnp.zeros((1, A_HEADS, A_HEAD_DIM, A_HEAD_DIM), F32), _pick_tile(t, 256))
        kvc_blocks = _compress(kvc.reshape(t // BLK, BLK * C_KV), w_big, cmp_bias, w2_big,
                               _pick_tile(t // BLK, 256), 2048)
        yb = _nsa_prompt(q, gates, kvc_blocks, kvs16, kvw16, 128, 512)
        hp, conv_new = _merge(hp, y.reshape(t, C_A), bonus, g_, yb, mg, p_prompt[i].reshape(t, -1),
                              jnp.zeros((SUBLANES, LANES), F32), mconsts, _pick_tile(t, 256), True, final)
        kv6 = lambda z, n_: z.reshape(n_, -1, KV_HEADS, 2, HEAD_DIM)
        outs[0].append(kv6(kvc, 1))
        outs[2].append(kv6(kvs, 1))
        outs[4].append(kv6(kvw[t - min(WINDOW, t):], 1))
        outs[6].append(s_new)
        outs[8].append(pa[t - 1:t])
        outs[10].append(conv_new[SUBLANES - (CONV_W - 1):].reshape(1, CONV_W - 1, f2))

        pa, q, kvc, kvs, kvs16, kvw, kvw16, gates, mg = _proj(hs, g1, w_perm, _pick_tile(db, 128))
        seqs = _rwkv_prep(pa, state_shift[i], rw, bd, _pick_tile(db, 128), False)
        r_, w_, k_, v_, a_, b_, g_, bonus = seqs
        y, s_new = _rwkv_scan([z.reshape(db, 1, C_A) for z in (r_, w_, k_, v_, a_, b_)], state_wkv[i], 1)
        pool = _compress(cache_cmp_kv[i].reshape(n_pool * PAGE_BLOCKS, BLK * C_KV), w_big, cmp_bias, w2_big,
                         _pick_tile(n_pool * PAGE_BLOCKS, 1024), 2048)
        q3 = q.reshape(db, 1, C_B)
        oc, idx = _samp_cmp(page_table, q3, pool.reshape(n_pool, PAGE_BLOCKS * C_KV), past)
        idx2 = idx[:, :, 0, :N_SEL].reshape(db, KV_HEADS * N_SEL)
        yb, win_new = _samp_sel(page_table, idx2, q3, gates.reshape(db, 1, GATE_PAD), oc, kvs.reshape(db, 1, C_KV),
                                kvw.reshape(db, 1, C_KV), state_win_kv[i].reshape(db, wb, C_KV),
                                cache_sel_kv[i].reshape(n_pool, page, C_KV), past)
        hs, conv_new = _merge(hs, y.reshape(db, C_A), bonus, g_, yb.reshape(db, C_B), mg, p_sample[i].reshape(db, -1),
                              state_ffn_conv[i].reshape(db, 2 * f2), mconsts, _pick_tile(db, 128), False, final)
        outs[1].append(kv6(kvc, db))
        outs[3].append(kv6(kvs, db))
        outs[5].append(win_new.reshape(db, wb, KV_HEADS, 2, HEAD_DIM))
        outs[7].append(s_new)
        outs[9].append(pa)
        outs[11].append(conv_new.reshape(db, CONV_W - 1, f2))

    stacked = [jnp.stack(o) for o in outs]
    return (hp.reshape(b, t, d_model), hs.reshape(db, dt, d_model), *stacked)
```

```python
import functools

import jax
import jax.numpy as jnp
from jax import lax
from jax.experimental import pallas as pl
from jax.experimental.pallas import tpu as pltpu

F32 = jnp.float32
BF16 = jnp.bfloat16
HI = lax.Precision.HIGHEST

A_HEADS = 8
A_HEAD_DIM = 64
C_A = A_HEADS * A_HEAD_DIM
R_W = 64
R_A = 64
R_G = 128
C_SHIFT = 3 * C_A + R_W + R_A + R_G
LNX_EPS = 64e-5
Q_HEADS = 8
KV_HEADS = 2
GROUP = Q_HEADS // KV_HEADS
HEAD_DIM = 64
C_B = Q_HEADS * HEAD_DIM
C_KV = KV_HEADS * 2 * HEAD_DIM
BLK = 64
N_SEL = 16
WINDOW = 512
D_CMP = 128
FORCE_SCORE = 1e4
CONV_W = 3
NORM_EPS = 1e-6

LANES = 128
SUBLANES = 8
VMEM_BYTES_V7X = 64 * 1024 * 1024

NEG = -1e30
M_FLOOR = -1e29
KV_LANES = 2 * HEAD_DIM

GATE_PAD = LANES


def _vmem_limit(nbytes):
    return int(min(max(nbytes, 16 * 1024 * 1024), VMEM_BYTES_V7X - 8 * 1024 * 1024))


def _const_spec(shape):
    nd = len(shape)
    return pl.BlockSpec(shape, lambda *_: (0,) * nd, pipeline_mode=pl.Buffered(1))


def _rms(x, g):
    return x * lax.rsqrt(jnp.mean(x * x, axis=-1, keepdims=True) + NORM_EPS) * g


def _dot(a, b, **kw):
    return jnp.dot(a, b, preferred_element_type=F32, **kw)


def _dot_nt(a, b):
    return lax.dot_general(a, b, (((1,), (1,)), ((), ())), preferred_element_type=F32)


def _masked_softmax(s, mask):
    s = jnp.where(mask, s, NEG)
    m = jnp.max(s, axis=-1, keepdims=True)
    m = jnp.where(m > M_FLOOR, m, 0.0)
    e = jnp.exp(s - m)
    return e / jnp.maximum(jnp.sum(e, axis=-1, keepdims=True), 1e-30)


def _slope_col(hk, rows_per_head, pad_rows=0):
    cols = [jnp.full((rows_per_head, 1), 2.0 ** (-(hk * GROUP + g + 1)), F32) for g in range(GROUP)]
    if pad_rows:
        cols.append(jnp.zeros((pad_rows, 1), F32))
    return jnp.concatenate(cols, axis=0)


def _proj_body(x_ref, g_ref, w_ref, *o_refs, segs):
    xb = _rms(x_ref[...], g_ref[...]).astype(BF16)
    it = iter(o_refs)
    for off, width, outs in segs:
        r = _dot(xb, w_ref[:, off:off + width])
        for scale in outs:
            o_ref = next(it)
            o_ref[...] = (r * scale if scale != 1.0 else r).astype(o_ref.dtype)


def _proj(x, g, w_perm, tm):
    rows, d = x.shape
    d_model = d
    o = 0
    segs, shapes = [], []
    for width, outs in ((C_SHIFT, ((F32, 1.0),)), (C_B, ((BF16, HEAD_DIM ** -0.5),)), (C_KV, ((F32, 1.0),)),
                        (C_KV, ((F32, 1.0), (BF16, 1.0))), (C_KV, ((F32, 1.0), (BF16, 1.0))),
                        (GATE_PAD, ((F32, 1.0),)), (2 * d_model, ((F32, 1.0),))):
        segs.append((o, width, tuple(s for _, s in outs)))
        shapes += [(width, dt) for dt, _ in outs]
        o += width
    n_tot = o
    out_bytes = sum(tm * w * jnp.dtype(dt).itemsize for w, dt in shapes)
    vmem = 2 * tm * d * 4 + d * n_tot * 2 + 2 * out_bytes + (8 << 20)
    return pl.pallas_call(
        functools.partial(_proj_body, segs=tuple(segs)),
        grid=(rows // tm,),
        in_specs=[pl.BlockSpec((tm, d), lambda i: (i, 0)), _const_spec((1, d)), _const_spec((d, n_tot))],
        out_specs=[pl.BlockSpec((tm, w), lambda i: (i, 0)) for w, _ in shapes],
        out_shape=[jax.ShapeDtypeStruct((rows, w), dt) for w, dt in shapes],
        compiler_params=pltpu.CompilerParams(dimension_semantics=("arbitrary",), vmem_limit_bytes=_vmem_limit(vmem)),
        name="proj",
    )(x, g, w_perm)


def _prep_body(pa_ref, prev_ref, mu_ref, w0_ref, w2_ref, a0_ref, a2_ref, g2_ref, kk_ref, ka_ref, rk_ref, bd_ref,
               r_o, w_o, k_o, v_o, a_o, b_o, g_o, bonus_o, *, seq_mode):
    pf = pa_ref[...]
    if seq_mode:
        i = pl.program_id(0)
        first = jnp.where(i > 0, prev_ref[SUBLANES - 1:SUBLANES, :], 0.0)
        rid = lax.broadcasted_iota(jnp.int32, pf.shape, 0)
        prev = jnp.where(rid == 0, first, pltpu.roll(pf, 1, 0))
    else:
        prev = prev_ref[...]
    xs = pf + (prev - pf) * mu_ref[...]
    r = xs[:, 0:C_A]
    k = xs[:, C_A:2 * C_A]
    v = xs[:, 2 * C_A:3 * C_A]
    o = 3 * C_A
    wd = xs[:, o:o + R_W]
    ad = xs[:, o + R_W:o + R_W + R_A]
    gd = xs[:, o + R_W + R_A:]
    nz = -(w0_ref[...] + _dot(jnp.tanh(wd), w2_ref[...], precision=HI))
    softplus = jnp.maximum(nz, 0.0) + jnp.log(1.0 + jnp.exp(-jnp.abs(nz)))
    w = -softplus - 0.5
    a = jax.nn.sigmoid(a0_ref[...] + _dot(ad, a2_ref[...], precision=HI))
    g = _dot(jax.nn.sigmoid(gd), g2_ref[...], precision=HI)
    kk = k * kk_ref[...]
    ss = _dot(kk * kk, bd_ref[...], precision=HI)
    kkn = kk / jnp.maximum(jnp.sqrt(ss), 1e-12)
    k2 = k * (1.0 + (a - 1.0) * ka_ref[...])
    r_o[...] = r
    w_o[...] = jnp.exp(-jnp.exp(w))
    k_o[...] = k2
    v_o[...] = v
    a_o[...] = -kkn
    b_o[...] = kkn * a
    g_o[...] = g
    bonus_o[...] = _dot(r * k2 * rk_ref[...], bd_ref[...], precision=HI) * v


def _rwkv_prep(pa, prev, rw, bd, tm, seq_mode):
    rows = pa.shape[0]
    mu, w0, w2, a0, a2, g2, k_k, k_a, r_k = rw
    if seq_mode:
        tb = tm // SUBLANES
        prev_spec = pl.BlockSpec((SUBLANES, C_SHIFT), lambda i: (jnp.maximum(i * tb - 1, 0), 0))
        prev = pa
    else:
        prev_spec = pl.BlockSpec((tm, C_SHIFT), lambda i: (i, 0))
    row = lambda z: z.reshape(1, -1)
    consts = [row(mu), row(w0), w2, row(a0), a2, g2, row(k_k), row(k_a), row(r_k), bd]
    vmem = 4 * tm * C_SHIFT * 4 + 2 * 8 * tm * C_A * 4 + 16 * tm * C_A * 4 + (8 << 20)
    return pl.pallas_call(
        functools.partial(_prep_body, seq_mode=seq_mode),
        grid=(rows // tm,),
        in_specs=[pl.BlockSpec((tm, C_SHIFT), lambda i: (i, 0)), prev_spec] + [_const_spec(c.shape) for c in consts],
        out_specs=[pl.BlockSpec((tm, C_A), lambda i: (i, 0))] * 8,
        out_shape=[jax.ShapeDtypeStruct((rows, C_A), F32)] * 8,
        compiler_params=pltpu.CompilerParams(dimension_semantics=("arbitrary",), vmem_limit_bytes=_vmem_limit(vmem)),
        name="rwkv_prep",
    )(pa, prev, *consts)


N_PAIR = A_HEADS // 2


def _scan_body(r_ref, w_ref, k_ref, v_ref, a_ref, b_ref, s0_ref, y_ref, so_ref, st_ref, *, tc, nc):
    c = pl.program_id(1)

    @pl.when(c == 0)
    def _():
        for p in range(N_PAIR):
            st_ref[p] = jnp.concatenate([s0_ref[0, 2 * p], s0_ref[0, 2 * p + 1]], axis=-1)

    shape = (A_HEAD_DIM, LANES)
    lane = lax.broadcasted_iota(jnp.int32, shape, 1)
    sub = lax.broadcasted_iota(jnp.int32, shape, 0)
    lo = lane < A_HEAD_DIM
    diag = (lane & (A_HEAD_DIM - 1)) == sub

    def seg_sum(x):
        s_lo = jnp.sum(jnp.where(lo, x, 0.0), axis=-1, keepdims=True)
        s_hi = jnp.sum(jnp.where(lo, 0.0, x), axis=-1, keepdims=True)
        return jnp.where(lo, s_lo, s_hi)

    grp = min(SUBLANES, tc)

    def token_group(gi, carry):
        base = pl.multiple_of(gi * grp, grp)
        for p in range(N_PAIR):
            sl = slice(LANES * p, LANES * (p + 1))
            rt, wt, kt, vt, at, bt = (ref[0, pl.ds(base, grp), sl] for ref in (r_ref, w_ref, k_ref, v_ref, a_ref, b_ref))
            s = st_ref[p]
            ys = []
            for j in range(grp):
                row = lambda z: z[j:j + 1, :]
                sa = seg_sum(s * row(at))
                v_col = seg_sum(jnp.where(diag, row(vt), 0.0))
                s = s * row(wt) + sa * row(bt) + v_col * row(kt)
                y_col = seg_sum(s * row(rt))
                ys.append(jnp.sum(jnp.where(diag, y_col, 0.0), axis=0, keepdims=True))
            st_ref[p] = s
            y_ref[0, pl.ds(base, grp), sl] = jnp.concatenate(ys, axis=0) if grp > 1 else ys[0]
        return carry

    lax.fori_loop(0, tc // grp, token_group, 0)

    @pl.when(c == nc - 1)
    def _():
        for p in range(N_PAIR):
            s = st_ref[p]
            so_ref[0, 2 * p] = s[:, :A_HEAD_DIM]
            so_ref[0, 2 * p + 1] = s[:, A_HEAD_DIM:]


def _rwkv_scan(seqs, s0, tc):
    b, t, _ = seqs[0].shape
    nc = t // tc
    seq_spec = pl.BlockSpec((1, tc, C_A), lambda i, c: (i, c, 0))
    st_spec = pl.BlockSpec((1, A_HEADS, A_HEAD_DIM, A_HEAD_DIM), lambda i, c: (i, 0, 0, 0))
    return pl.pallas_call(
        functools.partial(_scan_body, tc=tc, nc=nc),
        grid=(b, nc),
        in_specs=[seq_spec] * 6 + [st_spec],
        out_specs=[seq_spec, st_spec],
        out_shape=[jax.ShapeDtypeStruct((b, t, C_A), F32), jax.ShapeDtypeStruct(s0.shape, F32)],
        scratch_shapes=[pltpu.VMEM((N_PAIR, A_HEAD_DIM, LANES), F32)],
        compiler_params=pltpu.CompilerParams(dimension_semantics=("arbitrary", "arbitrary")),
        name="rwkv_scan",
    )(*seqs, s0)


def _gelu_tanh(x):
    return 0.5 * x * (1.0 + jnp.tanh(0.7978845608028654 * (x + 0.044715 * (x * x * x))))


def _compress_body(x_ref, w_ref, b_ref, w2_ref, o_ref, acc_ref, *, nk, mlp):
    kk = pl.program_id(1)

    @pl.when(kk == 0)
    def _():
        acc_ref[...] = jnp.zeros_like(acc_ref)

    acc_ref[...] += _dot(x_ref[...].astype(BF16), w_ref[...])

    @pl.when(kk == nk - 1)
    def _():
        h = acc_ref[...] + b_ref[...]
        if mlp:
            o_ref[...] = _dot(_gelu_tanh(h).astype(BF16), w2_ref[...])
        else:
            o_ref[...] = h


def _compress(x2, w_big, bias, w2_big, tm, tk, mlp=True):
    m, kdim = x2.shape
    nh = w_big.shape[1]
    n_out = w2_big.shape[1] if mlp else nh
    nk = kdim // tk
    vmem = 2 * tm * tk * 4 + 2 * tk * nh * 2 + 3 * tm * nh * 4 + 2 * tm * n_out * 4 + (8 << 20)
    return pl.pallas_call(
        functools.partial(_compress_body, nk=nk, mlp=mlp),
        grid=(m // tm, nk),
        in_specs=[pl.BlockSpec((tm, tk), lambda i, k: (i, k)), pl.BlockSpec((tk, nh), lambda i, k: (k, 0)),
                  _const_spec(bias.shape), _const_spec(w2_big.shape)],
        out_specs=pl.BlockSpec((tm, n_out), lambda i, k: (i, 0)),
        out_shape=jax.ShapeDtypeStruct((m, n_out), F32),
        scratch_shapes=[pltpu.VMEM((tm, nh), F32)],
        compiler_params=pltpu.CompilerParams(dimension_semantics=("arbitrary", "arbitrary"),
                                             vmem_limit_bytes=_vmem_limit(vmem)),
        name="nsa_compress",
    )(x2, w_big, bias, w2_big)


PAGE_ROWS = KV_HEADS * 2 * HEAD_DIM


def _compress_pool_body(x_ref, w_ref, b_ref, w2_ref, o_ref, *, pg):
    m = pg * KV_HEADS
    for c in range(2):
        acc = jnp.zeros((m, 2 * D_CMP), F32)
        for d in range(HEAD_DIM):
            r0 = c * HEAD_DIM + d
            x_cd = x_ref[pl.ds(r0, m, stride=2 * HEAD_DIM), :].astype(BF16)
            acc = acc + _dot(x_cd, w_ref[r0 * 2 * BLK:(r0 + 1) * 2 * BLK, :])
        h = acc + b_ref[c:c + 1, :]
        o_ref[:, c * 2 * HEAD_DIM:(c + 1) * 2 * HEAD_DIM] = _dot(_gelu_tanh(h).astype(BF16), w2_ref[c])


def _compress_pool(x2, w_tok, bias2, w2_pair, pg):
    n_pool = x2.shape[0] // PAGE_ROWS
    m = pg * KV_HEADS
    vmem = 2 * pg * PAGE_ROWS * 2 * BLK * 4 + int(w_tok.size) * 2 + 8 * m * 2 * D_CMP * 4 + (8 << 20)
    return pl.pallas_call(
        functools.partial(_compress_pool_body, pg=pg),
        grid=(n_pool // pg,),
        in_specs=[pl.BlockSpec((pg * PAGE_ROWS, 2 * BLK), lambda i: (i, 0)), _const_spec(w_tok.shape),
                  _const_spec(bias2.shape), _const_spec(w2_pair.shape)],
        out_specs=pl.BlockSpec((m, C_KV), lambda i: (i, 0)),
        out_shape=jax.ShapeDtypeStruct((n_pool * KV_HEADS, C_KV), F32),
        compiler_params=pltpu.CompilerParams(dimension_semantics=("arbitrary",), vmem_limit_bytes=_vmem_limit(vmem)),
        name="nsa_compress_pool",
    )(x2, w_tok, bias2, w2_pair)


def _topk_mask(s, ids, k):
    sel = jnp.zeros(s.shape, F32)
    for _ in range(k):
        m = jnp.max(s, axis=-1, keepdims=True)
        pick = jnp.min(jnp.where(s == m, ids, 1e9), axis=-1, keepdims=True)
        hit = ids == pick
        sel = jnp.where(hit, 1.0, sel)
        s = jnp.where(hit, -jnp.inf, s)
    return sel


def _head_rows(q_ref, hk, pad_rows=0, pad_lanes=True):
    parts = [q_ref[:, (hk * GROUP + g) * HEAD_DIM:(hk * GROUP + g + 1) * HEAD_DIM] for g in range(GROUP)]
    if pad_rows:
        parts.append(jnp.zeros((pad_rows, HEAD_DIM), parts[0].dtype))
    qh = jnp.concatenate(parts, axis=0)
    return jnp.concatenate([qh, jnp.zeros_like(qh)], axis=-1) if pad_lanes else qh


def _nsa_prompt_body(q_ref, gt_ref, kc_ref, ks_ref, kw_ref, em_ref, o_ref, *, tq, tk, nbc):
    i = pl.program_id(0)
    t0 = i * tq
    rows = GROUP * tq
    qpos1 = t0 + lax.broadcasted_iota(jnp.int32, (tq, 1), 0)
    qpos = jnp.concatenate([qpos1] * GROUP, axis=0)
    qposf = qpos.astype(F32)
    blk = lax.broadcasted_iota(jnp.int32, (1, nbc), 1)
    blkf = blk.astype(F32)
    cur = qpos1 // BLK
    gt = jax.nn.sigmoid(gt_ref[...])
    bpt = tk // BLK
    pieces = []
    for hk in range(KV_HEADS):
        hl = slice(hk * KV_LANES, (hk + 1) * KV_LANES)
        qh = _head_rows(q_ref, hk)
        slope = _slope_col(hk, tq)
        kcb = kc_ref[:, hl].astype(BF16)
        cpos = blk * BLK + (BLK - 1)
        s = _dot_nt(qh, kcb) - slope * (qposf - cpos.astype(F32))
        p_c = _masked_softmax(s, cpos <= qpos)
        o_c = _dot(p_c.astype(BF16), kcb)[:, HEAD_DIM:]
        score = p_c[0:tq]
        for g in range(1, GROUP):
            score = score + p_c[g * tq:(g + 1) * tq]
        forced = (blk == 0) | (blk == cur) | (blk == cur - 1)
        sc = jnp.where(forced, FORCE_SCORE, score)
        sc = jnp.where(blk > cur, -1.0, sc)
        sel = _topk_mask(sc, blkf, min(N_SEL, nbc))
        not_sel = jnp.where(blk <= cur, 1.0 - sel, 1.0)

        def sel_step(j, carry):
            m, l, acc = carry
            k0 = pl.multiple_of(j * tk, tk)
            kv = ks_ref[pl.ds(k0, tk), hl]
            kpos = k0 + lax.broadcasted_iota(jnp.int32, (1, tk), 1)
            ns_j = jnp.where(blk // bpt == j, not_sel, 0.0).astype(BF16)
            mb = _dot(ns_j, em_ref[...])
            bias = jnp.concatenate([mb] * GROUP, axis=0)
            s = _dot_nt(qh, kv) + slope * (kpos - t0).astype(F32) + bias
            s = jnp.where(kpos <= qpos, s, NEG)
            m_new = jnp.maximum(m, jnp.max(s, axis=-1, keepdims=True))
            alpha = jnp.exp(m - m_new)
            p = jnp.exp(s - m_new)
            l = alpha * l + jnp.sum(p, axis=-1, keepdims=True)
            acc = alpha * acc + _dot(p.astype(BF16), kv)
            return m_new, l, acc

        n_tiles = (t0 + tq + tk - 1) // tk
        init = (jnp.full((rows, 1), M_FLOOR, F32), jnp.zeros((rows, 1), F32), jnp.zeros((rows, KV_LANES), F32))
        _, l, acc = lax.fori_loop(0, n_tiles, sel_step, init)
        o_s = (acc / jnp.maximum(l, 1e-30))[:, HEAD_DIM:]

        wl = WINDOW + tq
        ws = pl.multiple_of(jnp.maximum(t0 - WINDOW, 0), tq)
        kvw = kw_ref[pl.ds(ws, wl), hl]
        kposw = ws + lax.broadcasted_iota(jnp.int32, (1, wl), 1)
        s = _dot_nt(qh, kvw) - slope * (qposf - kposw.astype(F32))
        p_w = _masked_softmax(s, (kposw <= qpos) & (qpos - kposw < WINDOW))
        o_w = _dot(p_w.astype(BF16), kvw)[:, HEAD_DIM:]

        for g in range(GROUP):
            c0 = (hk * GROUP + g) * 3
            rs = slice(g * tq, (g + 1) * tq)
            pieces.append(gt[:, c0:c0 + 1] * o_c[rs] + gt[:, c0 + 1:c0 + 2] * o_s[rs] + gt[:, c0 + 2:c0 + 3] * o_w[rs])
    o_ref[...] = jnp.concatenate(pieces, axis=-1)


def _nsa_prompt(q, gates, kvc, ks, kw, tq, tk):
    t = q.shape[0]
    nbc = kvc.shape[0]
    bpt = tk // BLK
    em = jnp.where((jnp.arange(nbc)[:, None] % bpt) == (jnp.arange(tk)[None, :] // BLK), NEG, 0.0).astype(BF16)
    vmem = 2 * t * C_KV * 2 + nbc * C_KV * 4 + 24 * GROUP * tq * max(tk, WINDOW + tq) * 4 + (8 << 20)
    return pl.pallas_call(
        functools.partial(_nsa_prompt_body, tq=tq, tk=tk, nbc=nbc),
        grid=(t // tq,),
        in_specs=[pl.BlockSpec((tq, C_B), lambda i: (i, 0)), pl.BlockSpec((tq, GATE_PAD), lambda i: (i, 0)),
                  _const_spec(kvc.shape), _const_spec(ks.shape), _const_spec(kw.shape), _const_spec(em.shape)],
        out_specs=pl.BlockSpec((tq, C_B), lambda i: (i, 0)),
        out_shape=jax.ShapeDtypeStruct((t, C_B), F32),
        compiler_params=pltpu.CompilerParams(dimension_semantics=("arbitrary",), vmem_limit_bytes=_vmem_limit(vmem)),
        name="nsa_prompt",
    )(q, gates, kvc, ks, kw, em)


PAGE_BLOCKS = 2
Q_ROWS = SUBLANES


def _samp_cmp_body(pt_ref, q_ref, pool_ref, oc_ref, idx_ref, kbuf, sem, *, n_pages, past):
    n = pl.program_id(0)

    def page_copy(jp):
        return pltpu.make_async_copy(pool_ref.at[pl.ds(pt_ref[n, jp], 1), :], kbuf.at[pl.ds(jp, 1), :], sem)

    def start(jp, c):
        page_copy(jp).start()
        return c

    def wait(jp, c):
        page_copy(jp).wait()
        return c

    lax.fori_loop(0, n_pages, start, 0)
    lax.fori_loop(0, n_pages, wait, 0)

    nb_past = n_pages * PAGE_BLOCKS
    n_all = nb_past + 1
    qpos = past
    cur = qpos // BLK
    width = PAGE_BLOCKS * n_pages
    lane = lax.broadcasted_iota(jnp.int32, (1, width), 1)
    bid = jnp.where(lane < n_pages, PAGE_BLOCKS * lane, PAGE_BLOCKS * (lane - n_pages) + 1)
    cpos = bid * BLK + (BLK - 1)
    wide = 2 * width
    lane2 = lax.broadcasted_iota(jnp.int32, (1, wide), 1)
    bid2 = jnp.concatenate([bid, nb_past + lane], axis=-1)
    bid2f = bid2.astype(F32)
    for hk in range(KV_HEADS):
        q8 = _head_rows(q_ref.at[0], hk, pad_rows=Q_ROWS - GROUP, pad_lanes=False)
        z8 = jnp.zeros_like(q8)
        slope = _slope_col(hk, 1, pad_rows=Q_ROWS - GROUP)
        kmat = kbuf[:, hk * C_KV:hk * C_KV + KV_LANES].astype(BF16)
        vmat = kbuf[:, hk * C_KV + KV_LANES:(hk + 1) * C_KV].astype(BF16)
        s = jnp.concatenate([_dot_nt(jnp.concatenate([q8, z8], axis=-1), kmat),
                             _dot_nt(jnp.concatenate([z8, q8], axis=-1), kmat)], axis=-1)
        s = s - slope * (qpos - cpos).astype(F32)
        p_c = _masked_softmax(s, cpos <= qpos)
        o_even = _dot(p_c[:, :n_pages].astype(BF16), vmat)
        o_odd = _dot(p_c[:, n_pages:].astype(BF16), vmat)
        oc_ref[0, hk] = o_even[:, :HEAD_DIM] + o_odd[:, HEAD_DIM:]
        score = jnp.sum(p_c[0:GROUP], axis=0, keepdims=True)
        sc = jnp.concatenate([score, jnp.zeros((1, width), F32)], axis=-1)
        forced = (bid2 == 0) | (bid2 == cur) | (bid2 == cur - 1)
        sc = jnp.where(forced, FORCE_SCORE, sc)
        sc = jnp.where(bid2 > cur, -1.0, sc)
        sc = jnp.where(bid2 < n_all, sc, -jnp.inf)
        res = jnp.zeros((1, wide), F32)
        for it in range(N_SEL):
            m = jnp.max(sc, axis=-1, keepdims=True)
            pick = jnp.min(jnp.where(sc == m, bid2f, 1e9), axis=-1, keepdims=True)
            res = jnp.where(lane2 == it, pick, res)
            sc = jnp.where(bid2f == pick, -jnp.inf, sc)
        idx_ref[0, hk] = jnp.broadcast_to(res[:, :LANES], (SUBLANES, LANES)).astype(jnp.int32)


def _samp_cmp(page_table, q3, pool2, past):
    db, n_pages = page_table.shape
    grid_spec = pltpu.PrefetchScalarGridSpec(
        num_scalar_prefetch=1,
        grid=(db,),
        in_specs=[pl.BlockSpec((1, 1, C_B), lambda n, pt: (n, 0, 0)), pl.BlockSpec(memory_space=pl.ANY)],
        out_specs=[pl.BlockSpec((1, KV_HEADS, Q_ROWS, HEAD_DIM), lambda n, pt: (n, 0, 0, 0)),
                   pl.BlockSpec((1, KV_HEADS, SUBLANES, LANES), lambda n, pt: (n, 0, 0, 0))],
        scratch_shapes=[pltpu.VMEM((n_pages, KV_HEADS * C_KV), F32), pltpu.SemaphoreType.DMA(())],
    )
    return pl.pallas_call(
        functools.partial(_samp_cmp_body, n_pages=n_pages, past=past),
        grid_spec=grid_spec,
        out_shape=[jax.ShapeDtypeStruct((db, KV_HEADS, Q_ROWS, HEAD_DIM), F32),
                   jax.ShapeDtypeStruct((db, KV_HEADS, SUBLANES, LANES), jnp.int32)],
        compiler_params=pltpu.CompilerParams(dimension_semantics=("arbitrary",)),
        name="nsa_sample_cmp",
    )(page_table, q3, pool2)


def _samp_sel_body(pt_ref, idx_ref, q_ref, gt_ref, oc_ref, ksc_ref, kwr_ref, kwc_ref, win_ref, cache_ref,
                   o_ref, wino_ref, sbuf, sems, *, nb_past, past, wb):
    n = pl.program_id(0)
    qpos = past

    page_tok = PAGE_BLOCKS * BLK
    tok = lax.broadcasted_iota(jnp.int32, (HEAD_DIM, page_tok), 1)

    def page_copy(hk, s, ib):
        page = pt_ref[n, ib // PAGE_BLOCKS]
        return pltpu.make_async_copy(cache_ref.at[page, hk], sbuf.at[hk, :, :, pl.ds(s * page_tok, page_tok)],
                                     sems.at[hk * N_SEL + s])

    for hk in range(KV_HEADS):
        for s in range(N_SEL):
            ib = idx_ref[n, hk * N_SEL + s]

            @pl.when(ib < nb_past)
            def _():
                page_copy(hk, s, ib).start()

            @pl.when(ib >= nb_past)
            def _():
                for c in range(2):
                    sbuf[hk, c, :, s * page_tok:(s + 1) * page_tok] = jnp.where(tok == 0, ksc_ref[0, 2 * hk + c], 0.0)

    wtok = lax.broadcasted_iota(jnp.int32, (HEAD_DIM, wb), 1)
    for hk in range(KV_HEADS):
        for c in range(2):
            wino_ref[0, hk, c] = jnp.where(wtok == wb - 1, kwc_ref[0, 2 * hk + c],
                                           pltpu.roll(win_ref[0, hk, c], wb - 1, 1))

    for hk in range(KV_HEADS):
        for s in range(N_SEL):
            ib = idx_ref[n, hk * N_SEL + s]

            @pl.when(ib < nb_past)
            def _():
                page_copy(hk, s, ib).wait()

    gt = jax.nn.sigmoid(gt_ref[0])
    nk = N_SEL * page_tok
    lane = lax.broadcasted_iota(jnp.int32, (1, nk), 1)
    kposw = past - wb + lax.broadcasted_iota(jnp.int32, (1, wb), 1)
    wmask = (kposw <= qpos) & (qpos - kposw < WINDOW) & (kposw >= 0)
    pieces = []
    for hk in range(KV_HEADS):
        q8 = _head_rows(q_ref.at[0], hk, pad_rows=Q_ROWS - GROUP, pad_lanes=False)
        slope = _slope_col(hk, 1, pad_rows=Q_ROWS - GROUP)
        ibv = jnp.zeros((1, nk), jnp.int32)
        for s in range(N_SEL):
            ibv = jnp.where(lane // page_tok == s, idx_ref[n, hk * N_SEL + s], ibv)
        spos = (ibv - ibv % PAGE_BLOCKS) * BLK + lane % page_tok
        s_s = _dot(q8, sbuf[hk, 0].astype(BF16)) - slope * (qpos - spos).astype(F32)
        p_s = _masked_softmax(s_s, (spos // BLK == ibv) & (spos <= qpos))
        o_s = _dot_nt(p_s.astype(BF16), sbuf[hk, 1].astype(BF16))
        kwn = kwr_ref[0]
        k_new = kwn[:, hk * KV_LANES:hk * KV_LANES + HEAD_DIM]
        v_new = kwn[:, hk * KV_LANES + HEAD_DIM:(hk + 1) * KV_LANES]
        s_w = _dot(q8, win_ref[0, hk, 0].astype(BF16)) - slope * (qpos - kposw).astype(F32)
        s_w = jnp.where(wmask, s_w, NEG)
        s_n = jnp.sum(q8.astype(F32) * k_new, axis=-1, keepdims=True)
        m_w = jnp.maximum(jnp.max(s_w, axis=-1, keepdims=True), s_n)
        e_w = jnp.exp(s_w - m_w)
        e_n = jnp.exp(s_n - m_w)
        den = jnp.sum(e_w, axis=-1, keepdims=True) + e_n
        o_w = (_dot_nt(e_w.astype(BF16), win_ref[0, hk, 1].astype(BF16)) + e_n * v_new) / den
        o_c = oc_ref[0, hk]
        for g in range(GROUP):
            c0 = (hk * GROUP + g) * 3
            pieces.append(gt[:, c0:c0 + 1] * o_c[g:g + 1] + gt[:, c0 + 1:c0 + 2] * o_s[g:g + 1]
                          + gt[:, c0 + 2:c0 + 3] * o_w[g:g + 1])
    o_ref[0] = jnp.concatenate(pieces, axis=-1)


def _samp_sel(page_table, idx, q3, gates3, oc, ks_cols, kw_row, kw_cols, win_t, cache_t, past):
    db, n_pages = page_table.shape
    wb = win_t.shape[-1]
    nb_past = n_pages * PAGE_BLOCKS
    row3 = lambda w: pl.BlockSpec((1, 1, w), lambda n, pt, ix: (n, 0, 0))
    col4 = pl.BlockSpec((1, 2 * KV_HEADS, HEAD_DIM, 1), lambda n, pt, ix: (n, 0, 0, 0))
    win_spec = pl.BlockSpec((1, KV_HEADS, 2, HEAD_DIM, wb), lambda n, pt, ix: (n, 0, 0, 0, 0))
    grid_spec = pltpu.PrefetchScalarGridSpec(
        num_scalar_prefetch=2,
        grid=(db,),
        in_specs=[row3(C_B), row3(GATE_PAD),
                  pl.BlockSpec((1, KV_HEADS, Q_ROWS, HEAD_DIM), lambda n, pt, ix: (n, 0, 0, 0)),
                  col4, row3(C_KV), col4, win_spec, pl.BlockSpec(memory_space=pl.ANY)],
        out_specs=[row3(C_B), win_spec],
        scratch_shapes=[pltpu.VMEM((KV_HEADS, 2, HEAD_DIM, N_SEL * PAGE_BLOCKS * BLK), F32),
                        pltpu.SemaphoreType.DMA((KV_HEADS * N_SEL,))],
    )
    return pl.pallas_call(
        functools.partial(_samp_sel_body, nb_past=nb_past, past=past, wb=wb),
        grid_spec=grid_spec,
        out_shape=[jax.ShapeDtypeStruct((db, 1, C_B), F32), jax.ShapeDtypeStruct(win_t.shape, F32)],
        compiler_params=pltpu.CompilerParams(dimension_semantics=("arbitrary",)),
        name="nsa_sample_sel",
    )(page_table, idx, q3, gates3, oc, ks_cols, kw_row, kw_cols, win_t, cache_t)


FF_CHUNK = 256


def _merge_body(x_ref, y_ref, bonus_ref, g_ref, yb_ref, mg_ref, p_ref, cbuf_ref,
                lng_ref, lnb_ref, bd_ref, woa_ref, wob_ref, wout_ref, n2_ref, wup_ref, cw_ref, cb_ref, wdn_ref,
                n3_ref, wpe_ref, wpg_ref, fg_ref, o_ref, cnew_ref, carry_ref, *, seq_mode, final, d_ff, tm):
    i = pl.program_id(0)
    d_model = x_ref.shape[1]
    y = y_ref[...]
    inv = 1.0 / A_HEAD_DIM
    mean = _dot(y, bd_ref[...], precision=HI) * inv
    d = y - mean
    var = _dot(d * d, bd_ref[...], precision=HI) * inv
    ya = (d * lax.rsqrt(var + LNX_EPS) * lng_ref[...] + lnb_ref[...] + bonus_ref[...]) * g_ref[...]
    mg = mg_ref[...]
    m = (jax.nn.sigmoid(mg[:, :d_model]) * _dot(ya.astype(BF16), woa_ref[...])
         + jax.nn.sigmoid(mg[:, d_model:]) * _dot(yb_ref[...].astype(BF16), wob_ref[...]))
    h = x_ref[...] + _dot(m.astype(BF16), wout_ref[...])
    xn = _rms(h, n2_ref[...]).astype(BF16)

    if seq_mode:
        @pl.when(i == 0)
        def _():
            carry_ref[...] = jnp.zeros_like(carry_ref)
        rid = lax.broadcasted_iota(jnp.int32, (tm, FF_CHUNK), 0)

    acc = jnp.zeros((tm, d_model), F32)
    for c in range(d_ff // FF_CHUNK):
        parts = []
        for half in range(2):
            cs = slice(half * d_ff + c * FF_CHUNK, half * d_ff + (c + 1) * FF_CHUNK)
            up = _dot(xn, wup_ref[:, cs])
            if seq_mode:
                t1 = carry_ref[SUBLANES - 1:SUBLANES, cs]
                t2 = carry_ref[SUBLANES - 2:SUBLANES - 1, cs]
                up1 = jnp.where(rid == 0, t1, pltpu.roll(up, 1, 0))
                up2 = jnp.where(rid == 0, t2, jnp.where(rid == 1, t1, pltpu.roll(up, 2, 0)))
                carry_ref[:, cs] = up[tm - SUBLANES:, :]
            else:
                up2 = cbuf_ref[:, cs]
                up1 = cbuf_ref[:, 2 * d_ff + cs.start:2 * d_ff + cs.stop]
                cnew_ref[:, cs] = up1
                cnew_ref[:, 2 * d_ff + cs.start:2 * d_ff + cs.stop] = up
            parts.append(cb_ref[:, cs] + cw_ref[0:1, cs] * up2 + cw_ref[1:2, cs] * up1 + cw_ref[2:3, cs] * up)
        a, gate = parts
        act = (a * jax.nn.sigmoid(a) * gate).astype(BF16)
        acc = acc + _dot(act, wdn_ref[c * FF_CHUNK:(c + 1) * FF_CHUNK, :])
    if seq_mode:
        cnew_ref[...] = carry_ref[...]
    h = h + acc
    pe = _dot(p_ref[...].astype(BF16), wpe_ref[...])
    h = h + pe * jax.nn.sigmoid(_dot(_rms(h, n3_ref[...]).astype(BF16), wpg_ref[...]))
    o_ref[...] = _rms(h, fg_ref[...]) if final else h


def _merge(x, y, bonus, g, yb, mg, p, cbuf, consts, tm, seq_mode, final):
    rows, d_model = x.shape
    d_ff = consts[10].shape[0]
    f2 = 2 * d_ff
    rowspec = lambda w: pl.BlockSpec((tm, w), lambda i: (i, 0))
    if seq_mode:
        cbuf_spec = _const_spec(cbuf.shape)
        cnew_shape, cnew_spec = (SUBLANES, f2), pl.BlockSpec((SUBLANES, f2), lambda i: (0, 0))
    else:
        cbuf_spec = rowspec(2 * f2)
        cnew_shape, cnew_spec = (rows, 2 * f2), rowspec(2 * f2)
    wbytes = sum(int(c.size) * c.dtype.itemsize for c in consts)
    act = tm * (d_model * 3 + C_A * 4 + 256) * 4 + (0 if seq_mode else 2 * tm * 2 * f2 * 4)
    vmem = wbytes + 2 * act + 8 * tm * d_model * 4 + (8 << 20)
    return pl.pallas_call(
        functools.partial(_merge_body, seq_mode=seq_mode, final=final, d_ff=d_ff, tm=tm),
        grid=(rows // tm,),
        in_specs=[rowspec(d_model), rowspec(C_A), rowspec(C_A), rowspec(C_A), rowspec(C_B), rowspec(2 * d_model),
                  rowspec(p.shape[1]), cbuf_spec] + [_const_spec(c.shape) for c in consts],
        out_specs=[rowspec(d_model), cnew_spec],
        out_shape=[jax.ShapeDtypeStruct((rows, d_model), F32), jax.ShapeDtypeStruct(cnew_shape, F32)],
        scratch_shapes=[pltpu.VMEM((SUBLANES, f2), F32)],
        compiler_params=pltpu.CompilerParams(dimension_semantics=("arbitrary",), vmem_limit_bytes=_vmem_limit(vmem)),
        name="merge_ffn",
    )(x, y, bonus, g, yb, mg, p, cbuf, *consts)


def _pick_tile(n, target):
    t = min(n, target)
    while n % t:
        t //= 2
    return t


def _head_block_diag():
    h = jnp.arange(C_A) // A_HEAD_DIM
    return (h[:, None] == h[None, :]).astype(F32)


def _compress_weights(pe, w1, b1, w2):
    eye = jnp.eye(KV_HEADS, dtype=F32)
    eye_c = jnp.eye(2, dtype=F32)
    w_big = jnp.einsum('crdf,hg,ce->rhcdgef', w1, eye, eye_c).reshape(BLK * C_KV, KV_HEADS * 2 * D_CMP)
    w2_big = jnp.einsum('cfd,hg,ce->hcfged', w2, eye, eye_c).reshape(KV_HEADS * 2 * D_CMP, C_KV)
    pe_row = jnp.broadcast_to(jnp.transpose(pe, (1, 0, 2))[:, None], (BLK, KV_HEADS, 2, HEAD_DIM)).reshape(1, -1)
    b1_row = jnp.broadcast_to(b1[None], (KV_HEADS, 2, D_CMP)).reshape(1, -1)
    return w_big.astype(BF16), w2_big.astype(BF16), pe_row, b1_row


def kernel(x_prompt, x_sample, p_prompt, p_sample, cache_cmp_kv, cache_sel_kv, page_table, state_win_kv, state_wkv, state_shift, state_ffn_conv, norm1_g, w_in, shift_mu, rwkv_w0, rwkv_w2, rwkv_a0, rwkv_a2, rwkv_g2, rwkv_k_k, rwkv_k_a, rwkv_r_k, lnx_g, lnx_b, cmp_pe, cmp_w1, cmp_b1, cmp_w2, w_oa, w_ob, w_out, norm2_g, w_up, conv_w, conv_b, w_down, norm3_g, w_pe, w_pg, final_g):
    depth = w_in.shape[0]
    b, t, d_model = x_prompt.shape
    db, dt, _ = x_sample.shape
    n_pool, page = cache_cmp_kv.shape[1], cache_cmp_kv.shape[2]
    n_pages = page_table.shape[1]
    past = n_pages * page
    d_ff = w_down.shape[1]
    f2 = 2 * d_ff
    wb = state_win_kv.shape[2]
    assert b == 1 and dt == 1 and page == PAGE_BLOCKS * BLK
    assert t % BLK == 0 and t >= WINDOW + 128 and d_ff % FF_CHUNK == 0 and wb == WINDOW and past >= WINDOW
    assert N_SEL <= LANES and n_pages * PAGE_BLOCKS + 1 >= N_SEL

    bd = _head_block_diag()
    hp = x_prompt.reshape(t, d_model)
    hs = x_sample.reshape(db, d_model)
    outs = [[] for _ in range(12)]
    for i in range(depth):
        o_g = C_SHIFT + C_B + 3 * C_KV
        w_perm = jnp.concatenate(
            [w_in[i][:, :o_g], jnp.pad(w_in[i][:, o_g:o_g + 3 * Q_HEADS], ((0, 0), (0, GATE_PAD - 3 * Q_HEADS))),
             w_in[i][:, o_g + 3 * Q_HEADS:]], axis=1).astype(BF16)
        g1 = norm1_g[i].reshape(1, -1)
        rw = (shift_mu[i], rwkv_w0[i], rwkv_w2[i], rwkv_a0[i], rwkv_a2[i], rwkv_g2[i], rwkv_k_k[i], rwkv_k_a[i],
              rwkv_r_k[i].reshape(-1))
        w_big, w2_big, pe_row, b1_row = _compress_weights(cmp_pe[i], cmp_w1[i], cmp_b1[i], cmp_w2[i])
        zero_bias = jnp.zeros_like(b1_row)
        cmp_bias = _compress(jnp.broadcast_to(pe_row, (SUBLANES, pe_row.shape[1])), w_big, zero_bias, w2_big,
                             SUBLANES, 2048, mlp=False)[0:1] + b1_row
        eye_b = jnp.eye(PAGE_BLOCKS, dtype=F32)
        w_tok = jnp.einsum('crdf,eg->cdergf', cmp_w1[i], eye_b).reshape(2 * HEAD_DIM * PAGE_BLOCKS * BLK,
                                                                        PAGE_BLOCKS * D_CMP).astype(BF16)
        w2_pair = jnp.einsum('cfd,eg->cefgd', cmp_w2[i], eye_b).reshape(2, PAGE_BLOCKS * D_CMP,
                                                                       PAGE_BLOCKS * HEAD_DIM).astype(BF16)
        bias_pair = jnp.tile(cmp_bias[0, :2 * D_CMP].reshape(2, D_CMP), (1, PAGE_BLOCKS))
        row = lambda z: z.reshape(1, -1)
        mconsts = [row(lnx_g[i]), row(lnx_b[i]), bd, w_oa[i].astype(BF16), w_ob[i].astype(BF16),
                   w_out[i].astype(BF16), row(norm2_g[i]), w_up[i].astype(BF16), conv_w[i], row(conv_b[i]),
                   w_down[i].astype(BF16), row(norm3_g[i]), w_pe[i].astype(BF16), w_pg[i].astype(BF16), row(final_g)]
        final = i == depth - 1

        pa, q, kvc, kvs, kvs16, kvw, kvw16, gates, mg = _proj(hp, g1, w_perm, _pick_tile(t, 256))
        seqs = _rwkv_prep(pa, None, rw, bd, _pick_tile(t, 512), True)
        r_, w_, k_, v_, a_, b_, g_, bonus = seqs
        y, s_new = _rwkv_scan([z.reshape(1, t, C_A) for z in (r_, w_, k_, v_, a_, b_)],
                              jnp.zeros((1, A_HEADS, A_HEAD_DIM, A_HEAD_DIM), F32), _pick_tile(t, 256))
        kvc_blocks = _compress(kvc.reshape(t // BLK, BLK * C_KV), w_big, cmp_bias, w2_big,
                               _pick_tile(t // BLK, 256), 2048)
        yb = _nsa_prompt(q, gates, kvc_blocks, kvs16, kvw16, 128, 512)
        hp, conv_new = _merge(hp, y.reshape(t, C_A), bonus, g_, yb, mg, p_prompt[i].reshape(t, -1),
                              jnp.zeros((SUBLANES, LANES), F32), mconsts, _pick_tile(t, 256), True, final)
        kv6 = lambda z, n_: z.reshape(n_, -1, KV_HEADS, 2, HEAD_DIM)
        outs[0].append(kv6(kvc, 1))
        outs[2].append(kv6(kvs, 1))
        outs[4].append(kv6(kvw[t - min(WINDOW, t):], 1))
        outs[6].append(s_new)
        outs[8].append(pa[t - 1:t])
        outs[10].append(conv_new[SUBLANES - (CONV_W - 1):].reshape(1, CONV_W - 1, f2))

        pa, q, kvc, kvs, kvs16, kvw, kvw16, gates, mg = _proj(hs, g1, w_perm, _pick_tile(db, 128))
        seqs = _rwkv_prep(pa, state_shift[i], rw, bd, _pick_tile(db, 128), False)
        r_, w_, k_, v_, a_, b_, g_, bonus = seqs
        y, s_new = _rwkv_scan([z.reshape(db, 1, C_A) for z in (r_, w_, k_, v_, a_, b_)], state_wkv[i], 1)
        to_tok_minor = lambda z: jnp.transpose(z, (0, 2, 3, 4, 1))
        pool = _compress_pool(to_tok_minor(cache_cmp_kv[i]).reshape(n_pool * PAGE_ROWS, page), w_tok, bias_pair,
                              w2_pair, _pick_tile(n_pool, 64))
        q3 = q.reshape(db, 1, C_B)
        oc, idx = _samp_cmp(page_table, q3, pool.reshape(n_pool, KV_HEADS * C_KV), past)
        idx2 = idx[:, :, 0, :N_SEL].reshape(db, KV_HEADS * N_SEL)
        cols = lambda z: z.reshape(db, 2 * KV_HEADS, HEAD_DIM, 1)
        yb, win_new_t = _samp_sel(page_table, idx2, q3, gates.reshape(db, 1, GATE_PAD), oc, cols(kvs),
                                  kvw.reshape(db, 1, C_KV), cols(kvw), to_tok_minor(state_win_kv[i]),
                                  to_tok_minor(cache_sel_kv[i]), past)
        win_new = jnp.transpose(win_new_t, (0, 4, 1, 2, 3))
        hs, conv_new = _merge(hs, y.reshape(db, C_A), bonus, g_, yb.reshape(db, C_B), mg, p_sample[i].reshape(db, -1),
                              state_ffn_conv[i].reshape(db, 2 * f2), mconsts, _pick_tile(db, 128), False, final)
        outs[1].append(kv6(kvc, db))
        outs[3].append(kv6(kvs, db))
        outs[5].append(win_new.reshape(db, wb, KV_HEADS, 2, HEAD_DIM))
        outs[7].append(s_new)
        outs[9].append(pa)
        outs[11].append(conv_new.reshape(db, CONV_W - 1, f2))

    stacked = [jnp.stack(o) for o in outs]
    return (hp.reshape(b, t, d_model), hs.reshape(db, dt, d_model), *stacked)
```

```python
import functools

import jax
import jax.numpy as jnp
from jax import lax
from jax.experimental import pallas as pl
from jax.experimental.pallas import tpu as pltpu

F32 = jnp.float32
BF16 = jnp.bfloat16
HI = lax.Precision.HIGHEST

A_HEADS = 8
A_HEAD_DIM = 64
C_A = A_HEADS * A_HEAD_DIM
R_W = 64
R_A = 64
R_G = 128
C_SHIFT = 3 * C_A + R_W + R_A + R_G
LNX_EPS = 64e-5
Q_HEADS = 8
KV_HEADS = 2
GROUP = Q_HEADS // KV_HEADS
HEAD_DIM = 64
C_B = Q_HEADS * HEAD_DIM
C_KV = KV_HEADS * 2 * HEAD_DIM
BLK = 64
N_SEL = 16
WINDOW = 512
D_CMP = 128
FORCE_SCORE = 1e4
CONV_W = 3
NORM_EPS = 1e-6

LANES = 128
SUBLANES = 8
VMEM_BYTES_V7X = 64 * 1024 * 1024

NEG = -1e30
M_FLOOR = -1e29
KV_LANES = 2 * HEAD_DIM

GATE_PAD = LANES


def _vmem_limit(nbytes):
    return int(min(max(nbytes, 16 * 1024 * 1024), VMEM_BYTES_V7X - 8 * 1024 * 1024))


def _const_spec(shape):
    nd = len(shape)
    return pl.BlockSpec(shape, lambda *_: (0,) * nd, pipeline_mode=pl.Buffered(1))


def _rms(x, g):
    return x * lax.rsqrt(jnp.mean(x * x, axis=-1, keepdims=True) + NORM_EPS) * g


def _dot(a, b, **kw):
    return jnp.dot(a, b, preferred_element_type=F32, **kw)


def _dot_nt(a, b):
    return lax.dot_general(a, b, (((1,), (1,)), ((), ())), preferred_element_type=F32)


def _masked_softmax(s, mask):
    s = jnp.where(mask, s, NEG)
    m = jnp.max(s, axis=-1, keepdims=True)
    m = jnp.where(m > M_FLOOR, m, 0.0)
    e = jnp.exp(s - m)
    return e / jnp.maximum(jnp.sum(e, axis=-1, keepdims=True), 1e-30)


def _slope_col(hk, rows_per_head, pad_rows=0):
    cols = [jnp.full((rows_per_head, 1), 2.0 ** (-(hk * GROUP + g + 1)), F32) for g in range(GROUP)]
    if pad_rows:
        cols.append(jnp.zeros((pad_rows, 1), F32))
    return jnp.concatenate(cols, axis=0)


def _proj_body(x_ref, g_ref, w_ref, *o_refs, segs):
    xb = _rms(x_ref[...], g_ref[...]).astype(BF16)
    it = iter(o_refs)
    for off, width, outs in segs:
        r = _dot(xb, w_ref[:, off:off + width])
        for scale in outs:
            o_ref = next(it)
            o_ref[...] = (r * scale if scale != 1.0 else r).astype(o_ref.dtype)


def _proj(x, g, w_perm, tm):
    rows, d = x.shape
    d_model = d
    o = 0
    segs, shapes = [], []
    for width, outs in ((C_SHIFT, ((F32, 1.0),)), (C_B, ((BF16, HEAD_DIM ** -0.5),)), (C_KV, ((F32, 1.0),)),
                        (C_KV, ((F32, 1.0), (BF16, 1.0))), (C_KV, ((F32, 1.0), (BF16, 1.0))),
                        (GATE_PAD, ((F32, 1.0),)), (2 * d_model, ((F32, 1.0),))):
        segs.append((o, width, tuple(s for _, s in outs)))
        shapes += [(width, dt) for dt, _ in outs]
        o += width
    n_tot = o
    out_bytes = sum(tm * w * jnp.dtype(dt).itemsize for w, dt in shapes)
    vmem = 2 * tm * d * 4 + d * n_tot * 2 + 2 * out_bytes + (8 << 20)
    return pl.pallas_call(
        functools.partial(_proj_body, segs=tuple(segs)),
        grid=(rows // tm,),
        in_specs=[pl.BlockSpec((tm, d), lambda i: (i, 0)), _const_spec((1, d)), _const_spec((d, n_tot))],
        out_specs=[pl.BlockSpec((tm, w), lambda i: (i, 0)) for w, _ in shapes],
        out_shape=[jax.ShapeDtypeStruct((rows, w), dt) for w, dt in shapes],
        compiler_params=pltpu.CompilerParams(dimension_semantics=("arbitrary",), vmem_limit_bytes=_vmem_limit(vmem)),
        name="proj",
    )(x, g, w_perm)


def _prep_body(pa_ref, prev_ref, mu_ref, w0_ref, w2_ref, a0_ref, a2_ref, g2_ref, kk_ref, ka_ref, rk_ref, bd_ref,
               r_o, w_o, lw_o, k_o, v_o, a_o, b_o, g_o, bonus_o, *, seq_mode):
    pf = pa_ref[...]
    if seq_mode:
        i = pl.program_id(0)
        first = jnp.where(i > 0, prev_ref[SUBLANES - 1:SUBLANES, :], 0.0)
        rid = lax.broadcasted_iota(jnp.int32, pf.shape, 0)
        prev = jnp.where(rid == 0, first, pltpu.roll(pf, 1, 0))
    else:
        prev = prev_ref[...]
    xs = pf + (prev - pf) * mu_ref[...]
    r = xs[:, 0:C_A]
    k = xs[:, C_A:2 * C_A]
    v = xs[:, 2 * C_A:3 * C_A]
    o = 3 * C_A
    wd = xs[:, o:o + R_W]
    ad = xs[:, o + R_W:o + R_W + R_A]
    gd = xs[:, o + R_W + R_A:]
    nz = -(w0_ref[...] + _dot(jnp.tanh(wd), w2_ref[...], precision=HI))
    softplus = jnp.maximum(nz, 0.0) + jnp.log(1.0 + jnp.exp(-jnp.abs(nz)))
    w = -softplus - 0.5
    a = jax.nn.sigmoid(a0_ref[...] + _dot(ad, a2_ref[...], precision=HI))
    g = _dot(jax.nn.sigmoid(gd), g2_ref[...], precision=HI)
    kk = k * kk_ref[...]
    ss = _dot(kk * kk, bd_ref[...], precision=HI)
    kkn = kk / jnp.maximum(jnp.sqrt(ss), 1e-12)
    k2 = k * (1.0 + (a - 1.0) * ka_ref[...])
    r_o[...] = r
    lw = -jnp.exp(w)
    lw_o[...] = lw
    w_o[...] = jnp.exp(lw)
    k_o[...] = k2
    v_o[...] = v
    a_o[...] = -kkn
    b_o[...] = kkn * a
    g_o[...] = g
    bonus_o[...] = _dot(r * k2 * rk_ref[...], bd_ref[...], precision=HI) * v


def _rwkv_prep(pa, prev, rw, bd, tm, seq_mode):
    rows = pa.shape[0]
    mu, w0, w2, a0, a2, g2, k_k, k_a, r_k = rw
    if seq_mode:
        tb = tm // SUBLANES
        prev_spec = pl.BlockSpec((SUBLANES, C_SHIFT), lambda i: (jnp.maximum(i * tb - 1, 0), 0))
        prev = pa
    else:
        prev_spec = pl.BlockSpec((tm, C_SHIFT), lambda i: (i, 0))
    row = lambda z: z.reshape(1, -1)
    consts = [row(mu), row(w0), w2, row(a0), a2, g2, row(k_k), row(k_a), row(r_k), bd]
    vmem = 4 * tm * C_SHIFT * 4 + 2 * 8 * tm * C_A * 4 + 16 * tm * C_A * 4 + (8 << 20)
    return pl.pallas_call(
        functools.partial(_prep_body, seq_mode=seq_mode),
        grid=(rows // tm,),
        in_specs=[pl.BlockSpec((tm, C_SHIFT), lambda i: (i, 0)), prev_spec] + [_const_spec(c.shape) for c in consts],
        out_specs=[pl.BlockSpec((tm, C_A), lambda i: (i, 0))] * 9,
        out_shape=[jax.ShapeDtypeStruct((rows, C_A), F32)] * 9,
        compiler_params=pltpu.CompilerParams(dimension_semantics=("arbitrary",), vmem_limit_bytes=_vmem_limit(vmem)),
        name="rwkv_prep",
    )(pa, prev, *consts)


N_PAIR = A_HEADS // 2


def _scan_body(r_ref, w_ref, k_ref, v_ref, a_ref, b_ref, s0_ref, y_ref, so_ref, st_ref, *, tc, nc):
    c = pl.program_id(1)

    @pl.when(c == 0)
    def _():
        for p in range(N_PAIR):
            st_ref[p] = jnp.concatenate([s0_ref[0, 2 * p], s0_ref[0, 2 * p + 1]], axis=-1)

    shape = (A_HEAD_DIM, LANES)
    lane = lax.broadcasted_iota(jnp.int32, shape, 1)
    sub = lax.broadcasted_iota(jnp.int32, shape, 0)
    lo = lane < A_HEAD_DIM
    diag = (lane & (A_HEAD_DIM - 1)) == sub

    def seg_sum(x):
        s_lo = jnp.sum(jnp.where(lo, x, 0.0), axis=-1, keepdims=True)
        s_hi = jnp.sum(jnp.where(lo, 0.0, x), axis=-1, keepdims=True)
        return jnp.where(lo, s_lo, s_hi)

    grp = min(SUBLANES, tc)

    def token_group(gi, carry):
        base = pl.multiple_of(gi * grp, grp)
        for p in range(N_PAIR):
            sl = slice(LANES * p, LANES * (p + 1))
            rt, wt, kt, vt, at, bt = (ref[0, pl.ds(base, grp), sl] for ref in (r_ref, w_ref, k_ref, v_ref, a_ref, b_ref))
            s = st_ref[p]
            ys = []
            for j in range(grp):
                row = lambda z: z[j:j + 1, :]
                sa = seg_sum(s * row(at))
                v_col = seg_sum(jnp.where(diag, row(vt), 0.0))
                s = s * row(wt) + sa * row(bt) + v_col * row(kt)
                y_col = seg_sum(s * row(rt))
                ys.append(jnp.sum(jnp.where(diag, y_col, 0.0), axis=0, keepdims=True))
            st_ref[p] = s
            y_ref[0, pl.ds(base, grp), sl] = jnp.concatenate(ys, axis=0) if grp > 1 else ys[0]
        return carry

    lax.fori_loop(0, tc // grp, token_group, 0)

    @pl.when(c == nc - 1)
    def _():
        for p in range(N_PAIR):
            s = st_ref[p]
            so_ref[0, 2 * p] = s[:, :A_HEAD_DIM]
            so_ref[0, 2 * p + 1] = s[:, A_HEAD_DIM:]


def _rwkv_scan(seqs, s0, tc):
    b, t, _ = seqs[0].shape
    nc = t // tc
    seq_spec = pl.BlockSpec((1, tc, C_A), lambda i, c: (i, c, 0))
    st_spec = pl.BlockSpec((1, A_HEADS, A_HEAD_DIM, A_HEAD_DIM), lambda i, c: (i, 0, 0, 0))
    return pl.pallas_call(
        functools.partial(_scan_body, tc=tc, nc=nc),
        grid=(b, nc),
        in_specs=[seq_spec] * 6 + [st_spec],
        out_specs=[seq_spec, st_spec],
        out_shape=[jax.ShapeDtypeStruct((b, t, C_A), F32), jax.ShapeDtypeStruct(s0.shape, F32)],
        scratch_shapes=[pltpu.VMEM((N_PAIR, A_HEAD_DIM, LANES), F32)],
        compiler_params=pltpu.CompilerParams(dimension_semantics=("arbitrary", "arbitrary")),
        name="rwkv_scan",
    )(*seqs, s0)


SCAN_CHUNK = 64


def _mm(a, b):
    return _dot(a.astype(BF16), b.astype(BF16))


def _chunk_scan_body(r_ref, lw_ref, k_ref, v_ref, a_ref, b_ref, s0_ref, y_ref, so_ref, st_ref, *, tc, nc):
    cidx = pl.program_id(1)
    c = SCAN_CHUNK
    d = A_HEAD_DIM

    @pl.when(cidx == 0)
    def _():
        for h in range(A_HEADS):
            st_ref[h] = s0_ref[0, h].T

    ri = lax.broadcasted_iota(jnp.int32, (c, c), 0)
    ci = lax.broadcasted_iota(jnp.int32, (c, c), 1)
    lower = ci <= ri
    strict = ci < ri
    tri = lower.astype(F32)
    eye = lax.broadcasted_iota(jnp.int32, (d, d), 0) == lax.broadcasted_iota(jnp.int32, (d, d), 1)
    n_double = (c - 1).bit_length()

    def chunk(step, carry):
        base = pl.multiple_of(step * c, c)
        r, lw, k, v, a, b = (ref[0, pl.ds(base, c), :] for ref in (r_ref, lw_ref, k_ref, v_ref, a_ref, b_ref))
        cum = _dot(tri, lw, precision=HI)
        tot = cum[c - 1:c, :]
        e_inv = jnp.exp(-cum)
        e_rest = jnp.exp(tot - cum)
        at = a * jnp.exp(cum - lw)
        rt = r * jnp.exp(cum)
        bt = b * e_inv
        kt = k * e_inv
        bh = b * e_rest
        kh = k * e_rest
        g_tot = jnp.exp(tot)
        heads = range(A_HEADS)
        hsl = [slice(h * d, (h + 1) * d) for h in heads]
        g4 = [_dot_nt(jnp.concatenate([at[:, s], rt[:, s]], axis=0).astype(BF16),
                      jnp.concatenate([bt[:, s], kt[:, s]], axis=0).astype(BF16)) for s in hsl]
        lp = [jnp.where(strict, g[:c, :c], 0.0).astype(BF16) for g in g4]
        m_l = [jnp.where(strict, g[:c, c:], 0.0) for g in g4]
        p_b = [jnp.where(lower, g[c:, :c], 0.0) for g in g4]
        p_k = [jnp.where(lower, g[c:, c:], 0.0) for g in g4]
        vb = [v[:, s].astype(BF16) for s in hsl]
        z = [jnp.concatenate([at[:, hsl[h]], _mm(m_l[h], vb[h])], axis=-1) for h in heads]
        for q in range(n_double):
            z = [z[h] + _mm(lp[h], z[h]) for h in heads]
            if q < n_double - 1:
                lp = [_mm(lp[h], lp[h]).astype(BF16) for h in heads]
        zb = [zz.astype(BF16) for zz in z]
        bz = [_mm(bh[:, hsl[h]].T, zb[h]) for h in heads]
        kv = [_mm(kh[:, hsl[h]].T, vb[h]) for h in heads]
        pz = [_mm(p_b[h], zb[h]) for h in heads]
        pv = [_mm(p_k[h], vb[h]) for h in heads]
        ys = []
        for h in heads:
            st = st_ref[h]
            a_c = jnp.where(eye, g_tot[:, hsl[h]], 0.0) + bz[h][:, :d]
            ys.append(_dot(rt[:, hsl[h]] + pz[h][:, :d], st, precision=HI) + pz[h][:, d:] + pv[h])
            st_ref[h] = _dot(a_c, st, precision=HI) + bz[h][:, d:] + kv[h]
        y_ref[0, pl.ds(base, c), :] = jnp.concatenate(ys, axis=-1)
        return carry

    lax.fori_loop(0, tc // c, chunk, 0)

    @pl.when(cidx == nc - 1)
    def _():
        for h in range(A_HEADS):
            so_ref[0, h] = st_ref[h].T


def _rwkv_chunk_scan(seqs, s0, tc):
    b, t, _ = seqs[0].shape
    nc = t // tc
    seq_spec = pl.BlockSpec((1, tc, C_A), lambda i, c: (i, c, 0))
    st_spec = pl.BlockSpec((1, A_HEADS, A_HEAD_DIM, A_HEAD_DIM), lambda i, c: (i, 0, 0, 0))
    return pl.pallas_call(
        functools.partial(_chunk_scan_body, tc=tc, nc=nc),
        grid=(b, nc),
        in_specs=[seq_spec] * 6 + [st_spec],
        out_specs=[seq_spec, st_spec],
        out_shape=[jax.ShapeDtypeStruct((b, t, C_A), F32), jax.ShapeDtypeStruct(s0.shape, F32)],
        scratch_shapes=[pltpu.VMEM((A_HEADS, A_HEAD_DIM, A_HEAD_DIM), F32)],
        compiler_params=pltpu.CompilerParams(dimension_semantics=("arbitrary", "arbitrary")),
        name="rwkv_chunk_scan",
    )(*seqs, s0)


def _gelu_tanh(x):
    return 0.5 * x * (1.0 + jnp.tanh(0.7978845608028654 * (x + 0.044715 * (x * x * x))))


def _compress_body(x_ref, w_ref, b_ref, w2_ref, o_ref, acc_ref, *, nk, mlp):
    kk = pl.program_id(1)

    @pl.when(kk == 0)
    def _():
        acc_ref[...] = jnp.zeros_like(acc_ref)

    acc_ref[...] += _dot(x_ref[...].astype(BF16), w_ref[...])

    @pl.when(kk == nk - 1)
    def _():
        h = acc_ref[...] + b_ref[...]
        if mlp:
            o_ref[...] = _dot(_gelu_tanh(h).astype(BF16), w2_ref[...])
        else:
            o_ref[...] = h


def _compress(x2, w_big, bias, w2_big, tm, tk, mlp=True):
    m, kdim = x2.shape
    nh = w_big.shape[1]
    n_out = w2_big.shape[1] if mlp else nh
    nk = kdim // tk
    vmem = 2 * tm * tk * 4 + 2 * tk * nh * 2 + 3 * tm * nh * 4 + 2 * tm * n_out * 4 + (8 << 20)
    return pl.pallas_call(
        functools.partial(_compress_body, nk=nk, mlp=mlp),
        grid=(m // tm, nk),
        in_specs=[pl.BlockSpec((tm, tk), lambda i, k: (i, k)), pl.BlockSpec((tk, nh), lambda i, k: (k, 0)),
                  _const_spec(bias.shape), _const_spec(w2_big.shape)],
        out_specs=pl.BlockSpec((tm, n_out), lambda i, k: (i, 0)),
        out_shape=jax.ShapeDtypeStruct((m, n_out), F32),
        scratch_shapes=[pltpu.VMEM((tm, nh), F32)],
        compiler_params=pltpu.CompilerParams(dimension_semantics=("arbitrary", "arbitrary"),
                                             vmem_limit_bytes=_vmem_limit(vmem)),
        name="nsa_compress",
    )(x2, w_big, bias, w2_big)


PAGE_ROWS = KV_HEADS * 2 * HEAD_DIM


def _compress_pool_body(x_ref, w_ref, b_ref, w2_ref, o_ref, *, pg):
    m = pg * KV_HEADS
    for c in range(2):
        acc = jnp.zeros((m, 2 * D_CMP), F32)
        for d in range(HEAD_DIM):
            r0 = c * HEAD_DIM + d
            x_cd = x_ref[pl.ds(r0, m, stride=2 * HEAD_DIM), :].astype(BF16)
            acc = acc + _dot(x_cd, w_ref[r0 * 2 * BLK:(r0 + 1) * 2 * BLK, :])
        h = acc + b_ref[c:c + 1, :]
        o_ref[:, c * 2 * HEAD_DIM:(c + 1) * 2 * HEAD_DIM] = _dot(_gelu_tanh(h).astype(BF16), w2_ref[c])


def _compress_pool(x2, w_tok, bias2, w2_pair, pg):
    n_pool = x2.shape[0] // PAGE_ROWS
    m = pg * KV_HEADS
    vmem = 2 * pg * PAGE_ROWS * 2 * BLK * 4 + int(w_tok.size) * 2 + 8 * m * 2 * D_CMP * 4 + (8 << 20)
    return pl.pallas_call(
        functools.partial(_compress_pool_body, pg=pg),
        grid=(n_pool // pg,),
        in_specs=[pl.BlockSpec((pg * PAGE_ROWS, 2 * BLK), lambda i: (i, 0)), _const_spec(w_tok.shape),
                  _const_spec(bias2.shape), _const_spec(w2_pair.shape)],
        out_specs=pl.BlockSpec((m, C_KV), lambda i: (i, 0)),
        out_shape=jax.ShapeDtypeStruct((n_pool * KV_HEADS, C_KV), F32),
        compiler_params=pltpu.CompilerParams(dimension_semantics=("arbitrary",), vmem_limit_bytes=_vmem_limit(vmem)),
        name="nsa_compress_pool",
    )(x2, w_tok, bias2, w2_pair)


def _topk_mask(s, ids, k, axis):
    sel = jnp.zeros(s.shape, F32)
    for _ in range(k):
        m = jnp.max(s, axis=axis, keepdims=True)
        pick = jnp.min(jnp.where(s == m, ids, 1e9), axis=axis, keepdims=True)
        hit = ids == pick
        sel = jnp.where(hit, 1.0, sel)
        s = jnp.where(hit, -jnp.inf, s)
    return sel


def _head_rows(q_ref, hk, pad_rows=0, pad_lanes=True):
    parts = [q_ref[:, (hk * GROUP + g) * HEAD_DIM:(hk * GROUP + g + 1) * HEAD_DIM] for g in range(GROUP)]
    if pad_rows:
        parts.append(jnp.zeros((pad_rows, HEAD_DIM), parts[0].dtype))
    qh = jnp.concatenate(parts, axis=0)
    return jnp.concatenate([qh, jnp.zeros_like(qh)], axis=-1) if pad_lanes else qh


AUG_HI = HEAD_DIM
AUG_LO = HEAD_DIM + 1
AUG_ONE = HEAD_DIM + 2
AUG_BLK = HEAD_DIM + 3
POS_SPLIT = LANES


def _nsa_prompt_body(q_ref, gt_ref, kc_ref, kct_ref, ka_ref, vat_ref, kw_ref, em_ref, o_ref, *, tq, tk, nbc):
    i = pl.program_id(0)
    t0 = i * tq
    rows = GROUP * tq
    qpos1 = t0 + lax.broadcasted_iota(jnp.int32, (tq, 1), 0)
    qpos = jnp.concatenate([qpos1] * GROUP, axis=0)
    qposf = qpos.astype(F32)
    qrow1 = t0 + lax.broadcasted_iota(jnp.int32, (1, tq), 1)
    qrow = jnp.concatenate([qrow1] * GROUP, axis=1)
    cur_row = qrow1 // BLK
    blkc = lax.broadcasted_iota(jnp.int32, (nbc, 1), 0)
    blkcf = blkc.astype(F32)
    blk_row = lax.broadcasted_iota(jnp.int32, (1, nbc), 1)
    lane = lax.broadcasted_iota(jnp.int32, (1, KV_LANES), 1)
    gt = jax.nn.sigmoid(gt_ref[...])
    bpt = tk // BLK
    t0f = t0.astype(F32)
    o_cs, o_ws, not_sels, q_augs = [], [], [], []
    for hk in range(KV_HEADS):
        hl = slice(hk * KV_LANES, (hk + 1) * KV_LANES)
        qh = _head_rows(q_ref, hk)
        slope = _slope_col(hk, tq)
        slope_row = jnp.concatenate([jnp.full((1, tq), 2.0 ** (-(hk * GROUP + g + 1)), F32) for g in range(GROUP)],
                                    axis=1)
        kcb = kc_ref[:, hl].astype(BF16)
        cposc = blkc * BLK + (BLK - 1)
        s_t = _dot_nt(kcb, qh) - slope_row * (qrow - cposc).astype(F32)
        s_t = jnp.where(cposc <= qrow, s_t, NEG)
        m_c = jnp.max(s_t, axis=0, keepdims=True)
        m_c = jnp.where(m_c > M_FLOOR, m_c, 0.0)
        e_c = jnp.exp(s_t - m_c)
        p_t = e_c / jnp.maximum(jnp.sum(e_c, axis=0, keepdims=True), 1e-30)
        o_c = _dot(kct_ref[hl, :].astype(BF16), p_t.astype(BF16))[HEAD_DIM:, :].T
        score = p_t[:, 0:tq]
        for g in range(1, GROUP):
            score = score + p_t[:, g * tq:(g + 1) * tq]
        forced = (blkc == 0) | (blkc == cur_row) | (blkc == cur_row - 1)
        sc = jnp.where(forced, FORCE_SCORE, score)
        sc = jnp.where(blkc > cur_row, -1.0, sc)
        sel_t = _topk_mask(sc, blkcf, min(N_SEL, nbc), axis=0)
        not_sels.append(jnp.where(blkc <= cur_row, 1.0 - sel_t, 1.0).T)
        o_cs.append(o_c)

        wl = WINDOW + tq
        ws = pl.multiple_of(jnp.maximum(t0 - WINDOW, 0), tq)
        kvw = kw_ref[pl.ds(ws, wl), hl]
        kposw = ws + lax.broadcasted_iota(jnp.int32, (1, wl), 1)
        s = _dot_nt(qh, kvw) - slope * (qposf - kposw.astype(F32))
        p_w = _masked_softmax(s, (kposw <= qpos) & (qpos - kposw < WINDOW))
        o_ws.append(_dot(p_w.astype(BF16), kvw)[:, HEAD_DIM:])

        q_augs.append(qh.astype(F32) + jnp.where(lane == AUG_HI, slope * POS_SPLIT, 0.0)
                      + jnp.where(lane == AUG_LO, slope, 0.0) + jnp.where(lane == AUG_ONE, -slope * t0f, 0.0))

    def sel_step(j, carry, causal):
        k0 = pl.multiple_of(j * tk, tk)
        tile_blocks = blk_row // bpt == j
        out = []
        for hk in range(KV_HEADS):
            m, acc = carry[hk]
            mk = _dot(jnp.where(tile_blocks, not_sels[hk], 0.0).astype(BF16), em_ref[...])
            q_j = (q_augs[hk] + jnp.concatenate([mk] * GROUP, axis=0)).astype(BF16)
            s = _dot_nt(ka_ref[hk, pl.ds(k0, tk), :], q_j)
            if causal:
                kpos = k0 + lax.broadcasted_iota(jnp.int32, (tk, 1), 0)
                s = jnp.where(kpos <= qrow, s, NEG)
            m_new = jnp.maximum(m, jnp.max(s, axis=0, keepdims=True))
            p = jnp.exp(s - m_new).astype(BF16)
            out.append((m_new, jnp.exp(m - m_new) * acc + _dot(vat_ref[hk, j], p)))
        return tuple(out)

    n_tiles = (t0 + tq + tk - 1) // tk
    init = tuple((jnp.full((1, rows), M_FLOOR, F32), jnp.zeros((KV_LANES, rows), F32)) for _ in range(KV_HEADS))
    carry = lax.fori_loop(0, n_tiles - 1, functools.partial(sel_step, causal=False), init)
    carry = sel_step(n_tiles - 1, carry, True)

    pieces = []
    for hk in range(KV_HEADS):
        acc = carry[hk][1]
        o_s = (acc[:HEAD_DIM] / jnp.maximum(acc[HEAD_DIM:HEAD_DIM + 1], 1e-30)).T
        for g in range(GROUP):
            c0 = (hk * GROUP + g) * 3
            rs = slice(g * tq, (g + 1) * tq)
            pieces.append(gt[:, c0:c0 + 1] * o_cs[hk][rs] + gt[:, c0 + 1:c0 + 2] * o_s[rs]
                          + gt[:, c0 + 2:c0 + 3] * o_ws[hk][rs])
    o_ref[...] = jnp.concatenate(pieces, axis=-1)


def _nsa_prompt(q, gates, kvc, ks, kw, tq, tk):
    t = q.shape[0]
    nbc = kvc.shape[0]
    bpt = tk // BLK
    assert t <= POS_SPLIT * 256 and AUG_BLK + bpt <= KV_LANES and t % tk == 0
    em = jnp.where(jnp.arange(KV_LANES)[None, :] == AUG_BLK + jnp.arange(nbc)[:, None] % bpt, NEG, 0.0).astype(BF16)
    pos = jnp.arange(t, dtype=jnp.int32)[:, None]
    aug_lane = jnp.arange(HEAD_DIM, dtype=jnp.int32)[None, :] + HEAD_DIM
    k_aug = (jnp.where(aug_lane == AUG_HI, pos // POS_SPLIT, 0) + jnp.where(aug_lane == AUG_LO, pos % POS_SPLIT, 0)
             + jnp.where(aug_lane == AUG_ONE, 1, 0)
             + jnp.where(aug_lane == AUG_BLK + (pos // BLK) % bpt, 1, 0)).astype(BF16)
    v_aug = jnp.broadcast_to(jnp.where(aug_lane == HEAD_DIM, 1, 0).astype(BF16), (t, HEAD_DIM))
    ka = jnp.stack([jnp.concatenate([ks[:, h * KV_LANES:h * KV_LANES + HEAD_DIM], k_aug], axis=1)
                    for h in range(KV_HEADS)])
    va = jnp.stack([jnp.concatenate([ks[:, h * KV_LANES + HEAD_DIM:(h + 1) * KV_LANES], v_aug], axis=1)
                    for h in range(KV_HEADS)])
    va = jnp.transpose(va.reshape(KV_HEADS, t // tk, tk, KV_LANES), (0, 1, 3, 2))
    kct = kvc.T
    vmem = 3 * t * C_KV * 2 + 2 * nbc * C_KV * 4 + 24 * GROUP * tq * max(tk, WINDOW + tq) * 4 + (8 << 20)
    return pl.pallas_call(
        functools.partial(_nsa_prompt_body, tq=tq, tk=tk, nbc=nbc),
        grid=(t // tq,),
        in_specs=[pl.BlockSpec((tq, C_B), lambda i: (i, 0)), pl.BlockSpec((tq, GATE_PAD), lambda i: (i, 0)),
                  _const_spec(kvc.shape), _const_spec(kct.shape), _const_spec(ka.shape), _const_spec(va.shape),
                  _const_spec(kw.shape), _const_spec(em.shape)],
        out_specs=pl.BlockSpec((tq, C_B), lambda i: (i, 0)),
        out_shape=jax.ShapeDtypeStruct((t, C_B), F32),
        compiler_params=pltpu.CompilerParams(dimension_semantics=("arbitrary",), vmem_limit_bytes=_vmem_limit(vmem)),
        name="nsa_prompt",
    )(q, gates, kvc, kct, ka, va, kw, em)


PAGE_BLOCKS = 2
Q_ROWS = SUBLANES


def _samp_cmp_body(pt_ref, q_ref, pool_ref, oc_ref, idx_ref, kbuf, sem, *, n_pages, past):
    n = pl.program_id(0)

    def page_copy(jp):
        return pltpu.make_async_copy(pool_ref.at[pl.ds(pt_ref[n, jp], 1), :], kbuf.at[pl.ds(jp, 1), :], sem)

    def start(jp, c):
        page_copy(jp).start()
        return c

    def wait(jp, c):
        page_copy(jp).wait()
        return c

    lax.fori_loop(0, n_pages, start, 0)
    lax.fori_loop(0, n_pages, wait, 0)

    nb_past = n_pages * PAGE_BLOCKS
    n_all = nb_past + 1
    qpos = past
    cur = qpos // BLK
    width = PAGE_BLOCKS * n_pages
    lane = lax.broadcasted_iota(jnp.int32, (1, width), 1)
    bid = jnp.where(lane < n_pages, PAGE_BLOCKS * lane, PAGE_BLOCKS * (lane - n_pages) + 1)
    cpos = bid * BLK + (BLK - 1)
    wide = 2 * width
    lane2 = lax.broadcasted_iota(jnp.int32, (1, wide), 1)
    bid2 = jnp.concatenate([bid, nb_past + lane], axis=-1)
    bid2f = bid2.astype(F32)
    for hk in range(KV_HEADS):
        q8 = _head_rows(q_ref.at[0], hk, pad_rows=Q_ROWS - GROUP, pad_lanes=False)
        z8 = jnp.zeros_like(q8)
        slope = _slope_col(hk, 1, pad_rows=Q_ROWS - GROUP)
        kmat = kbuf[:, hk * C_KV:hk * C_KV + KV_LANES].astype(BF16)
        vmat = kbuf[:, hk * C_KV + KV_LANES:(hk + 1) * C_KV].astype(BF16)
        s = jnp.concatenate([_dot_nt(jnp.concatenate([q8, z8], axis=-1), kmat),
                             _dot_nt(jnp.concatenate([z8, q8], axis=-1), kmat)], axis=-1)
        s = s - slope * (qpos - cpos).astype(F32)
        p_c = _masked_softmax(s, cpos <= qpos)
        o_even = _dot(p_c[:, :n_pages].astype(BF16), vmat)
        o_odd = _dot(p_c[:, n_pages:].astype(BF16), vmat)
        oc_ref[0, hk] = o_even[:, :HEAD_DIM] + o_odd[:, HEAD_DIM:]
        score = jnp.sum(p_c[0:GROUP], axis=0, keepdims=True)
        sc = jnp.concatenate([score, jnp.zeros((1, width), F32)], axis=-1)
        forced = (bid2 == 0) | (bid2 == cur) | (bid2 == cur - 1)
        sc = jnp.where(forced, FORCE_SCORE, sc)
        sc = jnp.where(bid2 > cur, -1.0, sc)
        sc = jnp.where(bid2 < n_all, sc, -jnp.inf)
        res = jnp.zeros((1, wide), F32)
        for it in range(N_SEL):
            m = jnp.max(sc, axis=-1, keepdims=True)
            pick = jnp.min(jnp.where(sc == m, bid2f, 1e9), axis=-1, keepdims=True)
            res = jnp.where(lane2 == it, pick, res)
            sc = jnp.where(bid2f == pick, -jnp.inf, sc)
        idx_ref[0, hk] = jnp.broadcast_to(res[:, :LANES], (SUBLANES, LANES)).astype(jnp.int32)


def _samp_cmp(page_table, q3, pool2, past):
    db, n_pages = page_table.shape
    grid_spec = pltpu.PrefetchScalarGridSpec(
        num_scalar_prefetch=1,
        grid=(db,),
        in_specs=[pl.BlockSpec((1, 1, C_B), lambda n, pt: (n, 0, 0)), pl.BlockSpec(memory_space=pl.ANY)],
        out_specs=[pl.BlockSpec((1, KV_HEADS, Q_ROWS, HEAD_DIM), lambda n, pt: (n, 0, 0, 0)),
                   pl.BlockSpec((1, KV_HEADS, SUBLANES, LANES), lambda n, pt: (n, 0, 0, 0))],
        scratch_shapes=[pltpu.VMEM((n_pages, KV_HEADS * C_KV), F32), pltpu.SemaphoreType.DMA(())],
    )
    return pl.pallas_call(
        functools.partial(_samp_cmp_body, n_pages=n_pages, past=past),
        grid_spec=grid_spec,
        out_shape=[jax.ShapeDtypeStruct((db, KV_HEADS, Q_ROWS, HEAD_DIM), F32),
                   jax.ShapeDtypeStruct((db, KV_HEADS, SUBLANES, LANES), jnp.int32)],
        compiler_params=pltpu.CompilerParams(dimension_semantics=("arbitrary",)),
        name="nsa_sample_cmp",
    )(page_table, q3, pool2)


def _samp_sel_body(pt_ref, idx_ref, q_ref, gt_ref, oc_ref, ksc_ref, kwr_ref, kwc_ref, win_ref, cache_ref,
                   o_ref, wino_ref, sbuf, sems, *, nb_past, past, wb):
    n = pl.program_id(0)
    qpos = past

    page_tok = PAGE_BLOCKS * BLK
    tok = lax.broadcasted_iota(jnp.int32, (HEAD_DIM, page_tok), 1)

    def page_copy(hk, s, ib):
        page = pt_ref[n, ib // PAGE_BLOCKS]
        return pltpu.make_async_copy(cache_ref.at[page, hk], sbuf.at[hk, :, :, pl.ds(s * page_tok, page_tok)],
                                     sems.at[hk * N_SEL + s])

    for hk in range(KV_HEADS):
        for s in range(N_SEL):
            ib = idx_ref[n, hk * N_SEL + s]

            @pl.when(ib < nb_past)
            def _():
                page_copy(hk, s, ib).start()

            @pl.when(ib >= nb_past)
            def _():
                for c in range(2):
                    sbuf[hk, c, :, s * page_tok:(s + 1) * page_tok] = jnp.where(tok == 0, ksc_ref[0, 2 * hk + c], 0.0)

    wtok = lax.broadcasted_iota(jnp.int32, (HEAD_DIM, wb), 1)
    for hk in range(KV_HEADS):
        for c in range(2):
            wino_ref[0, hk, c] = jnp.where(wtok == wb - 1, kwc_ref[0, 2 * hk + c],
                                           pltpu.roll(win_ref[0, hk, c], wb - 1, 1))

    for hk in range(KV_HEADS):
        for s in range(N_SEL):
            ib = idx_ref[n, hk * N_SEL + s]

            @pl.when(ib < nb_past)
            def _():
                page_copy(hk, s, ib).wait()

    gt = jax.nn.sigmoid(gt_ref[0])
    nk = N_SEL * page_tok
    lane = lax.broadcasted_iota(jnp.int32, (1, nk), 1)
    kposw = past - wb + lax.broadcasted_iota(jnp.int32, (1, wb), 1)
    wmask = (kposw <= qpos) & (qpos - kposw < WINDOW) & (kposw >= 0)
    pieces = []
    for hk in range(KV_HEADS):
        q8 = _head_rows(q_ref.at[0], hk, pad_rows=Q_ROWS - GROUP, pad_lanes=False)
        slope = _slope_col(hk, 1, pad_rows=Q_ROWS - GROUP)
        ibv = jnp.zeros((1, nk), jnp.int32)
        for s in range(N_SEL):
            ibv = jnp.where(lane // page_tok == s, idx_ref[n, hk * N_SEL + s], ibv)
        spos = (ibv - ibv % PAGE_BLOCKS) * BLK + lane % page_tok
        s_s = _dot(q8, sbuf[hk, 0].astype(BF16)) - slope * (qpos - spos).astype(F32)
        p_s = _masked_softmax(s_s, (spos // BLK == ibv) & (spos <= qpos))
        o_s = _dot_nt(p_s.astype(BF16), sbuf[hk, 1].astype(BF16))
        kwn = kwr_ref[0]
        k_new = kwn[:, hk * KV_LANES:hk * KV_LANES + HEAD_DIM]
        v_new = kwn[:, hk * KV_LANES + HEAD_DIM:(hk + 1) * KV_LANES]
        s_w = _dot(q8, win_ref[0, hk, 0].astype(BF16)) - slope * (qpos - kposw).astype(F32)
        s_w = jnp.where(wmask, s_w, NEG)
        s_n = jnp.sum(q8.astype(F32) * k_new, axis=-1, keepdims=True)
        m_w = jnp.maximum(jnp.max(s_w, axis=-1, keepdims=True), s_n)
        e_w = jnp.exp(s_w - m_w)
        e_n = jnp.exp(s_n - m_w)
        den = jnp.sum(e_w, axis=-1, keepdims=True) + e_n
        o_w = (_dot_nt(e_w.astype(BF16), win_ref[0, hk, 1].astype(BF16)) + e_n * v_new) / den
        o_c = oc_ref[0, hk]
        for g in range(GROUP):
            c0 = (hk * GROUP + g) * 3
            pieces.append(gt[:, c0:c0 + 1] * o_c[g:g + 1] + gt[:, c0 + 1:c0 + 2] * o_s[g:g + 1]
                          + gt[:, c0 + 2:c0 + 3] * o_w[g:g + 1])
    o_ref[0] = jnp.concatenate(pieces, axis=-1)


def _samp_sel(page_table, idx, q3, gates3, oc, ks_cols, kw_row, kw_cols, win_t, cache_t, past):
    db, n_pages = page_table.shape
    wb = win_t.shape[-1]
    nb_past = n_pages * PAGE_BLOCKS
    row3 = lambda w: pl.BlockSpec((1, 1, w), lambda n, pt, ix: (n, 0, 0))
    col4 = pl.BlockSpec((1, 2 * KV_HEADS, HEAD_DIM, 1), lambda n, pt, ix: (n, 0, 0, 0))
    win_spec = pl.BlockSpec((1, KV_HEADS, 2, HEAD_DIM, wb), lambda n, pt, ix: (n, 0, 0, 0, 0))
    grid_spec = pltpu.PrefetchScalarGridSpec(
        num_scalar_prefetch=2,
        grid=(db,),
        in_specs=[row3(C_B), row3(GATE_PAD),
                  pl.BlockSpec((1, KV_HEADS, Q_ROWS, HEAD_DIM), lambda n, pt, ix: (n, 0, 0, 0)),
                  col4, row3(C_KV), col4, win_spec, pl.BlockSpec(memory_space=pl.ANY)],
        out_specs=[row3(C_B), win_spec],
        scratch_shapes=[pltpu.VMEM((KV_HEADS, 2, HEAD_DIM, N_SEL * PAGE_BLOCKS * BLK), F32),
                        pltpu.SemaphoreType.DMA((KV_HEADS * N_SEL,))],
    )
    return pl.pallas_call(
        functools.partial(_samp_sel_body, nb_past=nb_past, past=past, wb=wb),
        grid_spec=grid_spec,
        out_shape=[jax.ShapeDtypeStruct((db, 1, C_B), F32), jax.ShapeDtypeStruct(win_t.shape, F32)],
        compiler_params=pltpu.CompilerParams(dimension_semantics=("arbitrary",)),
        name="nsa_sample_sel",
    )(page_table, idx, q3, gates3, oc, ks_cols, kw_row, kw_cols, win_t, cache_t)


FF_CHUNK = 256


def _merge_body(x_ref, y_ref, bonus_ref, g_ref, yb_ref, mg_ref, p_ref, cbuf_ref,
                lng_ref, lnb_ref, bd_ref, woa_ref, wob_ref, wout_ref, n2_ref, wup_ref, cw_ref, cb_ref, wdn_ref,
                n3_ref, wpe_ref, wpg_ref, fg_ref, o_ref, cnew_ref, carry_ref, *, seq_mode, final, d_ff, tm):
    i = pl.program_id(0)
    d_model = x_ref.shape[1]
    y = y_ref[...]
    inv = 1.0 / A_HEAD_DIM
    mean = _dot(y, bd_ref[...], precision=HI) * inv
    d = y - mean
    var = _dot(d * d, bd_ref[...], precision=HI) * inv
    ya = (d * lax.rsqrt(var + LNX_EPS) * lng_ref[...] + lnb_ref[...] + bonus_ref[...]) * g_ref[...]
    mg = mg_ref[...]
    m = (jax.nn.sigmoid(mg[:, :d_model]) * _dot(ya.astype(BF16), woa_ref[...])
         + jax.nn.sigmoid(mg[:, d_model:]) * _dot(yb_ref[...].astype(BF16), wob_ref[...]))
    h = x_ref[...] + _dot(m.astype(BF16), wout_ref[...])
    xn = _rms(h, n2_ref[...]).astype(BF16)

    if seq_mode:
        @pl.when(i == 0)
        def _():
            carry_ref[...] = jnp.zeros_like(carry_ref)
        rid = lax.broadcasted_iota(jnp.int32, (tm, FF_CHUNK), 0)

    acc = jnp.zeros((tm, d_model), F32)
    for c in range(d_ff // FF_CHUNK):
        parts = []
        for half in range(2):
            cs = slice(half * d_ff + c * FF_CHUNK, half * d_ff + (c + 1) * FF_CHUNK)
            up = _dot(xn, wup_ref[:, cs])
            if seq_mode:
                t1 = carry_ref[SUBLANES - 1:SUBLANES, cs]
                t2 = carry_ref[SUBLANES - 2:SUBLANES - 1, cs]
                up1 = jnp.where(rid == 0, t1, pltpu.roll(up, 1, 0))
                up2 = jnp.where(rid == 0, t2, jnp.where(rid == 1, t1, pltpu.roll(up, 2, 0)))
                carry_ref[:, cs] = up[tm - SUBLANES:, :]
            else:
                up2 = cbuf_ref[:, cs]
                up1 = cbuf_ref[:, 2 * d_ff + cs.start:2 * d_ff + cs.stop]
                cnew_ref[:, cs] = up1
                cnew_ref[:, 2 * d_ff + cs.start:2 * d_ff + cs.stop] = up
            parts.append(cb_ref[:, cs] + cw_ref[0:1, cs] * up2 + cw_ref[1:2, cs] * up1 + cw_ref[2:3, cs] * up)
        a, gate = parts
        act = (a * jax.nn.sigmoid(a) * gate).astype(BF16)
        acc = acc + _dot(act, wdn_ref[c * FF_CHUNK:(c + 1) * FF_CHUNK, :])
    if seq_mode:
        cnew_ref[...] = carry_ref[...]
    h = h + acc
    pe = _dot(p_ref[...].astype(BF16), wpe_ref[...])
    h = h + pe * jax.nn.sigmoid(_dot(_rms(h, n3_ref[...]).astype(BF16), wpg_ref[...]))
    o_ref[...] = _rms(h, fg_ref[...]) if final else h


def _merge(x, y, bonus, g, yb, mg, p, cbuf, consts, tm, seq_mode, final):
    rows, d_model = x.shape
    d_ff = consts[10].shape[0]
    f2 = 2 * d_ff
    rowspec = lambda w: pl.BlockSpec((tm, w), lambda i: (i, 0))
    if seq_mode:
        cbuf_spec = _const_spec(cbuf.shape)
        cnew_shape, cnew_spec = (SUBLANES, f2), pl.BlockSpec((SUBLANES, f2), lambda i: (0, 0))
    else:
        cbuf_spec = rowspec(2 * f2)
        cnew_shape, cnew_spec = (rows, 2 * f2), rowspec(2 * f2)
    wbytes = sum(int(c.size) * c.dtype.itemsize for c in consts)
    act = tm * (d_model * 3 + C_A * 4 + 256) * 4 + (0 if seq_mode else 2 * tm * 2 * f2 * 4)
    vmem = wbytes + 2 * act + 8 * tm * d_model * 4 + (8 << 20)
    return pl.pallas_call(
        functools.partial(_merge_body, seq_mode=seq_mode, final=final, d_ff=d_ff, tm=tm),
        grid=(rows // tm,),
        in_specs=[rowspec(d_model), rowspec(C_A), rowspec(C_A), rowspec(C_A), rowspec(C_B), rowspec(2 * d_model),
                  rowspec(p.shape[1]), cbuf_spec] + [_const_spec(c.shape) for c in consts],
        out_specs=[rowspec(d_model), cnew_spec],
        out_shape=[jax.ShapeDtypeStruct((rows, d_model), F32), jax.ShapeDtypeStruct(cnew_shape, F32)],
        scratch_shapes=[pltpu.VMEM((SUBLANES, f2), F32)],
        compiler_params=pltpu.CompilerParams(dimension_semantics=("arbitrary",), vmem_limit_bytes=_vmem_limit(vmem)),
        name="merge_ffn",
    )(x, y, bonus, g, yb, mg, p, cbuf, *consts)


def _pick_tile(n, target):
    t = min(n, target)
    while n % t:
        t //= 2
    return t


def _head_block_diag():
    h = jnp.arange(C_A) // A_HEAD_DIM
    return (h[:, None] == h[None, :]).astype(F32)


def _compress_weights(pe, w1, b1, w2):
    eye = jnp.eye(KV_HEADS, dtype=F32)
    eye_c = jnp.eye(2, dtype=F32)
    w_big = jnp.einsum('crdf,hg,ce->rhcdgef', w1, eye, eye_c).reshape(BLK * C_KV, KV_HEADS * 2 * D_CMP)
    w2_big = jnp.einsum('cfd,hg,ce->hcfged', w2, eye, eye_c).reshape(KV_HEADS * 2 * D_CMP, C_KV)
    pe_row = jnp.broadcast_to(jnp.transpose(pe, (1, 0, 2))[:, None], (BLK, KV_HEADS, 2, HEAD_DIM)).reshape(1, -1)
    b1_row = jnp.broadcast_to(b1[None], (KV_HEADS, 2, D_CMP)).reshape(1, -1)
    return w_big.astype(BF16), w2_big.astype(BF16), pe_row, b1_row


def kernel(x_prompt, x_sample, p_prompt, p_sample, cache_cmp_kv, cache_sel_kv, page_table, state_win_kv, state_wkv, state_shift, state_ffn_conv, norm1_g, w_in, shift_mu, rwkv_w0, rwkv_w2, rwkv_a0, rwkv_a2, rwkv_g2, rwkv_k_k, rwkv_k_a, rwkv_r_k, lnx_g, lnx_b, cmp_pe, cmp_w1, cmp_b1, cmp_w2, w_oa, w_ob, w_out, norm2_g, w_up, conv_w, conv_b, w_down, norm3_g, w_pe, w_pg, final_g):
    depth = w_in.shape[0]
    b, t, d_model = x_prompt.shape
    db, dt, _ = x_sample.shape
    n_pool, page = cache_cmp_kv.shape[1], cache_cmp_kv.shape[2]
    n_pages = page_table.shape[1]
    past = n_pages * page
    d_ff = w_down.shape[1]
    f2 = 2 * d_ff
    wb = state_win_kv.shape[2]
    assert b == 1 and dt == 1 and page == PAGE_BLOCKS * BLK
    assert t % BLK == 0 and t >= WINDOW + 128 and d_ff % FF_CHUNK == 0 and wb == WINDOW and past >= WINDOW
    assert N_SEL <= LANES and n_pages * PAGE_BLOCKS + 1 >= N_SEL

    bd = _head_block_diag()
    hp = x_prompt.reshape(t, d_model)
    hs = x_sample.reshape(db, d_model)
    outs = [[] for _ in range(12)]
    for i in range(depth):
        o_g = C_SHIFT + C_B + 3 * C_KV
        w_perm = jnp.concatenate(
            [w_in[i][:, :o_g], jnp.pad(w_in[i][:, o_g:o_g + 3 * Q_HEADS], ((0, 0), (0, GATE_PAD - 3 * Q_HEADS))),
             w_in[i][:, o_g + 3 * Q_HEADS:]], axis=1).astype(BF16)
        g1 = norm1_g[i].reshape(1, -1)
        rw = (shift_mu[i], rwkv_w0[i], rwkv_w2[i], rwkv_a0[i], rwkv_a2[i], rwkv_g2[i], rwkv_k_k[i], rwkv_k_a[i],
              rwkv_r_k[i].reshape(-1))
        w_big, w2_big, pe_row, b1_row = _compress_weights(cmp_pe[i], cmp_w1[i], cmp_b1[i], cmp_w2[i])
        zero_bias = jnp.zeros_like(b1_row)
        cmp_bias = _compress(jnp.broadcast_to(pe_row, (SUBLANES, pe_row.shape[1])), w_big, zero_bias, w2_big,
                             SUBLANES, 2048, mlp=False)[0:1] + b1_row
        eye_b = jnp.eye(PAGE_BLOCKS, dtype=F32)
        w_tok = jnp.einsum('crdf,eg->cdergf', cmp_w1[i], eye_b).reshape(2 * HEAD_DIM * PAGE_BLOCKS * BLK,
                                                                        PAGE_BLOCKS * D_CMP).astype(BF16)
        w2_pair = jnp.einsum('cfd,eg->cefgd', cmp_w2[i], eye_b).reshape(2, PAGE_BLOCKS * D_CMP,
                                                                       PAGE_BLOCKS * HEAD_DIM).astype(BF16)
        bias_pair = jnp.tile(cmp_bias[0, :2 * D_CMP].reshape(2, D_CMP), (1, PAGE_BLOCKS))
        row = lambda z: z.reshape(1, -1)
        mconsts = [row(lnx_g[i]), row(lnx_b[i]), bd, w_oa[i].astype(BF16), w_ob[i].astype(BF16),
                   w_out[i].astype(BF16), row(norm2_g[i]), w_up[i].astype(BF16), conv_w[i], row(conv_b[i]),
                   w_down[i].astype(BF16), row(norm3_g[i]), w_pe[i].astype(BF16), w_pg[i].astype(BF16), row(final_g)]
        final = i == depth - 1

        pa, q, kvc, kvs, kvs16, kvw, kvw16, gates, mg = _proj(hp, g1, w_perm, _pick_tile(t, 256))
        seqs = _rwkv_prep(pa, None, rw, bd, _pick_tile(t, 512), True)
        r_, w_, lw_, k_, v_, a_, b_, g_, bonus = seqs
        y, s_new = _rwkv_chunk_scan([z.reshape(1, t, C_A) for z in (r_, lw_, k_, v_, a_, b_)],
                                    jnp.zeros((1, A_HEADS, A_HEAD_DIM, A_HEAD_DIM), F32), _pick_tile(t, 512))
        kvc_blocks = _compress(kvc.reshape(t // BLK, BLK * C_KV), w_big, cmp_bias, w2_big,
                               _pick_tile(t // BLK, 256), 2048)
        yb = _nsa_prompt(q, gates, kvc_blocks, kvs16, kvw16, 128, 512)
        hp, conv_new = _merge(hp, y.reshape(t, C_A), bonus, g_, yb, mg, p_prompt[i].reshape(t, -1),
                              jnp.zeros((SUBLANES, LANES), F32), mconsts, _pick_tile(t, 256), True, final)
        kv6 = lambda z, n_: z.reshape(n_, -1, KV_HEADS, 2, HEAD_DIM)
        outs[0].append(kv6(kvc, 1))
        outs[2].append(kv6(kvs, 1))
        outs[4].append(kv6(kvw[t - min(WINDOW, t):], 1))
        outs[6].append(s_new)
        outs[8].append(pa[t - 1:t])
        outs[10].append(conv_new[SUBLANES - (CONV_W - 1):].reshape(1, CONV_W - 1, f2))

        pa, q, kvc, kvs, kvs16, kvw, kvw16, gates, mg = _proj(hs, g1, w_perm, _pick_tile(db, 128))
        seqs = _rwkv_prep(pa, state_shift[i], rw, bd, _pick_tile(db, 128), False)
        r_, w_, lw_, k_, v_, a_, b_, g_, bonus = seqs
        y, s_new = _rwkv_scan([z.reshape(db, 1, C_A) for z in (r_, w_, k_, v_, a_, b_)], state_wkv[i], 1)
        to_tok_minor = lambda z: jnp.transpose(z, (0, 2, 3, 4, 1))
        pool = _compress_pool(to_tok_minor(cache_cmp_kv[i]).reshape(n_pool * PAGE_ROWS, page), w_tok, bias_pair,
                              w2_pair, _pick_tile(n_pool, 64))
        q3 = q.reshape(db, 1, C_B)
        oc, idx = _samp_cmp(page_table, q3, pool.reshape(n_pool, KV_HEADS * C_KV), past)
        idx2 = idx[:, :, 0, :N_SEL].reshape(db, KV_HEADS * N_SEL)
        cols = lambda z: z.reshape(db, 2 * KV_HEADS, HEAD_DIM, 1)
        yb, win_new_t = _samp_sel(page_table, idx2, q3, gates.reshape(db, 1, GATE_PAD), oc, cols(kvs),
                                  kvw.reshape(db, 1, C_KV), cols(kvw), to_tok_minor(state_win_kv[i]),
                                  to_tok_minor(cache_sel_kv[i]), past)
        win_new = jnp.transpose(win_new_t, (0, 4, 1, 2, 3))
        hs, conv_new = _merge(hs, y.reshape(db, C_A), bonus, g_, yb.reshape(db, C_B), mg, p_sample[i].reshape(db, -1),
                              state_ffn_conv[i].reshape(db, 2 * f2), mconsts, _pick_tile(db, 128), False, final)
        outs[1].append(kv6(kvc, db))
        outs[3].append(kv6(kvs, db))
        outs[5].append(win_new.reshape(db, wb, KV_HEADS, 2, HEAD_DIM))
        outs[7].append(s_new)
        outs[9].append(pa)
        outs[11].append(conv_new.reshape(db, CONV_W - 1, f2))

    stacked = [jnp.stack(o) for o in outs]
    return (hp.reshape(b, t, d_model), hs.reshape(db, dt, d_model), *stacked)
```

```python
import functools

import jax
import jax.numpy as jnp
from jax import lax
from jax.experimental import pallas as pl
from jax.experimental.pallas import tpu as pltpu

F32 = jnp.float32
BF16 = jnp.bfloat16
HI = lax.Precision.HIGHEST

A_HEADS = 8
A_HEAD_DIM = 64
C_A = A_HEADS * A_HEAD_DIM
R_W = 64
R_A = 64
R_G = 128
C_SHIFT = 3 * C_A + R_W + R_A + R_G
LNX_EPS = 64e-5
Q_HEADS = 8
KV_HEADS = 2
GROUP = Q_HEADS // KV_HEADS
HEAD_DIM = 64
C_B = Q_HEADS * HEAD_DIM
C_KV = KV_HEADS * 2 * HEAD_DIM
BLK = 64
N_SEL = 16
WINDOW = 512
D_CMP = 128
FORCE_SCORE = 1e4
CONV_W = 3
NORM_EPS = 1e-6

LANES = 128
SUBLANES = 8
VMEM_BYTES_V7X = 64 * 1024 * 1024

NEG = -1e30
M_FLOOR = -1e29
KV_LANES = 2 * HEAD_DIM

GATE_PAD = LANES


def _vmem_limit(nbytes):
    return int(min(max(nbytes, 16 * 1024 * 1024), VMEM_BYTES_V7X - 8 * 1024 * 1024))


def _const_spec(shape):
    nd = len(shape)
    return pl.BlockSpec(shape, lambda *_: (0,) * nd, pipeline_mode=pl.Buffered(1))


def _rms(x, g):
    return x * lax.rsqrt(jnp.mean(x * x, axis=-1, keepdims=True) + NORM_EPS) * g


def _dot(a, b, **kw):
    return jnp.dot(a, b, preferred_element_type=F32, **kw)


def _dot_nt(a, b):
    return lax.dot_general(a, b, (((1,), (1,)), ((), ())), preferred_element_type=F32)


def _masked_softmax(s, mask):
    s = jnp.where(mask, s, NEG)
    m = jnp.max(s, axis=-1, keepdims=True)
    m = jnp.where(m > M_FLOOR, m, 0.0)
    e = jnp.exp(s - m)
    return e / jnp.maximum(jnp.sum(e, axis=-1, keepdims=True), 1e-30)


def _slope_col(hk, rows_per_head, pad_rows=0):
    cols = [jnp.full((rows_per_head, 1), 2.0 ** (-(hk * GROUP + g + 1)), F32) for g in range(GROUP)]
    if pad_rows:
        cols.append(jnp.zeros((pad_rows, 1), F32))
    return jnp.concatenate(cols, axis=0)


def _proj_body(x_ref, g_ref, w_ref, *o_refs, segs):
    xb = _rms(x_ref[...], g_ref[...]).astype(BF16)
    it = iter(o_refs)
    for off, width, outs in segs:
        r = _dot(xb, w_ref[:, off:off + width])
        for scale in outs:
            o_ref = next(it)
            o_ref[...] = (r * scale if scale != 1.0 else r).astype(o_ref.dtype)


def _proj(x, g, w_perm, tm):
    rows, d = x.shape
    d_model = d
    o = 0
    segs, shapes = [], []
    for width, outs in ((C_SHIFT, ((F32, 1.0),)), (C_B, ((BF16, HEAD_DIM ** -0.5),)), (C_KV, ((F32, 1.0),)),
                        (C_KV, ((F32, 1.0), (BF16, 1.0))), (C_KV, ((F32, 1.0), (BF16, 1.0))),
                        (GATE_PAD, ((F32, 1.0),)), (2 * d_model, ((F32, 1.0),))):
        segs.append((o, width, tuple(s for _, s in outs)))
        shapes += [(width, dt) for dt, _ in outs]
        o += width
    n_tot = o
    out_bytes = sum(tm * w * jnp.dtype(dt).itemsize for w, dt in shapes)
    vmem = 2 * tm * d * 4 + d * n_tot * 2 + 2 * out_bytes + (8 << 20)
    return pl.pallas_call(
        functools.partial(_proj_body, segs=tuple(segs)),
        grid=(rows // tm,),
        in_specs=[pl.BlockSpec((tm, d), lambda i: (i, 0)), _const_spec((1, d)), _const_spec((d, n_tot))],
        out_specs=[pl.BlockSpec((tm, w), lambda i: (i, 0)) for w, _ in shapes],
        out_shape=[jax.ShapeDtypeStruct((rows, w), dt) for w, dt in shapes],
        compiler_params=pltpu.CompilerParams(dimension_semantics=("arbitrary",), vmem_limit_bytes=_vmem_limit(vmem)),
        name="proj",
    )(x, g, w_perm)


def _prep_body(pa_ref, prev_ref, mu_ref, w0_ref, w2_ref, a0_ref, a2_ref, g2_ref, kk_ref, ka_ref, rk_ref, bd_ref,
               r_o, w_o, lw_o, k_o, v_o, a_o, b_o, g_o, bonus_o, *, seq_mode):
    pf = pa_ref[...]
    if seq_mode:
        i = pl.program_id(0)
        first = jnp.where(i > 0, prev_ref[SUBLANES - 1:SUBLANES, :], 0.0)
        rid = lax.broadcasted_iota(jnp.int32, pf.shape, 0)
        prev = jnp.where(rid == 0, first, pltpu.roll(pf, 1, 0))
    else:
        prev = prev_ref[...]
    xs = pf + (prev - pf) * mu_ref[...]
    r = xs[:, 0:C_A]
    k = xs[:, C_A:2 * C_A]
    v = xs[:, 2 * C_A:3 * C_A]
    o = 3 * C_A
    wd = xs[:, o:o + R_W]
    ad = xs[:, o + R_W:o + R_W + R_A]
    gd = xs[:, o + R_W + R_A:]
    nz = -(w0_ref[...] + _dot(jnp.tanh(wd), w2_ref[...], precision=HI))
    softplus = jnp.maximum(nz, 0.0) + jnp.log(1.0 + jnp.exp(-jnp.abs(nz)))
    w = -softplus - 0.5
    a = jax.nn.sigmoid(a0_ref[...] + _dot(ad, a2_ref[...], precision=HI))
    g = _dot(jax.nn.sigmoid(gd), g2_ref[...], precision=HI)
    kk = k * kk_ref[...]
    ss = _dot(kk * kk, bd_ref[...], precision=HI)
    kkn = kk / jnp.maximum(jnp.sqrt(ss), 1e-12)
    k2 = k * (1.0 + (a - 1.0) * ka_ref[...])
    r_o[...] = r
    lw = -jnp.exp(w)
    lw_o[...] = lw
    w_o[...] = jnp.exp(lw)
    k_o[...] = k2
    v_o[...] = v
    a_o[...] = -kkn
    b_o[...] = kkn * a
    g_o[...] = g
    bonus_o[...] = _dot(r * k2 * rk_ref[...], bd_ref[...], precision=HI) * v


def _rwkv_prep(pa, prev, rw, bd, tm, seq_mode):
    rows = pa.shape[0]
    mu, w0, w2, a0, a2, g2, k_k, k_a, r_k = rw
    if seq_mode:
        tb = tm // SUBLANES
        prev_spec = pl.BlockSpec((SUBLANES, C_SHIFT), lambda i: (jnp.maximum(i * tb - 1, 0), 0))
        prev = pa
    else:
        prev_spec = pl.BlockSpec((tm, C_SHIFT), lambda i: (i, 0))
    row = lambda z: z.reshape(1, -1)
    consts = [row(mu), row(w0), w2, row(a0), a2, g2, row(k_k), row(k_a), row(r_k), bd]
    vmem = 4 * tm * C_SHIFT * 4 + 2 * 8 * tm * C_A * 4 + 16 * tm * C_A * 4 + (8 << 20)
    return pl.pallas_call(
        functools.partial(_prep_body, seq_mode=seq_mode),
        grid=(rows // tm,),
        in_specs=[pl.BlockSpec((tm, C_SHIFT), lambda i: (i, 0)), prev_spec] + [_const_spec(c.shape) for c in consts],
        out_specs=[pl.BlockSpec((tm, C_A), lambda i: (i, 0))] * 9,
        out_shape=[jax.ShapeDtypeStruct((rows, C_A), F32)] * 9,
        compiler_params=pltpu.CompilerParams(dimension_semantics=("arbitrary",), vmem_limit_bytes=_vmem_limit(vmem)),
        name="rwkv_prep",
    )(pa, prev, *consts)


N_PAIR = A_HEADS // 2


def _scan_body(r_ref, w_ref, k_ref, v_ref, a_ref, b_ref, s0_ref, y_ref, so_ref, st_ref, *, tc, nc):
    c = pl.program_id(1)

    @pl.when(c == 0)
    def _():
        for p in range(N_PAIR):
            st_ref[p] = jnp.concatenate([s0_ref[0, 2 * p], s0_ref[0, 2 * p + 1]], axis=-1)

    shape = (A_HEAD_DIM, LANES)
    lane = lax.broadcasted_iota(jnp.int32, shape, 1)
    sub = lax.broadcasted_iota(jnp.int32, shape, 0)
    lo = lane < A_HEAD_DIM
    diag = (lane & (A_HEAD_DIM - 1)) == sub

    def seg_sum(x):
        s_lo = jnp.sum(jnp.where(lo, x, 0.0), axis=-1, keepdims=True)
        s_hi = jnp.sum(jnp.where(lo, 0.0, x), axis=-1, keepdims=True)
        return jnp.where(lo, s_lo, s_hi)

    grp = min(SUBLANES, tc)

    def token_group(gi, carry):
        base = pl.multiple_of(gi * grp, grp)
        for p in range(N_PAIR):
            sl = slice(LANES * p, LANES * (p + 1))
            rt, wt, kt, vt, at, bt = (ref[0, pl.ds(base, grp), sl] for ref in (r_ref, w_ref, k_ref, v_ref, a_ref, b_ref))
            s = st_ref[p]
            ys = []
            for j in range(grp):
                row = lambda z: z[j:j + 1, :]
                sa = seg_sum(s * row(at))
                v_col = seg_sum(jnp.where(diag, row(vt), 0.0))
                s = s * row(wt) + sa * row(bt) + v_col * row(kt)
                y_col = seg_sum(s * row(rt))
                ys.append(jnp.sum(jnp.where(diag, y_col, 0.0), axis=0, keepdims=True))
            st_ref[p] = s
            y_ref[0, pl.ds(base, grp), sl] = jnp.concatenate(ys, axis=0) if grp > 1 else ys[0]
        return carry

    lax.fori_loop(0, tc // grp, token_group, 0)

    @pl.when(c == nc - 1)
    def _():
        for p in range(N_PAIR):
            s = st_ref[p]
            so_ref[0, 2 * p] = s[:, :A_HEAD_DIM]
            so_ref[0, 2 * p + 1] = s[:, A_HEAD_DIM:]


def _rwkv_scan(seqs, s0, tc):
    b, t, _ = seqs[0].shape
    nc = t // tc
    seq_spec = pl.BlockSpec((1, tc, C_A), lambda i, c: (i, c, 0))
    st_spec = pl.BlockSpec((1, A_HEADS, A_HEAD_DIM, A_HEAD_DIM), lambda i, c: (i, 0, 0, 0))
    return pl.pallas_call(
        functools.partial(_scan_body, tc=tc, nc=nc),
        grid=(b, nc),
        in_specs=[seq_spec] * 6 + [st_spec],
        out_specs=[seq_spec, st_spec],
        out_shape=[jax.ShapeDtypeStruct((b, t, C_A), F32), jax.ShapeDtypeStruct(s0.shape, F32)],
        scratch_shapes=[pltpu.VMEM((N_PAIR, A_HEAD_DIM, LANES), F32)],
        compiler_params=pltpu.CompilerParams(dimension_semantics=("arbitrary", "arbitrary")),
        name="rwkv_scan",
    )(*seqs, s0)


SCAN_CHUNK = 64


def _mm(a, b):
    return _dot(a.astype(BF16), b.astype(BF16))


def _chunk_scan_body(r_ref, lw_ref, k_ref, v_ref, a_ref, b_ref, s0_ref, y_ref, so_ref, st_ref, *, tc, nc):
    cidx = pl.program_id(1)
    c = SCAN_CHUNK
    d = A_HEAD_DIM

    @pl.when(cidx == 0)
    def _():
        for h in range(A_HEADS):
            st_ref[h] = s0_ref[0, h].T

    ri = lax.broadcasted_iota(jnp.int32, (c, c), 0)
    ci = lax.broadcasted_iota(jnp.int32, (c, c), 1)
    lower = ci <= ri
    strict = ci < ri
    tri = lower.astype(F32)
    eye = lax.broadcasted_iota(jnp.int32, (d, d), 0) == lax.broadcasted_iota(jnp.int32, (d, d), 1)
    n_double = (c - 1).bit_length()

    def chunk(step, carry):
        base = pl.multiple_of(step * c, c)
        r, lw, k, v, a, b = (ref[0, pl.ds(base, c), :] for ref in (r_ref, lw_ref, k_ref, v_ref, a_ref, b_ref))
        cum = _dot(tri, lw, precision=HI)
        tot = cum[c - 1:c, :]
        e_inv = jnp.exp(-cum)
        e_rest = jnp.exp(tot - cum)
        at = a * jnp.exp(cum - lw)
        rt = r * jnp.exp(cum)
        bt = b * e_inv
        kt = k * e_inv
        bh = b * e_rest
        kh = k * e_rest
        g_tot = jnp.exp(tot)
        heads = range(A_HEADS)
        hsl = [slice(h * d, (h + 1) * d) for h in heads]
        g4 = [_dot_nt(jnp.concatenate([at[:, s], rt[:, s]], axis=0).astype(BF16),
                      jnp.concatenate([bt[:, s], kt[:, s]], axis=0).astype(BF16)) for s in hsl]
        lp = [jnp.where(strict, g[:c, :c], 0.0).astype(BF16) for g in g4]
        m_l = [jnp.where(strict, g[:c, c:], 0.0) for g in g4]
        p_b = [jnp.where(lower, g[c:, :c], 0.0) for g in g4]
        p_k = [jnp.where(lower, g[c:, c:], 0.0) for g in g4]
        vb = [v[:, s].astype(BF16) for s in hsl]
        z = [jnp.concatenate([at[:, hsl[h]], _mm(m_l[h], vb[h])], axis=-1) for h in heads]
        for q in range(n_double):
            z = [z[h] + _mm(lp[h], z[h]) for h in heads]
            if q < n_double - 1:
                lp = [_mm(lp[h], lp[h]).astype(BF16) for h in heads]
        zb = [zz.astype(BF16) for zz in z]
        bz = [_mm(bh[:, hsl[h]].T, zb[h]) for h in heads]
        kv = [_mm(kh[:, hsl[h]].T, vb[h]) for h in heads]
        pz = [_mm(p_b[h], zb[h]) for h in heads]
        pv = [_mm(p_k[h], vb[h]) for h in heads]
        ys = []
        for h in heads:
            st = st_ref[h]
            a_c = jnp.where(eye, g_tot[:, hsl[h]], 0.0) + bz[h][:, :d]
            ys.append(_dot(rt[:, hsl[h]] + pz[h][:, :d], st, precision=HI) + pz[h][:, d:] + pv[h])
            st_ref[h] = _dot(a_c, st, precision=HI) + bz[h][:, d:] + kv[h]
        y_ref[0, pl.ds(base, c), :] = jnp.concatenate(ys, axis=-1)
        return carry

    lax.fori_loop(0, tc // c, chunk, 0)

    @pl.when(cidx == nc - 1)
    def _():
        for h in range(A_HEADS):
            so_ref[0, h] = st_ref[h].T


def _rwkv_chunk_scan(seqs, s0, tc):
    b, t, _ = seqs[0].shape
    nc = t // tc
    seq_spec = pl.BlockSpec((1, tc, C_A), lambda i, c: (i, c, 0))
    st_spec = pl.BlockSpec((1, A_HEADS, A_HEAD_DIM, A_HEAD_DIM), lambda i, c: (i, 0, 0, 0))
    return pl.pallas_call(
        functools.partial(_chunk_scan_body, tc=tc, nc=nc),
        grid=(b, nc),
        in_specs=[seq_spec] * 6 + [st_spec],
        out_specs=[seq_spec, st_spec],
        out_shape=[jax.ShapeDtypeStruct((b, t, C_A), F32), jax.ShapeDtypeStruct(s0.shape, F32)],
        scratch_shapes=[pltpu.VMEM((A_HEADS, A_HEAD_DIM, A_HEAD_DIM), F32)],
        compiler_params=pltpu.CompilerParams(dimension_semantics=("arbitrary", "arbitrary")),
        name="rwkv_chunk_scan",
    )(*seqs, s0)


def _gelu_tanh(x):
    return 0.5 * x * (1.0 + jnp.tanh(0.7978845608028654 * (x + 0.044715 * (x * x * x))))


def _compress_body(x_ref, w_ref, b_ref, w2_ref, o_ref, acc_ref, *, nk, mlp):
    kk = pl.program_id(1)

    @pl.when(kk == 0)
    def _():
        acc_ref[...] = jnp.zeros_like(acc_ref)

    acc_ref[...] += _dot(x_ref[...].astype(BF16), w_ref[...])

    @pl.when(kk == nk - 1)
    def _():
        h = acc_ref[...] + b_ref[...]
        if mlp:
            o_ref[...] = _dot(_gelu_tanh(h).astype(BF16), w2_ref[...])
        else:
            o_ref[...] = h


def _compress(x2, w_big, bias, w2_big, tm, tk, mlp=True):
    m, kdim = x2.shape
    nh = w_big.shape[1]
    n_out = w2_big.shape[1] if mlp else nh
    nk = kdim // tk
    vmem = 2 * tm * tk * 4 + 2 * tk * nh * 2 + 3 * tm * nh * 4 + 2 * tm * n_out * 4 + (8 << 20)
    return pl.pallas_call(
        functools.partial(_compress_body, nk=nk, mlp=mlp),
        grid=(m // tm, nk),
        in_specs=[pl.BlockSpec((tm, tk), lambda i, k: (i, k)), pl.BlockSpec((tk, nh), lambda i, k: (k, 0)),
                  _const_spec(bias.shape), _const_spec(w2_big.shape)],
        out_specs=pl.BlockSpec((tm, n_out), lambda i, k: (i, 0)),
        out_shape=jax.ShapeDtypeStruct((m, n_out), F32),
        scratch_shapes=[pltpu.VMEM((tm, nh), F32)],
        compiler_params=pltpu.CompilerParams(dimension_semantics=("arbitrary", "arbitrary"),
                                             vmem_limit_bytes=_vmem_limit(vmem)),
        name="nsa_compress",
    )(x2, w_big, bias, w2_big)


PAGE_ROWS = KV_HEADS * 2 * HEAD_DIM


def _compress_pool_body(x_ref, w_ref, b_ref, w2_ref, o_ref, *, pg):
    m = pg * KV_HEADS
    xt = jnp.swapaxes(x_ref[...], 0, 1)
    for c in range(2):
        acc = jnp.zeros((m, 2 * D_CMP), F32)
        for d in range(HEAD_DIM):
            r0 = c * HEAD_DIM + d
            acc = acc + _dot(xt[r0].astype(BF16), w_ref[r0 * 2 * BLK:(r0 + 1) * 2 * BLK, :])
        h = acc + b_ref[c:c + 1, :]
        o_ref[:, c * 2 * HEAD_DIM:(c + 1) * 2 * HEAD_DIM] = _dot(_gelu_tanh(h).astype(BF16), w2_ref[c])


def _compress_pool(x2, w_tok, bias2, w2_pair, pg):
    n_pool = x2.shape[0] // KV_HEADS
    m = pg * KV_HEADS
    vmem = 4 * pg * PAGE_ROWS * 2 * BLK * 4 + int(w_tok.size) * 2 + 8 * m * 2 * D_CMP * 4 + (8 << 20)
    return pl.pallas_call(
        functools.partial(_compress_pool_body, pg=pg),
        grid=(n_pool // pg,),
        in_specs=[pl.BlockSpec((m, 2 * HEAD_DIM, 2 * BLK), lambda i: (i, 0, 0)), _const_spec(w_tok.shape),
                  _const_spec(bias2.shape), _const_spec(w2_pair.shape)],
        out_specs=pl.BlockSpec((m, C_KV), lambda i: (i, 0)),
        out_shape=jax.ShapeDtypeStruct((n_pool * KV_HEADS, C_KV), F32),
        compiler_params=pltpu.CompilerParams(dimension_semantics=("arbitrary",), vmem_limit_bytes=_vmem_limit(vmem)),
        name="nsa_compress_pool",
    )(x2, w_tok, bias2, w2_pair)


def _topk_mask(s, ids, k, axis):
    sel = jnp.zeros(s.shape, F32)
    for _ in range(k):
        m = jnp.max(s, axis=axis, keepdims=True)
        pick = jnp.min(jnp.where(s == m, ids, 1e9), axis=axis, keepdims=True)
        hit = ids == pick
        sel = jnp.where(hit, 1.0, sel)
        s = jnp.where(hit, -jnp.inf, s)
    return sel


def _head_rows(q_ref, hk, pad_rows=0, pad_lanes=True):
    parts = [q_ref[:, (hk * GROUP + g) * HEAD_DIM:(hk * GROUP + g + 1) * HEAD_DIM] for g in range(GROUP)]
    if pad_rows:
        parts.append(jnp.zeros((pad_rows, HEAD_DIM), parts[0].dtype))
    qh = jnp.concatenate(parts, axis=0)
    return jnp.concatenate([qh, jnp.zeros_like(qh)], axis=-1) if pad_lanes else qh


AUG_HI = HEAD_DIM
AUG_LO = HEAD_DIM + 1
AUG_ONE = HEAD_DIM + 2
AUG_BLK = HEAD_DIM + 3
POS_SPLIT = LANES


def _nsa_select_body(q_ref, kc_ref, kct_ref, oc_ref, ns_ref, flag_ref, *, tq, nbc):
    i = pl.program_id(0)
    t0 = i * tq
    qrow1 = t0 + lax.broadcasted_iota(jnp.int32, (1, tq), 1)
    qrow = jnp.concatenate([qrow1] * GROUP, axis=1)
    cur_row = qrow1 // BLK
    blkc = lax.broadcasted_iota(jnp.int32, (nbc, 1), 0)
    blkcf = blkc.astype(F32)
    pieces, masks, flags = [], [], []
    for hk in range(KV_HEADS):
        hl = slice(hk * KV_LANES, (hk + 1) * KV_LANES)
        qh = _head_rows(q_ref, hk)
        slope_row = jnp.concatenate([jnp.full((1, tq), 2.0 ** (-(hk * GROUP + g + 1)), F32) for g in range(GROUP)],
                                    axis=1)
        kcb = kc_ref[:, hl].astype(BF16)
        cposc = blkc * BLK + (BLK - 1)
        s_t = _dot_nt(kcb, qh) - slope_row * (qrow - cposc).astype(F32)
        s_t = jnp.where(cposc <= qrow, s_t, NEG)
        m_c = jnp.max(s_t, axis=0, keepdims=True)
        m_c = jnp.where(m_c > M_FLOOR, m_c, 0.0)
        e_c = jnp.exp(s_t - m_c)
        p_t = e_c / jnp.maximum(jnp.sum(e_c, axis=0, keepdims=True), 1e-30)
        o_c = _dot(kct_ref[hl, :].astype(BF16), p_t.astype(BF16))[HEAD_DIM:, :].T
        score = p_t[:, 0:tq]
        for g in range(1, GROUP):
            score = score + p_t[:, g * tq:(g + 1) * tq]
        forced = (blkc == 0) | (blkc == cur_row) | (blkc == cur_row - 1)
        sc = jnp.where(forced, FORCE_SCORE, score)
        sc = jnp.where(blkc > cur_row, -1.0, sc)
        sel_t = _topk_mask(sc, blkcf, min(N_SEL, nbc), axis=0)
        sel_q = jnp.where(blkc <= cur_row, sel_t, 0.0).T
        masks.append(1.0 - sel_q)
        flags.append(jnp.max(sel_q, axis=0, keepdims=True))
        pieces += [o_c[g * tq:(g + 1) * tq] for g in range(GROUP)]
    oc_ref[...] = jnp.concatenate(pieces, axis=-1)
    ns_ref[...] = jnp.concatenate(masks, axis=-1).astype(BF16)
    flag_ref[0] = jnp.concatenate(flags, axis=0)


def _nsa_select(q, kvc, tq):
    t = q.shape[0]
    nbc = kvc.shape[0]
    kct = kvc.T
    return pl.pallas_call(
        functools.partial(_nsa_select_body, tq=tq, nbc=nbc),
        grid=(t // tq,),
        in_specs=[pl.BlockSpec((tq, C_B), lambda i: (i, 0)), _const_spec(kvc.shape), _const_spec(kct.shape)],
        out_specs=[pl.BlockSpec((tq, C_B), lambda i: (i, 0)), pl.BlockSpec((tq, KV_HEADS * nbc), lambda i: (i, 0)),
                   pl.BlockSpec((1, KV_HEADS, nbc), lambda i: (i, 0, 0))],
        out_shape=[jax.ShapeDtypeStruct((t, C_B), F32), jax.ShapeDtypeStruct((t, KV_HEADS * nbc), BF16),
                   jax.ShapeDtypeStruct((t // tq, KV_HEADS, nbc), F32)],
        compiler_params=pltpu.CompilerParams(dimension_semantics=("arbitrary",)),
        name="nsa_select",
    )(q, kvc, kct)


def _nsa_prompt_body(cnt_ref, lst_ref, q_ref, gt_ref, oc_ref, ns_ref, ka_ref, vat_ref, kw_ref, em_ref, o_ref,
                     *, tq, tk, nbc, max_tiles):
    i = pl.program_id(0)
    t0 = i * tq
    rows = GROUP * tq
    qpos1 = t0 + lax.broadcasted_iota(jnp.int32, (tq, 1), 0)
    qpos = jnp.concatenate([qpos1] * GROUP, axis=0)
    qposf = qpos.astype(F32)
    qrow = jnp.concatenate([t0 + lax.broadcasted_iota(jnp.int32, (1, tq), 1)] * GROUP, axis=1)
    blk_row = lax.broadcasted_iota(jnp.int32, (1, nbc), 1)
    lane = lax.broadcasted_iota(jnp.int32, (1, KV_LANES), 1)
    gt = jax.nn.sigmoid(gt_ref[...])
    bpt = tk // BLK
    t0f = t0.astype(F32)
    o_ws, not_sels, q_augs = [], [], []
    for hk in range(KV_HEADS):
        hl = slice(hk * KV_LANES, (hk + 1) * KV_LANES)
        qh = _head_rows(q_ref, hk)
        slope = _slope_col(hk, tq)
        not_sels.append(ns_ref[:, hk * nbc:(hk + 1) * nbc])

        wl = WINDOW + tq
        ws = pl.multiple_of(jnp.maximum(t0 - WINDOW, 0), tq)
        kvw = kw_ref[pl.ds(ws, wl), hl]
        kposw = ws + lax.broadcasted_iota(jnp.int32, (1, wl), 1)
        s = _dot_nt(qh, kvw) - slope * (qposf - kposw.astype(F32))
        p_w = _masked_softmax(s, (kposw <= qpos) & (qpos - kposw < WINDOW))
        o_ws.append(_dot(p_w.astype(BF16), kvw)[:, HEAD_DIM:])

        q_augs.append(qh.astype(F32) + jnp.where(lane == AUG_HI, slope * POS_SPLIT, 0.0)
                      + jnp.where(lane == AUG_LO, slope, 0.0) + jnp.where(lane == AUG_ONE, -slope * t0f, 0.0))

    def sel_step(tiles, active, carry, causal):
        out = []
        for hk in range(KV_HEADS):
            m, acc = carry[hk]
            j = tiles[hk]
            k0 = pl.multiple_of(j * tk, tk)
            ns = not_sels[hk] if active[hk] is None else jnp.where(active[hk], not_sels[hk], jnp.ones_like(not_sels[hk]))
            mk = _dot(jnp.where(blk_row // bpt == j, ns, jnp.zeros_like(ns)), em_ref[...])
            q_j = (q_augs[hk] + jnp.concatenate([mk] * GROUP, axis=0)).astype(BF16)
            s = _dot_nt(ka_ref[hk, pl.ds(k0, tk), :], q_j)
            if causal:
                kpos = k0 + lax.broadcasted_iota(jnp.int32, (tk, 1), 0)
                s = jnp.where(kpos <= qrow, s, NEG)
            m_new = jnp.maximum(m, jnp.max(s, axis=0, keepdims=True))
            p = jnp.exp(s - m_new).astype(BF16)
            out.append((m_new, jnp.exp(m - m_new) * acc + _dot(vat_ref[hk, j], p)))
        return tuple(out)

    cnts = [cnt_ref[i * KV_HEADS + hk] for hk in range(KV_HEADS)]

    def skip_step(n, carry):
        active = [n < cnts[hk] - 1 for hk in range(KV_HEADS)]
        tiles = [jnp.where(active[hk], lst_ref[(i * KV_HEADS + hk) * max_tiles + n], 0) for hk in range(KV_HEADS)]
        return sel_step(tiles, active, carry, False)

    init = tuple((jnp.full((1, rows), M_FLOOR, F32), jnp.zeros((KV_LANES, rows), F32)) for _ in range(KV_HEADS))
    n_steps = functools.reduce(jnp.maximum, cnts) - 1
    carry = lax.fori_loop(0, n_steps, skip_step, init)
    last = (t0 + tq + tk - 1) // tk - 1
    carry = sel_step([last] * KV_HEADS, [None] * KV_HEADS, carry, True)

    pieces = []
    for hk in range(KV_HEADS):
        acc = carry[hk][1]
        o_s = (acc[:HEAD_DIM] / jnp.maximum(acc[HEAD_DIM:HEAD_DIM + 1], 1e-30)).T
        for g in range(GROUP):
            c0 = (hk * GROUP + g) * 3
            cs = slice((hk * GROUP + g) * HEAD_DIM, (hk * GROUP + g + 1) * HEAD_DIM)
            rs = slice(g * tq, (g + 1) * tq)
            pieces.append(gt[:, c0:c0 + 1] * oc_ref[:, cs] + gt[:, c0 + 1:c0 + 2] * o_s[rs]
                          + gt[:, c0 + 2:c0 + 3] * o_ws[hk][rs])
    o_ref[...] = jnp.concatenate(pieces, axis=-1)


def _nsa_prompt(q, gates, kvc, ks, kw, tq, tk):
    t = q.shape[0]
    nbc = kvc.shape[0]
    bpt = tk // BLK
    max_tiles = nbc // bpt
    assert t <= POS_SPLIT * 256 and AUG_BLK + bpt <= KV_LANES and t % tk == 0
    oc, not_sel, blk_any = _nsa_select(q, kvc, tq)
    tile_any = jnp.max(blk_any.reshape(t // tq, KV_HEADS, max_tiles, bpt), axis=-1) > 0.0
    lst = jnp.argsort(jnp.logical_not(tile_any), axis=-1, stable=True).astype(jnp.int32).reshape(-1)
    cnt = jnp.sum(tile_any, axis=-1).astype(jnp.int32).reshape(-1)
    em = jnp.where(jnp.arange(KV_LANES)[None, :] == AUG_BLK + jnp.arange(nbc)[:, None] % bpt, NEG, 0.0).astype(BF16)
    pos = jnp.arange(t, dtype=jnp.int32)[:, None]
    aug_lane = jnp.arange(HEAD_DIM, dtype=jnp.int32)[None, :] + HEAD_DIM
    k_aug = (jnp.where(aug_lane == AUG_HI, pos // POS_SPLIT, 0) + jnp.where(aug_lane == AUG_LO, pos % POS_SPLIT, 0)
             + jnp.where(aug_lane == AUG_ONE, 1, 0)
             + jnp.where(aug_lane == AUG_BLK + (pos // BLK) % bpt, 1, 0)).astype(BF16)
    v_aug = jnp.broadcast_to(jnp.where(aug_lane == HEAD_DIM, 1, 0).astype(BF16), (t, HEAD_DIM))
    ka = jnp.stack([jnp.concatenate([ks[:, h * KV_LANES:h * KV_LANES + HEAD_DIM], k_aug], axis=1)
                    for h in range(KV_HEADS)])
    va = jnp.stack([jnp.concatenate([ks[:, h * KV_LANES + HEAD_DIM:(h + 1) * KV_LANES], v_aug], axis=1)
                    for h in range(KV_HEADS)])
    va = jnp.transpose(va.reshape(KV_HEADS, t // tk, tk, KV_LANES), (0, 1, 3, 2))
    vmem = 3 * t * C_KV * 2 + 24 * GROUP * tq * max(tk, WINDOW + tq) * 4 + (8 << 20)
    tile = lambda w: pl.BlockSpec((tq, w), lambda i, c, l: (i, 0))
    const = lambda z: pl.BlockSpec(z.shape, lambda i, c, l: (0,) * z.ndim, pipeline_mode=pl.Buffered(1))
    grid_spec = pltpu.PrefetchScalarGridSpec(
        num_scalar_prefetch=2,
        grid=(t // tq,),
        in_specs=[tile(C_B), tile(GATE_PAD), tile(C_B), tile(KV_HEADS * nbc), const(ka), const(va), const(kw), const(em)],
        out_specs=tile(C_B),
    )
    return pl.pallas_call(
        functools.partial(_nsa_prompt_body, tq=tq, tk=tk, nbc=nbc, max_tiles=max_tiles),
        grid_spec=grid_spec,
        out_shape=jax.ShapeDtypeStruct((t, C_B), F32),
        compiler_params=pltpu.CompilerParams(dimension_semantics=("arbitrary",), vmem_limit_bytes=_vmem_limit(vmem)),
        name="nsa_prompt",
    )(cnt, lst, q, gates, oc, not_sel, ka, va, kw, em)


PAGE_BLOCKS = 2
Q_ROWS = SUBLANES


def _samp_cmp_body(pt_ref, q_ref, pool_ref, oc_ref, sc_ref, kbuf, sems, *, n_pages, past, db):
    n = pl.program_id(0)

    slot = n % 2

    def page_copy(sample, buf, jp):
        return pltpu.make_async_copy(pool_ref.at[pl.ds(pt_ref[sample, jp], 1), :], kbuf.at[buf, pl.ds(jp, 1), :],
                                     sems.at[buf])

    def start_all(sample, buf):
        def start(jp, c):
            page_copy(sample, buf, jp).start()
            return c
        lax.fori_loop(0, n_pages, start, 0)

    def wait(jp, c):
        page_copy(n, slot, jp).wait()
        return c

    @pl.when(n == 0)
    def _():
        start_all(0, 0)

    @pl.when(n + 1 < db)
    def _():
        start_all(n + 1, 1 - slot)

    lax.fori_loop(0, n_pages, wait, 0)

    nb_past = n_pages * PAGE_BLOCKS
    n_all = nb_past + 1
    qpos = past
    cur = qpos // BLK
    width = PAGE_BLOCKS * n_pages
    lane = lax.broadcasted_iota(jnp.int32, (1, width), 1)
    bid = jnp.where(lane < n_pages, PAGE_BLOCKS * lane, PAGE_BLOCKS * (lane - n_pages) + 1)
    cpos = bid * BLK + (BLK - 1)
    bid2 = jnp.concatenate([bid, nb_past + lane], axis=-1)
    scores = []
    for hk in range(KV_HEADS):
        q8 = _head_rows(q_ref.at[0], hk, pad_rows=Q_ROWS - GROUP, pad_lanes=False)
        z8 = jnp.zeros_like(q8)
        slope = _slope_col(hk, 1, pad_rows=Q_ROWS - GROUP)
        kmat = kbuf[slot, :, hk * C_KV:hk * C_KV + KV_LANES].astype(BF16)
        vmat = kbuf[slot, :, hk * C_KV + KV_LANES:(hk + 1) * C_KV].astype(BF16)
        s = jnp.concatenate([_dot_nt(jnp.concatenate([q8, z8], axis=-1), kmat),
                             _dot_nt(jnp.concatenate([z8, q8], axis=-1), kmat)], axis=-1)
        s = s - slope * (qpos - cpos).astype(F32)
        p_c = _masked_softmax(s, cpos <= qpos)
        o_even = _dot(p_c[:, :n_pages].astype(BF16), vmat)
        o_odd = _dot(p_c[:, n_pages:].astype(BF16), vmat)
        oc_ref[0, hk] = o_even[:, :HEAD_DIM] + o_odd[:, HEAD_DIM:]
        score = jnp.sum(p_c[0:GROUP], axis=0, keepdims=True)
        sc = jnp.concatenate([score, jnp.zeros((1, width), F32)], axis=-1)
        forced = (bid2 == 0) | (bid2 == cur) | (bid2 == cur - 1)
        sc = jnp.where(forced, FORCE_SCORE, sc)
        sc = jnp.where(bid2 > cur, -1.0, sc)
        scores.append(jnp.where(bid2 < n_all, sc, -jnp.inf))
    sc_ref[0] = jnp.concatenate(scores, axis=0)


def _samp_cmp(page_table, q3, pool2, past):
    db, n_pages = page_table.shape
    wide = 2 * PAGE_BLOCKS * n_pages
    grid_spec = pltpu.PrefetchScalarGridSpec(
        num_scalar_prefetch=1,
        grid=(db,),
        in_specs=[pl.BlockSpec((1, 1, C_B), lambda n, pt: (n, 0, 0)), pl.BlockSpec(memory_space=pl.ANY)],
        out_specs=[pl.BlockSpec((1, KV_HEADS, Q_ROWS, HEAD_DIM), lambda n, pt: (n, 0, 0, 0)),
                   pl.BlockSpec((1, KV_HEADS, wide), lambda n, pt: (n, 0, 0))],
        scratch_shapes=[pltpu.VMEM((2, n_pages, KV_HEADS * C_KV), F32), pltpu.SemaphoreType.DMA((2,))],
    )
    return pl.pallas_call(
        functools.partial(_samp_cmp_body, n_pages=n_pages, past=past, db=db),
        grid_spec=grid_spec,
        out_shape=[jax.ShapeDtypeStruct((db, KV_HEADS, Q_ROWS, HEAD_DIM), F32),
                   jax.ShapeDtypeStruct((db, KV_HEADS, wide), F32)],
        compiler_params=pltpu.CompilerParams(dimension_semantics=("arbitrary",)),
        name="nsa_sample_cmp",
    )(page_table, q3, pool2)


def _samp_topk_body(sc_ref, idx_ref, *, n_pages, nb_past):
    s_t = sc_ref[...].T
    npos, nrow = s_t.shape
    width = PAGE_BLOCKS * n_pages
    pos = lax.broadcasted_iota(jnp.int32, (npos, 1), 0)
    ids = jnp.where(pos < n_pages, PAGE_BLOCKS * pos,
                    jnp.where(pos < width, PAGE_BLOCKS * (pos - n_pages) + 1, nb_past + pos - width)).astype(F32)
    picks = []
    for _ in range(N_SEL):
        m = jnp.max(s_t, axis=0, keepdims=True)
        pick = jnp.min(jnp.where(s_t == m, ids, 1e9), axis=0, keepdims=True)
        picks.append(pick)
        s_t = jnp.where(ids == pick, -jnp.inf, s_t)
    res = jnp.concatenate(picks + [jnp.zeros((LANES - N_SEL, nrow), F32)], axis=0)
    idx_ref[...] = res.T.astype(jnp.int32)


def _samp_topk(scores2, n_pages):
    rows = scores2.shape[0]
    return pl.pallas_call(
        functools.partial(_samp_topk_body, n_pages=n_pages, nb_past=n_pages * PAGE_BLOCKS),
        out_shape=jax.ShapeDtypeStruct((rows, LANES), jnp.int32),
        name="nsa_sample_topk",
    )(scores2)


def _samp_sel_body(pt_ref, idx_ref, q_ref, gt_ref, oc_ref, ksc_ref, kwr_ref, kwc_ref, win_ref, cache_ref,
                   o_ref, wino_ref, sbuf, sems, *, nb_past, past, wb, db):
    n = pl.program_id(0)
    qpos = past

    page_tok = PAGE_BLOCKS * BLK
    tok = lax.broadcasted_iota(jnp.int32, (HEAD_DIM, page_tok), 1)

    buf = n % 2

    def page_copy(sample, b, hk, s, ib):
        page = pt_ref[sample, ib // PAGE_BLOCKS]
        return pltpu.make_async_copy(cache_ref.at[page, hk], sbuf.at[b, hk, :, :, pl.ds(s * page_tok, page_tok)],
                                     sems.at[b, hk * N_SEL + s])

    def start_pages(sample, b):
        for hk in range(KV_HEADS):
            for s in range(N_SEL):
                ib = idx_ref[sample, hk * N_SEL + s]

                @pl.when(ib < nb_past)
                def _():
                    page_copy(sample, b, hk, s, ib).start()

    @pl.when(n == 0)
    def _():
        start_pages(0, 0)

    @pl.when(n + 1 < db)
    def _():
        start_pages(n + 1, 1 - buf)

    for hk in range(KV_HEADS):
        for s in range(N_SEL):
            ib = idx_ref[n, hk * N_SEL + s]

            @pl.when(ib >= nb_past)
            def _():
                for c in range(2):
                    sbuf[buf, hk, c, :, s * page_tok:(s + 1) * page_tok] = jnp.where(tok == 0, ksc_ref[0, 2 * hk + c],
                                                                                    0.0)

    wtok = lax.broadcasted_iota(jnp.int32, (HEAD_DIM, wb), 1)
    for hk in range(KV_HEADS):
        for c in range(2):
            wino_ref[0, hk, c] = jnp.where(wtok == wb - 1, kwc_ref[0, 2 * hk + c],
                                           pltpu.roll(win_ref[0, hk, c], wb - 1, 1))

    for hk in range(KV_HEADS):
        for s in range(N_SEL):
            ib = idx_ref[n, hk * N_SEL + s]

            @pl.when(ib < nb_past)
            def _():
                page_copy(n, buf, hk, s, ib).wait()

    gt = jax.nn.sigmoid(gt_ref[0])
    nk = N_SEL * page_tok
    lane = lax.broadcasted_iota(jnp.int32, (1, nk), 1)
    kposw = past - wb + lax.broadcasted_iota(jnp.int32, (1, wb), 1)
    wmask = (kposw <= qpos) & (qpos - kposw < WINDOW) & (kposw >= 0)
    pieces = []
    for hk in range(KV_HEADS):
        q8 = _head_rows(q_ref.at[0], hk, pad_rows=Q_ROWS - GROUP, pad_lanes=False)
        slope = _slope_col(hk, 1, pad_rows=Q_ROWS - GROUP)
        ibv = jnp.zeros((1, nk), jnp.int32)
        for s in range(N_SEL):
            ibv = jnp.where(lane // page_tok == s, idx_ref[n, hk * N_SEL + s], ibv)
        spos = (ibv - ibv % PAGE_BLOCKS) * BLK + lane % page_tok
        s_s = _dot(q8, sbuf[buf, hk, 0].astype(BF16)) - slope * (qpos - spos).astype(F32)
        p_s = _masked_softmax(s_s, (spos // BLK == ibv) & (spos <= qpos))
        o_s = _dot_nt(p_s.astype(BF16), sbuf[buf, hk, 1].astype(BF16))
        kwn = kwr_ref[0]
        k_new = kwn[:, hk * KV_LANES:hk * KV_LANES + HEAD_DIM]
        v_new = kwn[:, hk * KV_LANES + HEAD_DIM:(hk + 1) * KV_LANES]
        s_w = _dot(q8, win_ref[0, hk, 0].astype(BF16)) - slope * (qpos - kposw).astype(F32)
        s_w = jnp.where(wmask, s_w, NEG)
        s_n = jnp.sum(q8.astype(F32) * k_new, axis=-1, keepdims=True)
        m_w = jnp.maximum(jnp.max(s_w, axis=-1, keepdims=True), s_n)
        e_w = jnp.exp(s_w - m_w)
        e_n = jnp.exp(s_n - m_w)
        den = jnp.sum(e_w, axis=-1, keepdims=True) + e_n
        o_w = (_dot_nt(e_w.astype(BF16), win_ref[0, hk, 1].astype(BF16)) + e_n * v_new) / den
        o_c = oc_ref[0, hk]
        for g in range(GROUP):
            c0 = (hk * GROUP + g) * 3
            pieces.append(gt[:, c0:c0 + 1] * o_c[g:g + 1] + gt[:, c0 + 1:c0 + 2] * o_s[g:g + 1]
                          + gt[:, c0 + 2:c0 + 3] * o_w[g:g + 1])
    o_ref[0] = jnp.concatenate(pieces, axis=-1)


def _samp_sel(page_table, idx, q3, gates3, oc, ks_cols, kw_row, kw_cols, win_t, cache_t, past):
    db, n_pages = page_table.shape
    wb = win_t.shape[-1]
    nb_past = n_pages * PAGE_BLOCKS
    row3 = lambda w: pl.BlockSpec((1, 1, w), lambda n, pt, ix: (n, 0, 0))
    col4 = pl.BlockSpec((1, 2 * KV_HEADS, HEAD_DIM, 1), lambda n, pt, ix: (n, 0, 0, 0))
    win_spec = pl.BlockSpec((1, KV_HEADS, 2, HEAD_DIM, wb), lambda n, pt, ix: (n, 0, 0, 0, 0))
    grid_spec = pltpu.PrefetchScalarGridSpec(
        num_scalar_prefetch=2,
        grid=(db,),
        in_specs=[row3(C_B), row3(GATE_PAD),
                  pl.BlockSpec((1, KV_HEADS, Q_ROWS, HEAD_DIM), lambda n, pt, ix: (n, 0, 0, 0)),
                  col4, row3(C_KV), col4, win_spec, pl.BlockSpec(memory_space=pl.ANY)],
        out_specs=[row3(C_B), win_spec],
        scratch_shapes=[pltpu.VMEM((2, KV_HEADS, 2, HEAD_DIM, N_SEL * PAGE_BLOCKS * BLK), F32),
                        pltpu.SemaphoreType.DMA((2, KV_HEADS * N_SEL))],
    )
    return pl.pallas_call(
        functools.partial(_samp_sel_body, nb_past=nb_past, past=past, wb=wb, db=db),
        grid_spec=grid_spec,
        out_shape=[jax.ShapeDtypeStruct((db, 1, C_B), F32), jax.ShapeDtypeStruct(win_t.shape, F32)],
        compiler_params=pltpu.CompilerParams(dimension_semantics=("arbitrary",)),
        name="nsa_sample_sel",
    )(page_table, idx, q3, gates3, oc, ks_cols, kw_row, kw_cols, win_t, cache_t)


FF_CHUNK = 256


def _merge_body(x_ref, y_ref, bonus_ref, g_ref, yb_ref, mg_ref, p_ref, cbuf_ref,
                lng_ref, lnb_ref, bd_ref, woa_ref, wob_ref, wout_ref, n2_ref, wup_ref, cw_ref, cb_ref, wdn_ref,
                n3_ref, wpe_ref, wpg_ref, fg_ref, o_ref, cnew_ref, carry_ref, *, seq_mode, final, d_ff, tm):
    i = pl.program_id(0)
    d_model = x_ref.shape[1]
    y = y_ref[...]
    inv = 1.0 / A_HEAD_DIM
    mean = _dot(y, bd_ref[...], precision=HI) * inv
    d = y - mean
    var = _dot(d * d, bd_ref[...], precision=HI) * inv
    ya = (d * lax.rsqrt(var + LNX_EPS) * lng_ref[...] + lnb_ref[...] + bonus_ref[...]) * g_ref[...]
    mg = mg_ref[...]
    m = (jax.nn.sigmoid(mg[:, :d_model]) * _dot(ya.astype(BF16), woa_ref[...])
         + jax.nn.sigmoid(mg[:, d_model:]) * _dot(yb_ref[...].astype(BF16), wob_ref[...]))
    h = x_ref[...] + _dot(m.astype(BF16), wout_ref[...])
    xn = _rms(h, n2_ref[...]).astype(BF16)

    if seq_mode:
        @pl.when(i == 0)
        def _():
            carry_ref[...] = jnp.zeros_like(carry_ref)
        rid = lax.broadcasted_iota(jnp.int32, (tm, FF_CHUNK), 0)

    acc = jnp.zeros((tm, d_model), F32)
    for c in range(d_ff // FF_CHUNK):
        parts = []
        for half in range(2):
            cs = slice(half * d_ff + c * FF_CHUNK, half * d_ff + (c + 1) * FF_CHUNK)
            up = _dot(xn, wup_ref[:, cs])
            if seq_mode:
                t1 = carry_ref[SUBLANES - 1:SUBLANES, cs]
                t2 = carry_ref[SUBLANES - 2:SUBLANES - 1, cs]
                up1 = jnp.where(rid == 0, t1, pltpu.roll(up, 1, 0))
                up2 = jnp.where(rid == 0, t2, jnp.where(rid == 1, t1, pltpu.roll(up, 2, 0)))
                carry_ref[:, cs] = up[tm - SUBLANES:, :]
            else:
                up2 = cbuf_ref[:, cs]
                up1 = cbuf_ref[:, 2 * d_ff + cs.start:2 * d_ff + cs.stop]
                cnew_ref[:, cs] = up1
                cnew_ref[:, 2 * d_ff + cs.start:2 * d_ff + cs.stop] = up
            parts.append(cb_ref[:, cs] + cw_ref[0:1, cs] * up2 + cw_ref[1:2, cs] * up1 + cw_ref[2:3, cs] * up)
        a, gate = parts
        act = (a * jax.nn.sigmoid(a) * gate).astype(BF16)
        acc = acc + _dot(act, wdn_ref[c * FF_CHUNK:(c + 1) * FF_CHUNK, :])
    if seq_mode:
        cnew_ref[...] = carry_ref[...]
    h = h + acc
    pe = _dot(p_ref[...].astype(BF16), wpe_ref[...])
    h = h + pe * jax.nn.sigmoid(_dot(_rms(h, n3_ref[...]).astype(BF16), wpg_ref[...]))
    o_ref[...] = _rms(h, fg_ref[...]) if final else h


def _merge(x, y, bonus, g, yb, mg, p, cbuf, consts, tm, seq_mode, final):
    rows, d_model = x.shape
    d_ff = consts[10].shape[0]
    f2 = 2 * d_ff
    rowspec = lambda w: pl.BlockSpec((tm, w), lambda i: (i, 0))
    if seq_mode:
        cbuf_spec = _const_spec(cbuf.shape)
        cnew_shape, cnew_spec = (SUBLANES, f2), pl.BlockSpec((SUBLANES, f2), lambda i: (0, 0))
    else:
        cbuf_spec = rowspec(2 * f2)
        cnew_shape, cnew_spec = (rows, 2 * f2), rowspec(2 * f2)
    wbytes = sum(int(c.size) * c.dtype.itemsize for c in consts)
    act = tm * (d_model * 3 + C_A * 4 + 256) * 4 + (0 if seq_mode else 2 * tm * 2 * f2 * 4)
    vmem = wbytes + 2 * act + 8 * tm * d_model * 4 + (8 << 20)
    return pl.pallas_call(
        functools.partial(_merge_body, seq_mode=seq_mode, final=final, d_ff=d_ff, tm=tm),
        grid=(rows // tm,),
        in_specs=[rowspec(d_model), rowspec(C_A), rowspec(C_A), rowspec(C_A), rowspec(C_B), rowspec(2 * d_model),
                  rowspec(p.shape[1]), cbuf_spec] + [_const_spec(c.shape) for c in consts],
        out_specs=[rowspec(d_model), cnew_spec],
        out_shape=[jax.ShapeDtypeStruct((rows, d_model), F32), jax.ShapeDtypeStruct(cnew_shape, F32)],
        scratch_shapes=[pltpu.VMEM((SUBLANES, f2), F32)],
        compiler_params=pltpu.CompilerParams(dimension_semantics=("arbitrary",), vmem_limit_bytes=_vmem_limit(vmem)),
        name="merge_ffn",
    )(x, y, bonus, g, yb, mg, p, cbuf, *consts)


def _pick_tile(n, target):
    t = min(n, target)
    while n % t:
        t //= 2
    return t


def _head_block_diag():
    h = jnp.arange(C_A) // A_HEAD_DIM
    return (h[:, None] == h[None, :]).astype(F32)


def _compress_weights(pe, w1, b1, w2):
    eye = jnp.eye(KV_HEADS, dtype=F32)
    eye_c = jnp.eye(2, dtype=F32)
    w_big = jnp.einsum('crdf,hg,ce->rhcdgef', w1, eye, eye_c).reshape(BLK * C_KV, KV_HEADS * 2 * D_CMP)
    w2_big = jnp.einsum('cfd,hg,ce->hcfged', w2, eye, eye_c).reshape(KV_HEADS * 2 * D_CMP, C_KV)
    pe_row = jnp.broadcast_to(jnp.transpose(pe, (1, 0, 2))[:, None], (BLK, KV_HEADS, 2, HEAD_DIM)).reshape(1, -1)
    b1_row = jnp.broadcast_to(b1[None], (KV_HEADS, 2, D_CMP)).reshape(1, -1)
    return w_big.astype(BF16), w2_big.astype(BF16), pe_row, b1_row


def kernel(x_prompt, x_sample, p_prompt, p_sample, cache_cmp_kv, cache_sel_kv, page_table, state_win_kv, state_wkv, state_shift, state_ffn_conv, norm1_g, w_in, shift_mu, rwkv_w0, rwkv_w2, rwkv_a0, rwkv_a2, rwkv_g2, rwkv_k_k, rwkv_k_a, rwkv_r_k, lnx_g, lnx_b, cmp_pe, cmp_w1, cmp_b1, cmp_w2, w_oa, w_ob, w_out, norm2_g, w_up, conv_w, conv_b, w_down, norm3_g, w_pe, w_pg, final_g):
    depth = w_in.shape[0]
    b, t, d_model = x_prompt.shape
    db, dt, _ = x_sample.shape
    n_pool, page = cache_cmp_kv.shape[1], cache_cmp_kv.shape[2]
    n_pages = page_table.shape[1]
    past = n_pages * page
    d_ff = w_down.shape[1]
    f2 = 2 * d_ff
    wb = state_win_kv.shape[2]
    assert b == 1 and dt == 1 and page == PAGE_BLOCKS * BLK
    assert t % BLK == 0 and t >= WINDOW + 128 and d_ff % FF_CHUNK == 0 and wb == WINDOW and past >= WINDOW
    assert N_SEL <= LANES and n_pages * PAGE_BLOCKS + 1 >= N_SEL

    bd = _head_block_diag()
    hp = x_prompt.reshape(t, d_model)
    hs = x_sample.reshape(db, d_model)
    outs = [[] for _ in range(12)]
    for i in range(depth):
        o_g = C_SHIFT + C_B + 3 * C_KV
        w_perm = jnp.concatenate(
            [w_in[i][:, :o_g], jnp.pad(w_in[i][:, o_g:o_g + 3 * Q_HEADS], ((0, 0), (0, GATE_PAD - 3 * Q_HEADS))),
             w_in[i][:, o_g + 3 * Q_HEADS:]], axis=1).astype(BF16)
        g1 = norm1_g[i].reshape(1, -1)
        rw = (shift_mu[i], rwkv_w0[i], rwkv_w2[i], rwkv_a0[i], rwkv_a2[i], rwkv_g2[i], rwkv_k_k[i], rwkv_k_a[i],
              rwkv_r_k[i].reshape(-1))
        w_big, w2_big, pe_row, b1_row = _compress_weights(cmp_pe[i], cmp_w1[i], cmp_b1[i], cmp_w2[i])
        zero_bias = jnp.zeros_like(b1_row)
        cmp_bias = _compress(jnp.broadcast_to(pe_row, (SUBLANES, pe_row.shape[1])), w_big, zero_bias, w2_big,
                             SUBLANES, 2048, mlp=False)[0:1] + b1_row
        eye_b = jnp.eye(PAGE_BLOCKS, dtype=F32)
        w_tok = jnp.einsum('crdf,eg->cdergf', cmp_w1[i], eye_b).reshape(2 * HEAD_DIM * PAGE_BLOCKS * BLK,
                                                                        PAGE_BLOCKS * D_CMP).astype(BF16)
        w2_pair = jnp.einsum('cfd,eg->cefgd', cmp_w2[i], eye_b).reshape(2, PAGE_BLOCKS * D_CMP,
                                                                       PAGE_BLOCKS * HEAD_DIM).astype(BF16)
        bias_pair = jnp.tile(cmp_bias[0, :2 * D_CMP].reshape(2, D_CMP), (1, PAGE_BLOCKS))
        row = lambda z: z.reshape(1, -1)
        mconsts = [row(lnx_g[i]), row(lnx_b[i]), bd, w_oa[i].astype(BF16), w_ob[i].astype(BF16),
                   w_out[i].astype(BF16), row(norm2_g[i]), w_up[i].astype(BF16), conv_w[i], row(conv_b[i]),
                   w_down[i].astype(BF16), row(norm3_g[i]), w_pe[i].astype(BF16), w_pg[i].astype(BF16), row(final_g)]
        final = i == depth - 1

        pa, q, kvc, kvs, kvs16, kvw, kvw16, gates, mg = _proj(hp, g1, w_perm, _pick_tile(t, 256))
        seqs = _rwkv_prep(pa, None, rw, bd, _pick_tile(t, 512), True)
        r_, w_, lw_, k_, v_, a_, b_, g_, bonus = seqs
        y, s_new = _rwkv_chunk_scan([z.reshape(1, t, C_A) for z in (r_, lw_, k_, v_, a_, b_)],
                                    jnp.zeros((1, A_HEADS, A_HEAD_DIM, A_HEAD_DIM), F32), _pick_tile(t, 512))
        kvc_blocks = _compress(kvc.reshape(t // BLK, BLK * C_KV), w_big, cmp_bias, w2_big,
                               _pick_tile(t // BLK, 256), 2048)
        yb = _nsa_prompt(q, gates, kvc_blocks, kvs16, kvw16, 128, 512)
        hp, conv_new = _merge(hp, y.reshape(t, C_A), bonus, g_, yb, mg, p_prompt[i].reshape(t, -1),
                              jnp.zeros((SUBLANES, LANES), F32), mconsts, _pick_tile(t, 256), True, final)
        kv6 = lambda z, n_: z.reshape(n_, -1, KV_HEADS, 2, HEAD_DIM)
        outs[0].append(kv6(kvc, 1))
        outs[2].append(kv6(kvs, 1))
        outs[4].append(kv6(kvw[t - min(WINDOW, t):], 1))
        outs[6].append(s_new)
        outs[8].append(pa[t - 1:t])
        outs[10].append(conv_new[SUBLANES - (CONV_W - 1):].reshape(1, CONV_W - 1, f2))

        pa, q, kvc, kvs, kvs16, kvw, kvw16, gates, mg = _proj(hs, g1, w_perm, _pick_tile(db, 128))
        seqs = _rwkv_prep(pa, state_shift[i], rw, bd, _pick_tile(db, 128), False)
        r_, w_, lw_, k_, v_, a_, b_, g_, bonus = seqs
        y, s_new = _rwkv_scan([z.reshape(db, 1, C_A) for z in (r_, w_, k_, v_, a_, b_)], state_wkv[i], 1)
        to_tok_minor = lambda z: jnp.transpose(z, (0, 2, 3, 4, 1))
        pool = _compress_pool(to_tok_minor(cache_cmp_kv[i]).reshape(n_pool * KV_HEADS, 2 * HEAD_DIM, page), w_tok, bias_pair,
                              w2_pair, _pick_tile(n_pool, 64))
        q3 = q.reshape(db, 1, C_B)
        oc, sel_scores = _samp_cmp(page_table, q3, pool.reshape(n_pool, KV_HEADS * C_KV), past)
        idx = _samp_topk(sel_scores.reshape(db * KV_HEADS, -1), n_pages)
        idx2 = idx[:, :N_SEL].reshape(db, KV_HEADS * N_SEL)
        cols = lambda z: z.reshape(db, 2 * KV_HEADS, HEAD_DIM, 1)
        yb, win_new_t = _samp_sel(page_table, idx2, q3, gates.reshape(db, 1, GATE_PAD), oc, cols(kvs),
                                  kvw.reshape(db, 1, C_KV), cols(kvw), to_tok_minor(state_win_kv[i]),
                                  to_tok_minor(cache_sel_kv[i]), past)
        win_new = jnp.transpose(win_new_t, (0, 4, 1, 2, 3))
        hs, conv_new = _merge(hs, y.reshape(db, C_A), bonus, g_, yb.reshape(db, C_B), mg, p_sample[i].reshape(db, -1),
                              state_ffn_conv[i].reshape(db, 2 * f2), mconsts, _pick_tile(db, 128), False, final)
        outs[1].append(kv6(kvc, db))
        outs[3].append(kv6(kvs, db))
        outs[5].append(win_new.reshape(db, wb, KV_HEADS, 2, HEAD_DIM))
        outs[7].append(s_new)
        outs[9].append(pa)
        outs[11].append(conv_new.reshape(db, CONV_W - 1, f2))

    stacked = [jnp.stack(o) for o in outs]
    return (hp.reshape(b, t, d_model), hs.reshape(db, dt, d_model), *stacked)
```

```python
import functools

import jax
import jax.numpy as jnp
from jax import lax
from jax.experimental import pallas as pl
from jax.experimental.pallas import tpu as pltpu

F32 = jnp.float32
BF16 = jnp.bfloat16
HI = lax.Precision.HIGHEST

A_HEADS = 8
A_HEAD_DIM = 64
C_A = A_HEADS * A_HEAD_DIM
R_W = 64
R_A = 64
R_G = 128
C_SHIFT = 3 * C_A + R_W + R_A + R_G
LNX_EPS = 64e-5
Q_HEADS = 8
KV_HEADS = 2
GROUP = Q_HEADS // KV_HEADS
HEAD_DIM = 64
C_B = Q_HEADS * HEAD_DIM
C_KV = KV_HEADS * 2 * HEAD_DIM
BLK = 64
N_SEL = 16
WINDOW = 512
D_CMP = 128
FORCE_SCORE = 1e4
CONV_W = 3
NORM_EPS = 1e-6

LANES = 128
SUBLANES = 8
VMEM_BYTES_V7X = 64 * 1024 * 1024

NEG = -1e30
M_FLOOR = -1e29
KV_LANES = 2 * HEAD_DIM

GATE_PAD = LANES


def _vmem_limit(nbytes):
    return int(min(max(nbytes, 16 * 1024 * 1024), VMEM_BYTES_V7X - 8 * 1024 * 1024))


def _const_spec(shape):
    nd = len(shape)
    return pl.BlockSpec(shape, lambda *_: (0,) * nd, pipeline_mode=pl.Buffered(1))


def _rms(x, g):
    return x * lax.rsqrt(jnp.mean(x * x, axis=-1, keepdims=True) + NORM_EPS) * g


def _dot(a, b, **kw):
    return jnp.dot(a, b, preferred_element_type=F32, **kw)


def _dot_ones(x, ones):
    hi = x.astype(BF16)
    lo = (x - hi.astype(F32)).astype(BF16)
    return _dot(hi, ones) + _dot(lo, ones)


def _dot_nt(a, b):
    return lax.dot_general(a, b, (((1,), (1,)), ((), ())), preferred_element_type=F32)


def _masked_softmax(s, mask):
    s = jnp.where(mask, s, NEG)
    m = jnp.max(s, axis=-1, keepdims=True)
    m = jnp.where(m > M_FLOOR, m, 0.0)
    e = jnp.exp(s - m)
    return e / jnp.maximum(jnp.sum(e, axis=-1, keepdims=True), 1e-30)


def _slope_col(hk, rows_per_head, pad_rows=0):
    cols = [jnp.full((rows_per_head, 1), 2.0 ** (-(hk * GROUP + g + 1)), F32) for g in range(GROUP)]
    if pad_rows:
        cols.append(jnp.zeros((pad_rows, 1), F32))
    return jnp.concatenate(cols, axis=0)


def _proj_body(x_ref, g_ref, w_ref, *o_refs, segs):
    xb = _rms(x_ref[...], g_ref[...]).astype(BF16)
    it = iter(o_refs)
    for off, width, outs in segs:
        r = _dot(xb, w_ref[:, off:off + width])
        for scale in outs:
            o_ref = next(it)
            o_ref[...] = (r * scale if scale != 1.0 else r).astype(o_ref.dtype)


def _proj(x, g, w_perm, tm):
    rows, d = x.shape
    d_model = d
    o = 0
    segs, shapes = [], []
    for width, outs in ((C_SHIFT, ((F32, 1.0),)), (C_B, ((BF16, HEAD_DIM ** -0.5),)), (C_KV, ((F32, 1.0),)),
                        (C_KV, ((F32, 1.0), (BF16, 1.0))), (C_KV, ((F32, 1.0), (BF16, 1.0))),
                        (GATE_PAD, ((F32, 1.0),)), (2 * d_model, ((F32, 1.0),))):
        segs.append((o, width, tuple(s for _, s in outs)))
        shapes += [(width, dt) for dt, _ in outs]
        o += width
    n_tot = o
    out_bytes = sum(tm * w * jnp.dtype(dt).itemsize for w, dt in shapes)
    vmem = 2 * tm * d * 4 + d * n_tot * 2 + 2 * out_bytes + (8 << 20)
    return pl.pallas_call(
        functools.partial(_proj_body, segs=tuple(segs)),
        grid=(rows // tm,),
        in_specs=[pl.BlockSpec((tm, d), lambda i: (i, 0)), _const_spec((1, d)), _const_spec((d, n_tot))],
        out_specs=[pl.BlockSpec((tm, w), lambda i: (i, 0)) for w, _ in shapes],
        out_shape=[jax.ShapeDtypeStruct((rows, w), dt) for w, dt in shapes],
        compiler_params=pltpu.CompilerParams(dimension_semantics=("arbitrary",), vmem_limit_bytes=_vmem_limit(vmem)),
        name="proj",
    )(x, g, w_perm)


def _prep_body(pa_ref, prev_ref, mu_ref, w0_ref, w2_ref, a0_ref, a2_ref, g2_ref, kk_ref, ka_ref, rk_ref, bd_ref,
               r_o, w_o, lw_o, k_o, v_o, a_o, b_o, g_o, bonus_o, *, seq_mode):
    pf = pa_ref[...]
    if seq_mode:
        i = pl.program_id(0)
        first = jnp.where(i > 0, prev_ref[SUBLANES - 1:SUBLANES, :], 0.0)
        rid = lax.broadcasted_iota(jnp.int32, pf.shape, 0)
        prev = jnp.where(rid == 0, first, pltpu.roll(pf, 1, 0))
    else:
        prev = prev_ref[...]
    xs = pf + (prev - pf) * mu_ref[...]
    r = xs[:, 0:C_A]
    k = xs[:, C_A:2 * C_A]
    v = xs[:, 2 * C_A:3 * C_A]
    o = 3 * C_A
    wd = xs[:, o:o + R_W]
    ad = xs[:, o + R_W:o + R_W + R_A]
    gd = xs[:, o + R_W + R_A:]
    nz = -(w0_ref[...] + _dot(jnp.tanh(wd), w2_ref[...], precision=HI))
    softplus = jnp.maximum(nz, 0.0) + jnp.log(1.0 + jnp.exp(-jnp.abs(nz)))
    w = -softplus - 0.5
    a = jax.nn.sigmoid(a0_ref[...] + _dot(ad, a2_ref[...], precision=HI))
    g = _dot(jax.nn.sigmoid(gd), g2_ref[...], precision=HI)
    kk = k * kk_ref[...]
    ss = _dot_ones(kk * kk, bd_ref[...])
    kkn = kk / jnp.maximum(jnp.sqrt(ss), 1e-12)
    k2 = k * (1.0 + (a - 1.0) * ka_ref[...])
    r_o[...] = r
    lw = -jnp.exp(w)
    lw_o[...] = lw
    w_o[...] = jnp.exp(lw)
    k_o[...] = k2
    v_o[...] = v
    a_o[...] = -kkn
    b_o[...] = kkn * a
    g_o[...] = g
    bonus_o[...] = _dot_ones(r * k2 * rk_ref[...], bd_ref[...]) * v


def _rwkv_prep(pa, prev, rw, bd, tm, seq_mode):
    rows = pa.shape[0]
    mu, w0, w2, a0, a2, g2, k_k, k_a, r_k = rw
    if seq_mode:
        tb = tm // SUBLANES
        prev_spec = pl.BlockSpec((SUBLANES, C_SHIFT), lambda i: (jnp.maximum(i * tb - 1, 0), 0))
        prev = pa
    else:
        prev_spec = pl.BlockSpec((tm, C_SHIFT), lambda i: (i, 0))
    row = lambda z: z.reshape(1, -1)
    consts = [row(mu), row(w0), w2, row(a0), a2, g2, row(k_k), row(k_a), row(r_k), bd]
    vmem = 4 * tm * C_SHIFT * 4 + 2 * 8 * tm * C_A * 4 + 16 * tm * C_A * 4 + (8 << 20)
    return pl.pallas_call(
        functools.partial(_prep_body, seq_mode=seq_mode),
        grid=(rows // tm,),
        in_specs=[pl.BlockSpec((tm, C_SHIFT), lambda i: (i, 0)), prev_spec] + [_const_spec(c.shape) for c in consts],
        out_specs=[pl.BlockSpec((tm, C_A), lambda i: (i, 0))] * 9,
        out_shape=[jax.ShapeDtypeStruct((rows, C_A), F32)] * 9,
        compiler_params=pltpu.CompilerParams(dimension_semantics=("arbitrary",), vmem_limit_bytes=_vmem_limit(vmem)),
        name="rwkv_prep",
    )(pa, prev, *consts)


N_PAIR = A_HEADS // 2


def _scan_body(r_ref, w_ref, k_ref, v_ref, a_ref, b_ref, s0_ref, y_ref, so_ref, st_ref, *, tc, nc):
    c = pl.program_id(1)

    @pl.when(c == 0)
    def _():
        for p in range(N_PAIR):
            st_ref[p] = jnp.concatenate([s0_ref[0, 2 * p], s0_ref[0, 2 * p + 1]], axis=-1)

    shape = (A_HEAD_DIM, LANES)
    lane = lax.broadcasted_iota(jnp.int32, shape, 1)
    sub = lax.broadcasted_iota(jnp.int32, shape, 0)
    lo = lane < A_HEAD_DIM
    diag = (lane & (A_HEAD_DIM - 1)) == sub

    def seg_sum(x):
        s_lo = jnp.sum(jnp.where(lo, x, 0.0), axis=-1, keepdims=True)
        s_hi = jnp.sum(jnp.where(lo, 0.0, x), axis=-1, keepdims=True)
        return jnp.where(lo, s_lo, s_hi)

    grp = min(SUBLANES, tc)

    def token_group(gi, carry):
        base = pl.multiple_of(gi * grp, grp)
        for p in range(N_PAIR):
            sl = slice(LANES * p, LANES * (p + 1))
            rt, wt, kt, vt, at, bt = (ref[0, pl.ds(base, grp), sl] for ref in (r_ref, w_ref, k_ref, v_ref, a_ref, b_ref))
            s = st_ref[p]
            ys = []
            for j in range(grp):
                row = lambda z: z[j:j + 1, :]
                sa = seg_sum(s * row(at))
                v_col = seg_sum(jnp.where(diag, row(vt), 0.0))
                s = s * row(wt) + sa * row(bt) + v_col * row(kt)
                y_col = seg_sum(s * row(rt))
                ys.append(jnp.sum(jnp.where(diag, y_col, 0.0), axis=0, keepdims=True))
            st_ref[p] = s
            y_ref[0, pl.ds(base, grp), sl] = jnp.concatenate(ys, axis=0) if grp > 1 else ys[0]
        return carry

    lax.fori_loop(0, tc // grp, token_group, 0)

    @pl.when(c == nc - 1)
    def _():
        for p in range(N_PAIR):
            s = st_ref[p]
            so_ref[0, 2 * p] = s[:, :A_HEAD_DIM]
            so_ref[0, 2 * p + 1] = s[:, A_HEAD_DIM:]


def _rwkv_scan(seqs, s0, tc):
    b, t, _ = seqs[0].shape
    nc = t // tc
    seq_spec = pl.BlockSpec((1, tc, C_A), lambda i, c: (i, c, 0))
    st_spec = pl.BlockSpec((1, A_HEADS, A_HEAD_DIM, A_HEAD_DIM), lambda i, c: (i, 0, 0, 0))
    return pl.pallas_call(
        functools.partial(_scan_body, tc=tc, nc=nc),
        grid=(b, nc),
        in_specs=[seq_spec] * 6 + [st_spec],
        out_specs=[seq_spec, st_spec],
        out_shape=[jax.ShapeDtypeStruct((b, t, C_A), F32), jax.ShapeDtypeStruct(s0.shape, F32)],
        scratch_shapes=[pltpu.VMEM((N_PAIR, A_HEAD_DIM, LANES), F32)],
        compiler_params=pltpu.CompilerParams(dimension_semantics=("arbitrary", "arbitrary")),
        name="rwkv_scan",
    )(*seqs, s0)


SCAN_CHUNK = 64
SCAN_CHUNKS_PER_STEP = 4


def _mm(a, b):
    return _dot(a.astype(BF16), b.astype(BF16))


def _chunk_scan_body(r_ref, lw_ref, k_ref, v_ref, a_ref, b_ref, s0_ref, y_ref, so_ref, st_ref, *, tc, nc):
    cidx = pl.program_id(1)
    c = SCAN_CHUNK
    d = A_HEAD_DIM

    @pl.when(cidx == 0)
    def _():
        for h in range(A_HEADS):
            st_ref[h] = s0_ref[0, h].T

    ri = lax.broadcasted_iota(jnp.int32, (c, c), 0)
    ci = lax.broadcasted_iota(jnp.int32, (c, c), 1)
    lower = ci <= ri
    strict = ci < ri
    eye = lax.broadcasted_iota(jnp.int32, (d, d), 0) == lax.broadcasted_iota(jnp.int32, (d, d), 1)
    n_double = (c - 1).bit_length()
    per = SCAN_CHUNKS_PER_STEP
    slab = per * c
    sr = lax.broadcasted_iota(jnp.int32, (slab, slab), 0)
    sc = lax.broadcasted_iota(jnp.int32, (slab, slab), 1)
    tri = ((sr // c == sc // c) & (sc <= sr)).astype(F32)

    def step_fn(step, carry):
        base = pl.multiple_of(step * slab, slab)
        r, lw, k, v, a, b = (ref[0, pl.ds(base, slab), :] for ref in (r_ref, lw_ref, k_ref, v_ref, a_ref, b_ref))
        cum = _dot(tri, lw, precision=HI)
        tots = [cum[(q + 1) * c - 1:(q + 1) * c, :] for q in range(per)]
        tot = jnp.concatenate([jnp.broadcast_to(t_, (c, C_A)) for t_ in tots], axis=0)
        e_inv = jnp.exp(-cum)
        e_rest = jnp.exp(tot - cum)
        at = a * jnp.exp(cum - lw)
        rt = r * jnp.exp(cum)
        bt = b * e_inv
        kt = k * e_inv
        bh = b * e_rest
        kh = k * e_rest
        g_tot = [jnp.exp(t_) for t_ in tots]
        units = [(q, h) for q in range(per) for h in range(A_HEADS)]
        cut = lambda z, u: z[u[0] * c:(u[0] + 1) * c, u[1] * d:(u[1] + 1) * d]
        g4 = [_dot_nt(jnp.concatenate([cut(at, u), cut(rt, u)], axis=0).astype(BF16),
                      jnp.concatenate([cut(bt, u), cut(kt, u)], axis=0).astype(BF16)) for u in units]
        lp = [jnp.where(strict, g[:c, :c], 0.0).astype(BF16) for g in g4]
        m_l = [jnp.where(strict, g[:c, c:], 0.0) for g in g4]
        p_b = [jnp.where(lower, g[c:, :c], 0.0) for g in g4]
        p_k = [jnp.where(lower, g[c:, c:], 0.0) for g in g4]
        vb = [cut(v, u).astype(BF16) for u in units]
        n_u = range(len(units))
        z = [jnp.concatenate([cut(at, units[i]), _mm(m_l[i], vb[i])], axis=-1) for i in n_u]
        for q in range(n_double):
            z = [z[i] + _mm(lp[i], z[i]) for i in n_u]
            if q < n_double - 1:
                lp = [_mm(lp[i], lp[i]).astype(BF16) for i in n_u]
        zb = [zz.astype(BF16) for zz in z]
        bz = [_mm(cut(bh, units[i]).T, zb[i]) for i in n_u]
        kv = [_mm(cut(kh, units[i]).T, vb[i]) for i in n_u]
        pz = [_mm(p_b[i], zb[i]) for i in n_u]
        pv = [_mm(p_k[i], vb[i]) for i in n_u]
        rows = []
        for q in range(per):
            ys = []
            for h in range(A_HEADS):
                i = q * A_HEADS + h
                st = st_ref[h]
                a_c = jnp.where(eye, g_tot[q][:, h * d:(h + 1) * d], 0.0) + bz[i][:, :d]
                ys.append(_dot(cut(rt, units[i]) + pz[i][:, :d], st, precision=HI) + pz[i][:, d:] + pv[i])
                st_ref[h] = _dot(a_c, st, precision=HI) + bz[i][:, d:] + kv[i]
            rows.append(jnp.concatenate(ys, axis=-1))
        y_ref[0, pl.ds(base, slab), :] = jnp.concatenate(rows, axis=0)
        return carry

    lax.fori_loop(0, tc // slab, step_fn, 0)

    @pl.when(cidx == nc - 1)
    def _():
        for h in range(A_HEADS):
            so_ref[0, h] = st_ref[h].T


def _rwkv_chunk_scan(seqs, s0, tc):
    b, t, _ = seqs[0].shape
    nc = t // tc
    seq_spec = pl.BlockSpec((1, tc, C_A), lambda i, c: (i, c, 0))
    st_spec = pl.BlockSpec((1, A_HEADS, A_HEAD_DIM, A_HEAD_DIM), lambda i, c: (i, 0, 0, 0))
    return pl.pallas_call(
        functools.partial(_chunk_scan_body, tc=tc, nc=nc),
        grid=(b, nc),
        in_specs=[seq_spec] * 6 + [st_spec],
        out_specs=[seq_spec, st_spec],
        out_shape=[jax.ShapeDtypeStruct((b, t, C_A), F32), jax.ShapeDtypeStruct(s0.shape, F32)],
        scratch_shapes=[pltpu.VMEM((A_HEADS, A_HEAD_DIM, A_HEAD_DIM), F32)],
        compiler_params=pltpu.CompilerParams(dimension_semantics=("arbitrary", "arbitrary")),
        name="rwkv_chunk_scan",
    )(*seqs, s0)


def _gelu_tanh(x):
    return 0.5 * x * (1.0 + jnp.tanh(0.7978845608028654 * (x + 0.044715 * (x * x * x))))


def _compress_body(x_ref, w_ref, b_ref, w2_ref, o_ref, acc_ref, *, nk, mlp):
    kk = pl.program_id(1)

    @pl.when(kk == 0)
    def _():
        acc_ref[...] = jnp.zeros_like(acc_ref)

    acc_ref[...] += _dot(x_ref[...].astype(BF16), w_ref[...])

    @pl.when(kk == nk - 1)
    def _():
        h = acc_ref[...] + b_ref[...]
        if mlp:
            o_ref[...] = _dot(_gelu_tanh(h).astype(BF16), w2_ref[...])
        else:
            o_ref[...] = h


def _compress(x2, w_big, bias, w2_big, tm, tk, mlp=True):
    m, kdim = x2.shape
    nh = w_big.shape[1]
    n_out = w2_big.shape[1] if mlp else nh
    nk = kdim // tk
    vmem = 2 * tm * tk * 4 + 2 * tk * nh * 2 + 3 * tm * nh * 4 + 2 * tm * n_out * 4 + (8 << 20)
    return pl.pallas_call(
        functools.partial(_compress_body, nk=nk, mlp=mlp),
        grid=(m // tm, nk),
        in_specs=[pl.BlockSpec((tm, tk), lambda i, k: (i, k)), pl.BlockSpec((tk, nh), lambda i, k: (k, 0)),
                  _const_spec(bias.shape), _const_spec(w2_big.shape)],
        out_specs=pl.BlockSpec((tm, n_out), lambda i, k: (i, 0)),
        out_shape=jax.ShapeDtypeStruct((m, n_out), F32),
        scratch_shapes=[pltpu.VMEM((tm, nh), F32)],
        compiler_params=pltpu.CompilerParams(dimension_semantics=("arbitrary", "arbitrary"),
                                             vmem_limit_bytes=_vmem_limit(vmem)),
        name="nsa_compress",
    )(x2, w_big, bias, w2_big)


PAGE_ROWS = KV_HEADS * 2 * HEAD_DIM


def _compress_pool_body(x_ref, w_ref, b_ref, w2_ref, o_ref, *, pg):
    m = pg * KV_HEADS
    for c in range(2):
        xt = jnp.swapaxes(x_ref[:, c * HEAD_DIM:(c + 1) * HEAD_DIM, :], 0, 1)
        acc = jnp.zeros((m, 2 * D_CMP), F32)
        for d in range(0, HEAD_DIM, 2):
            r0 = c * HEAD_DIM + d
            x_pair = jnp.concatenate([xt[d], xt[d + 1]], axis=-1).astype(BF16)
            acc = acc + _dot(x_pair, w_ref[r0 * 2 * BLK:(r0 + 2) * 2 * BLK, :])
        h = acc + b_ref[c:c + 1, :]
        o_ref[:, c * 2 * HEAD_DIM:(c + 1) * 2 * HEAD_DIM] = _dot(_gelu_tanh(h).astype(BF16), w2_ref[c])


def _compress_pool(x2, w_tok, bias2, w2_pair, pg):
    n_pool = x2.shape[0] // KV_HEADS
    m = pg * KV_HEADS
    vmem = 3 * pg * PAGE_ROWS * 2 * BLK * 4 + int(w_tok.size) * 2 + 8 * m * 2 * D_CMP * 4 + (4 << 20)
    return pl.pallas_call(
        functools.partial(_compress_pool_body, pg=pg),
        grid=(n_pool // pg,),
        in_specs=[pl.BlockSpec((m, 2 * HEAD_DIM, 2 * BLK), lambda i: (i, 0, 0)), _const_spec(w_tok.shape),
                  _const_spec(bias2.shape), _const_spec(w2_pair.shape)],
        out_specs=pl.BlockSpec((m, C_KV), lambda i: (i, 0)),
        out_shape=jax.ShapeDtypeStruct((n_pool * KV_HEADS, C_KV), F32),
        compiler_params=pltpu.CompilerParams(dimension_semantics=("arbitrary",), vmem_limit_bytes=_vmem_limit(vmem)),
        name="nsa_compress_pool",
    )(x2, w_tok, bias2, w2_pair)


def _topk_mask(s, ids, k, axis):
    sel = jnp.zeros(s.shape, F32)
    for _ in range(k):
        m = jnp.max(s, axis=axis, keepdims=True)
        pick = jnp.min(jnp.where(s == m, ids, 1e9), axis=axis, keepdims=True)
        hit = ids == pick
        sel = jnp.where(hit, 1.0, sel)
        s = jnp.where(hit, -jnp.inf, s)
    return sel


def _head_rows(q_ref, hk, pad_rows=0, pad_lanes=True):
    parts = [q_ref[:, (hk * GROUP + g) * HEAD_DIM:(hk * GROUP + g + 1) * HEAD_DIM] for g in range(GROUP)]
    if pad_rows:
        parts.append(jnp.zeros((pad_rows, HEAD_DIM), parts[0].dtype))
    qh = jnp.concatenate(parts, axis=0)
    return jnp.concatenate([qh, jnp.zeros_like(qh)], axis=-1) if pad_lanes else qh


AUG_HI = HEAD_DIM
AUG_LO = HEAD_DIM + 1
AUG_ONE = HEAD_DIM + 2
AUG_BLK = HEAD_DIM + 3
POS_SPLIT = LANES


def _nsa_select_body(q_ref, kc_ref, kct_ref, oc_ref, ns_ref, flag_ref, *, tq, nbc):
    i = pl.program_id(0)
    t0 = i * tq
    qrow1 = t0 + lax.broadcasted_iota(jnp.int32, (1, tq), 1)
    qrow = jnp.concatenate([qrow1] * GROUP, axis=1)
    cur_row = qrow1 // BLK
    blkc = lax.broadcasted_iota(jnp.int32, (nbc, 1), 0)
    blkcf = blkc.astype(F32)
    pieces, masks, flags = [], [], []
    for hk in range(KV_HEADS):
        hl = slice(hk * KV_LANES, (hk + 1) * KV_LANES)
        qh = _head_rows(q_ref, hk)
        slope_row = jnp.concatenate([jnp.full((1, tq), 2.0 ** (-(hk * GROUP + g + 1)), F32) for g in range(GROUP)],
                                    axis=1)
        kcb = kc_ref[:, hl].astype(BF16)
        cposc = blkc * BLK + (BLK - 1)
        s_t = _dot_nt(kcb, qh) - slope_row * (qrow - cposc).astype(F32)
        s_t = jnp.where(cposc <= qrow, s_t, NEG)
        m_c = jnp.max(s_t, axis=0, keepdims=True)
        m_c = jnp.where(m_c > M_FLOOR, m_c, 0.0)
        e_c = jnp.exp(s_t - m_c)
        p_t = e_c / jnp.maximum(jnp.sum(e_c, axis=0, keepdims=True), 1e-30)
        o_c = _dot(kct_ref[hl, :].astype(BF16), p_t.astype(BF16))[HEAD_DIM:, :].T
        score = p_t[:, 0:tq]
        for g in range(1, GROUP):
            score = score + p_t[:, g * tq:(g + 1) * tq]
        forced = (blkc == 0) | (blkc == cur_row) | (blkc == cur_row - 1)
        sc = jnp.where(forced, FORCE_SCORE, score)
        sc = jnp.where(blkc > cur_row, -1.0, sc)
        sel_t = _topk_mask(sc, blkcf, min(N_SEL, nbc), axis=0)
        sel_q = jnp.where(blkc <= cur_row, sel_t, 0.0).T
        masks.append(1.0 - sel_q)
        flags.append(jnp.max(sel_q, axis=0, keepdims=True))
        pieces += [o_c[g * tq:(g + 1) * tq] for g in range(GROUP)]
    oc_ref[...] = jnp.concatenate(pieces, axis=-1)
    ns_ref[...] = jnp.concatenate(masks, axis=-1).astype(BF16)
    flag_ref[0] = jnp.concatenate(flags, axis=0)


def _nsa_select(q, kvc, tq):
    t = q.shape[0]
    nbc = kvc.shape[0]
    kct = kvc.T
    return pl.pallas_call(
        functools.partial(_nsa_select_body, tq=tq, nbc=nbc),
        grid=(t // tq,),
        in_specs=[pl.BlockSpec((tq, C_B), lambda i: (i, 0)), _const_spec(kvc.shape), _const_spec(kct.shape)],
        out_specs=[pl.BlockSpec((tq, C_B), lambda i: (i, 0)), pl.BlockSpec((tq, KV_HEADS * nbc), lambda i: (i, 0)),
                   pl.BlockSpec((1, KV_HEADS, nbc), lambda i: (i, 0, 0))],
        out_shape=[jax.ShapeDtypeStruct((t, C_B), F32), jax.ShapeDtypeStruct((t, KV_HEADS * nbc), BF16),
                   jax.ShapeDtypeStruct((t // tq, KV_HEADS, nbc), F32)],
        compiler_params=pltpu.CompilerParams(dimension_semantics=("arbitrary",)),
        name="nsa_select",
    )(q, kvc, kct)


def _nsa_prompt_body(cnt_ref, lst_ref, q_ref, gt_ref, oc_ref, ns_ref, ka_ref, vat_ref, kw_ref, em_ref, o_ref,
                     *, tq, tk, nbc, max_tiles):
    i = pl.program_id(0)
    t0 = i * tq
    rows = GROUP * tq
    qpos1 = t0 + lax.broadcasted_iota(jnp.int32, (tq, 1), 0)
    qpos = jnp.concatenate([qpos1] * GROUP, axis=0)
    qposf = qpos.astype(F32)
    qrow = jnp.concatenate([t0 + lax.broadcasted_iota(jnp.int32, (1, tq), 1)] * GROUP, axis=1)
    blk_row = lax.broadcasted_iota(jnp.int32, (1, nbc), 1)
    lane = lax.broadcasted_iota(jnp.int32, (1, KV_LANES), 1)
    gt = jax.nn.sigmoid(gt_ref[...])
    bpt = tk // BLK
    t0f = t0.astype(F32)
    o_ws, not_sels, q_augs = [], [], []
    for hk in range(KV_HEADS):
        hl = slice(hk * KV_LANES, (hk + 1) * KV_LANES)
        qh = _head_rows(q_ref, hk)
        slope = _slope_col(hk, tq)
        not_sels.append(ns_ref[:, hk * nbc:(hk + 1) * nbc])

        wl = WINDOW + tq
        ws = pl.multiple_of(jnp.maximum(t0 - WINDOW, 0), tq)
        kvw = kw_ref[pl.ds(ws, wl), hl]
        kposw = ws + lax.broadcasted_iota(jnp.int32, (1, wl), 1)
        s = _dot_nt(qh, kvw) - slope * (qposf - kposw.astype(F32))
        p_w = _masked_softmax(s, (kposw <= qpos) & (qpos - kposw < WINDOW))
        o_ws.append(_dot(p_w.astype(BF16), kvw)[:, HEAD_DIM:])

        q_augs.append(qh.astype(F32) + jnp.where(lane == AUG_HI, slope * POS_SPLIT, 0.0)
                      + jnp.where(lane == AUG_LO, slope, 0.0) + jnp.where(lane == AUG_ONE, -slope * t0f, 0.0))

    def sel_step(tiles, active, carry, causal):
        out = []
        for hk in range(KV_HEADS):
            m, acc = carry[hk]
            j = tiles[hk]
            k0 = pl.multiple_of(j * tk, tk)
            ns = not_sels[hk] if active[hk] is None else jnp.where(active[hk], not_sels[hk], jnp.ones_like(not_sels[hk]))
            mk = _dot(jnp.where(blk_row // bpt == j, ns, jnp.zeros_like(ns)), em_ref[...])
            q_j = (q_augs[hk] + jnp.concatenate([mk] * GROUP, axis=0)).astype(BF16)
            s = _dot_nt(ka_ref[hk, pl.ds(k0, tk), :], q_j)
            if causal:
                kpos = k0 + lax.broadcasted_iota(jnp.int32, (tk, 1), 0)
                s = jnp.where(kpos <= qrow, s, NEG)
            m_new = jnp.maximum(m, jnp.max(s, axis=0, keepdims=True))
            p = jnp.exp(s - m_new).astype(BF16)
            out.append((m_new, jnp.exp(m - m_new) * acc + _dot(vat_ref[hk, j], p)))
        return tuple(out)

    cnts = [cnt_ref[i * KV_HEADS + hk] for hk in range(KV_HEADS)]

    def skip_step(n, carry):
        active = [n < cnts[hk] - 1 for hk in range(KV_HEADS)]
        tiles = [jnp.where(active[hk], lst_ref[(i * KV_HEADS + hk) * max_tiles + n], 0) for hk in range(KV_HEADS)]
        return sel_step(tiles, active, carry, False)

    init = tuple((jnp.full((1, rows), M_FLOOR, F32), jnp.zeros((KV_LANES, rows), F32)) for _ in range(KV_HEADS))
    n_steps = functools.reduce(jnp.maximum, cnts) - 1
    carry = lax.fori_loop(0, n_steps, skip_step, init)
    last = (t0 + tq + tk - 1) // tk - 1
    carry = sel_step([last] * KV_HEADS, [None] * KV_HEADS, carry, True)

    pieces = []
    for hk in range(KV_HEADS):
        acc = carry[hk][1]
        o_s = (acc[:HEAD_DIM] / jnp.maximum(acc[HEAD_DIM:HEAD_DIM + 1], 1e-30)).T
        for g in range(GROUP):
            c0 = (hk * GROUP + g) * 3
            cs = slice((hk * GROUP + g) * HEAD_DIM, (hk * GROUP + g + 1) * HEAD_DIM)
            rs = slice(g * tq, (g + 1) * tq)
            pieces.append(gt[:, c0:c0 + 1] * oc_ref[:, cs] + gt[:, c0 + 1:c0 + 2] * o_s[rs]
                          + gt[:, c0 + 2:c0 + 3] * o_ws[hk][rs])
    o_ref[...] = jnp.concatenate(pieces, axis=-1)


def _nsa_prompt(q, gates, kvc, ks, kw, tq, tk):
    t = q.shape[0]
    nbc = kvc.shape[0]
    bpt = tk // BLK
    max_tiles = nbc // bpt
    assert t <= POS_SPLIT * 256 and AUG_BLK + bpt <= KV_LANES and t % tk == 0
    oc, not_sel, blk_any = _nsa_select(q, kvc, tq)
    tile_any = jnp.max(blk_any.reshape(t // tq, KV_HEADS, max_tiles, bpt), axis=-1) > 0.0
    rank = jnp.cumsum(tile_any.astype(jnp.int32), axis=-1) - 1
    slot = jnp.arange(max_tiles, dtype=jnp.int32)
    hit = tile_any[..., :, None] & (rank[..., :, None] == slot)
    lst = jnp.sum(jnp.where(hit, slot[:, None], 0), axis=-2).astype(jnp.int32).reshape(-1)
    cnt = jnp.sum(tile_any, axis=-1).astype(jnp.int32).reshape(-1)
    em = jnp.where(jnp.arange(KV_LANES)[None, :] == AUG_BLK + jnp.arange(nbc)[:, None] % bpt, NEG, 0.0).astype(BF16)
    pos = jnp.arange(t, dtype=jnp.int32)[:, None]
    aug_lane = jnp.arange(HEAD_DIM, dtype=jnp.int32)[None, :] + HEAD_DIM
    k_aug = (jnp.where(aug_lane == AUG_HI, pos // POS_SPLIT, 0) + jnp.where(aug_lane == AUG_LO, pos % POS_SPLIT, 0)
             + jnp.where(aug_lane == AUG_ONE, 1, 0)
             + jnp.where(aug_lane == AUG_BLK + (pos // BLK) % bpt, 1, 0)).astype(BF16)
    v_aug = jnp.broadcast_to(jnp.where(aug_lane == HEAD_DIM, 1, 0).astype(BF16), (t, HEAD_DIM))
    ka = jnp.stack([jnp.concatenate([ks[:, h * KV_LANES:h * KV_LANES + HEAD_DIM], k_aug], axis=1)
                    for h in range(KV_HEADS)])
    va = jnp.stack([jnp.concatenate([ks[:, h * KV_LANES + HEAD_DIM:(h + 1) * KV_LANES], v_aug], axis=1)
                    for h in range(KV_HEADS)])
    va = jnp.transpose(va.reshape(KV_HEADS, t // tk, tk, KV_LANES), (0, 1, 3, 2))
    vmem = 3 * t * C_KV * 2 + 24 * GROUP * tq * max(tk, WINDOW + tq) * 4 + (8 << 20)
    tile = lambda w: pl.BlockSpec((tq, w), lambda i, c, l: (i, 0))
    const = lambda z: pl.BlockSpec(z.shape, lambda i, c, l: (0,) * z.ndim, pipeline_mode=pl.Buffered(1))
    grid_spec = pltpu.PrefetchScalarGridSpec(
        num_scalar_prefetch=2,
        grid=(t // tq,),
        in_specs=[tile(C_B), tile(GATE_PAD), tile(C_B), tile(KV_HEADS * nbc), const(ka), const(va), const(kw), const(em)],
        out_specs=tile(C_B),
    )
    return pl.pallas_call(
        functools.partial(_nsa_prompt_body, tq=tq, tk=tk, nbc=nbc, max_tiles=max_tiles),
        grid_spec=grid_spec,
        out_shape=jax.ShapeDtypeStruct((t, C_B), F32),
        compiler_params=pltpu.CompilerParams(dimension_semantics=("arbitrary",), vmem_limit_bytes=_vmem_limit(vmem)),
        name="nsa_prompt",
    )(cnt, lst, q, gates, oc, not_sel, ka, va, kw, em)


PAGE_BLOCKS = 2
Q_ROWS = SUBLANES


def _samp_cmp_body(pt_ref, q_ref, pool_ref, oc_ref, sc_ref, kbuf, sems, *, n_pages, past, db):
    n = pl.program_id(0)

    slot = n % 2

    def page_copy(sample, buf, jp):
        return pltpu.make_async_copy(pool_ref.at[pl.ds(pt_ref[sample, jp], 1), :], kbuf.at[buf, pl.ds(jp, 1), :],
                                     sems.at[buf])

    def start_all(sample, buf):
        def start(jp, c):
            page_copy(sample, buf, jp).start()
            return c
        lax.fori_loop(0, n_pages, start, 0)

    def wait(jp, c):
        page_copy(n, slot, jp).wait()
        return c

    @pl.when(n == 0)
    def _():
        start_all(0, 0)

    @pl.when(n + 1 < db)
    def _():
        start_all(n + 1, 1 - slot)

    lax.fori_loop(0, n_pages, wait, 0)

    nb_past = n_pages * PAGE_BLOCKS
    n_all = nb_past + 1
    qpos = past
    cur = qpos // BLK
    width = PAGE_BLOCKS * n_pages
    lane = lax.broadcasted_iota(jnp.int32, (1, width), 1)
    bid = jnp.where(lane < n_pages, PAGE_BLOCKS * lane, PAGE_BLOCKS * (lane - n_pages) + 1)
    cpos = bid * BLK + (BLK - 1)
    bid2 = jnp.concatenate([bid, nb_past + lane], axis=-1)
    scores = []
    for hk in range(KV_HEADS):
        q8 = _head_rows(q_ref.at[0], hk, pad_rows=Q_ROWS - GROUP, pad_lanes=False)
        z8 = jnp.zeros_like(q8)
        slope = _slope_col(hk, 1, pad_rows=Q_ROWS - GROUP)
        kmat = kbuf[slot, :, hk * C_KV:hk * C_KV + KV_LANES].astype(BF16)
        vmat = kbuf[slot, :, hk * C_KV + KV_LANES:(hk + 1) * C_KV].astype(BF16)
        s = jnp.concatenate([_dot_nt(jnp.concatenate([q8, z8], axis=-1), kmat),
                             _dot_nt(jnp.concatenate([z8, q8], axis=-1), kmat)], axis=-1)
        s = s - slope * (qpos - cpos).astype(F32)
        p_c = _masked_softmax(s, cpos <= qpos)
        o_even = _dot(p_c[:, :n_pages].astype(BF16), vmat)
        o_odd = _dot(p_c[:, n_pages:].astype(BF16), vmat)
        oc_ref[0, hk] = o_even[:, :HEAD_DIM] + o_odd[:, HEAD_DIM:]
        score = jnp.sum(p_c[0:GROUP], axis=0, keepdims=True)
        sc = jnp.concatenate([score, jnp.zeros((1, width), F32)], axis=-1)
        forced = (bid2 == 0) | (bid2 == cur) | (bid2 == cur - 1)
        sc = jnp.where(forced, FORCE_SCORE, sc)
        sc = jnp.where(bid2 > cur, -1.0, sc)
        scores.append(jnp.where(bid2 < n_all, sc, -jnp.inf))
    sc_ref[0] = jnp.concatenate(scores, axis=0)


def _samp_cmp(page_table, q3, pool2, past):
    db, n_pages = page_table.shape
    wide = 2 * PAGE_BLOCKS * n_pages
    grid_spec = pltpu.PrefetchScalarGridSpec(
        num_scalar_prefetch=1,
        grid=(db,),
        in_specs=[pl.BlockSpec((1, 1, C_B), lambda n, pt: (n, 0, 0)), pl.BlockSpec(memory_space=pl.ANY)],
        out_specs=[pl.BlockSpec((1, KV_HEADS, Q_ROWS, HEAD_DIM), lambda n, pt: (n, 0, 0, 0)),
                   pl.BlockSpec((1, KV_HEADS, wide), lambda n, pt: (n, 0, 0))],
        scratch_shapes=[pltpu.VMEM((2, n_pages, KV_HEADS * C_KV), F32), pltpu.SemaphoreType.DMA((2,))],
    )
    return pl.pallas_call(
        functools.partial(_samp_cmp_body, n_pages=n_pages, past=past, db=db),
        grid_spec=grid_spec,
        out_shape=[jax.ShapeDtypeStruct((db, KV_HEADS, Q_ROWS, HEAD_DIM), F32),
                   jax.ShapeDtypeStruct((db, KV_HEADS, wide), F32)],
        compiler_params=pltpu.CompilerParams(dimension_semantics=("arbitrary",)),
        name="nsa_sample_cmp",
    )(page_table, q3, pool2)


def _samp_topk_body(sc_ref, idx_ref, *, n_pages, nb_past):
    s_t = sc_ref[...].T
    npos, nrow = s_t.shape
    width = PAGE_BLOCKS * n_pages
    pos = lax.broadcasted_iota(jnp.int32, (npos, 1), 0)
    ids = jnp.where(pos < n_pages, PAGE_BLOCKS * pos,
                    jnp.where(pos < width, PAGE_BLOCKS * (pos - n_pages) + 1, nb_past + pos - width)).astype(F32)
    picks = []
    for _ in range(N_SEL):
        m = jnp.max(s_t, axis=0, keepdims=True)
        pick = jnp.min(jnp.where(s_t == m, ids, 1e9), axis=0, keepdims=True)
        picks.append(pick)
        s_t = jnp.where(ids == pick, -jnp.inf, s_t)
    res = jnp.concatenate(picks + [jnp.zeros((LANES - N_SEL, nrow), F32)], axis=0)
    idx_ref[...] = res.T.astype(jnp.int32)


def _samp_topk(scores2, n_pages):
    rows = scores2.shape[0]
    return pl.pallas_call(
        functools.partial(_samp_topk_body, n_pages=n_pages, nb_past=n_pages * PAGE_BLOCKS),
        out_shape=jax.ShapeDtypeStruct((rows, LANES), jnp.int32),
        name="nsa_sample_topk",
    )(scores2)


def _samp_sel_body(pt_ref, idx_ref, q_ref, gt_ref, oc_ref, ksc_ref, kwr_ref, kwc_ref, win_ref, cache_ref,
                   o_ref, wino_ref, sbuf, sems, *, nb_past, past, wb, db):
    n = pl.program_id(0)
    qpos = past

    page_tok = PAGE_BLOCKS * BLK
    tok = lax.broadcasted_iota(jnp.int32, (HEAD_DIM, page_tok), 1)

    buf = n % 2

    def page_copy(sample, b, hk, s, ib):
        page = pt_ref[sample, ib // PAGE_BLOCKS]
        return pltpu.make_async_copy(cache_ref.at[page, hk], sbuf.at[b, hk, :, :, pl.ds(s * page_tok, page_tok)],
                                     sems.at[b, hk * N_SEL + s])

    def start_pages(sample, b):
        for hk in range(KV_HEADS):
            for s in range(N_SEL):
                ib = idx_ref[sample, hk * N_SEL + s]

                @pl.when(ib < nb_past)
                def _():
                    page_copy(sample, b, hk, s, ib).start()

    @pl.when(n == 0)
    def _():
        start_pages(0, 0)

    @pl.when(n + 1 < db)
    def _():
        start_pages(n + 1, 1 - buf)

    for hk in range(KV_HEADS):
        for s in range(N_SEL):
            ib = idx_ref[n, hk * N_SEL + s]

            @pl.when(ib >= nb_past)
            def _():
                for c in range(2):
                    sbuf[buf, hk, c, :, s * page_tok:(s + 1) * page_tok] = jnp.where(tok == 0, ksc_ref[0, 2 * hk + c],
                                                                                    0.0)

    wtok = lax.broadcasted_iota(jnp.int32, (HEAD_DIM, wb), 1)
    for hk in range(KV_HEADS):
        for c in range(2):
            wino_ref[0, hk, c] = jnp.where(wtok == wb - 1, kwc_ref[0, 2 * hk + c],
                                           pltpu.roll(win_ref[0, hk, c], wb - 1, 1))

    for hk in range(KV_HEADS):
        for s in range(N_SEL):
            ib = idx_ref[n, hk * N_SEL + s]

            @pl.when(ib < nb_past)
            def _():
                page_copy(n, buf, hk, s, ib).wait()

    gt = jax.nn.sigmoid(gt_ref[0])
    nk = N_SEL * page_tok
    lane = lax.broadcasted_iota(jnp.int32, (1, nk), 1)
    kposw = past - wb + lax.broadcasted_iota(jnp.int32, (1, wb), 1)
    wmask = (kposw <= qpos) & (qpos - kposw < WINDOW) & (kposw >= 0)
    pieces = []
    for hk in range(KV_HEADS):
        q8 = _head_rows(q_ref.at[0], hk, pad_rows=Q_ROWS - GROUP, pad_lanes=False)
        slope = _slope_col(hk, 1, pad_rows=Q_ROWS - GROUP)
        ibv = jnp.zeros((1, nk), jnp.int32)
        for s in range(N_SEL):
            ibv = jnp.where(lane // page_tok == s, idx_ref[n, hk * N_SEL + s], ibv)
        spos = (ibv - ibv % PAGE_BLOCKS) * BLK + lane % page_tok
        s_s = _dot(q8, sbuf[buf, hk, 0].astype(BF16)) - slope * (qpos - spos).astype(F32)
        p_s = _masked_softmax(s_s, (spos // BLK == ibv) & (spos <= qpos))
        o_s = _dot_nt(p_s.astype(BF16), sbuf[buf, hk, 1].astype(BF16))
        kwn = kwr_ref[0]
        k_new = kwn[:, hk * KV_LANES:hk * KV_LANES + HEAD_DIM]
        v_new = kwn[:, hk * KV_LANES + HEAD_DIM:(hk + 1) * KV_LANES]
        s_w = _dot(q8, win_ref[0, hk, 0].astype(BF16)) - slope * (qpos - kposw).astype(F32)
        s_w = jnp.where(wmask, s_w, NEG)
        s_n = jnp.sum(q8.astype(F32) * k_new, axis=-1, keepdims=True)
        m_w = jnp.maximum(jnp.max(s_w, axis=-1, keepdims=True), s_n)
        e_w = jnp.exp(s_w - m_w)
        e_n = jnp.exp(s_n - m_w)
        den = jnp.sum(e_w, axis=-1, keepdims=True) + e_n
        o_w = (_dot_nt(e_w.astype(BF16), win_ref[0, hk, 1].astype(BF16)) + e_n * v_new) / den
        o_c = oc_ref[0, hk]
        for g in range(GROUP):
            c0 = (hk * GROUP + g) * 3
            pieces.append(gt[:, c0:c0 + 1] * o_c[g:g + 1] + gt[:, c0 + 1:c0 + 2] * o_s[g:g + 1]
                          + gt[:, c0 + 2:c0 + 3] * o_w[g:g + 1])
    o_ref[0] = jnp.concatenate(pieces, axis=-1)


def _samp_sel(page_table, idx, q3, gates3, oc, ks_cols, kw_row, kw_cols, win_t, cache_t, past):
    db, n_pages = page_table.shape
    wb = win_t.shape[-1]
    nb_past = n_pages * PAGE_BLOCKS
    row3 = lambda w: pl.BlockSpec((1, 1, w), lambda n, pt, ix: (n, 0, 0))
    col4 = pl.BlockSpec((1, 2 * KV_HEADS, HEAD_DIM, 1), lambda n, pt, ix: (n, 0, 0, 0))
    win_spec = pl.BlockSpec((1, KV_HEADS, 2, HEAD_DIM, wb), lambda n, pt, ix: (n, 0, 0, 0, 0))
    grid_spec = pltpu.PrefetchScalarGridSpec(
        num_scalar_prefetch=2,
        grid=(db,),
        in_specs=[row3(C_B), row3(GATE_PAD),
                  pl.BlockSpec((1, KV_HEADS, Q_ROWS, HEAD_DIM), lambda n, pt, ix: (n, 0, 0, 0)),
                  col4, row3(C_KV), col4, win_spec, pl.BlockSpec(memory_space=pl.ANY)],
        out_specs=[row3(C_B), win_spec],
        scratch_shapes=[pltpu.VMEM((2, KV_HEADS, 2, HEAD_DIM, N_SEL * PAGE_BLOCKS * BLK), F32),
                        pltpu.SemaphoreType.DMA((2, KV_HEADS * N_SEL))],
    )
    return pl.pallas_call(
        functools.partial(_samp_sel_body, nb_past=nb_past, past=past, wb=wb, db=db),
        grid_spec=grid_spec,
        out_shape=[jax.ShapeDtypeStruct((db, 1, C_B), F32), jax.ShapeDtypeStruct(win_t.shape, F32)],
        compiler_params=pltpu.CompilerParams(dimension_semantics=("arbitrary",)),
        name="nsa_sample_sel",
    )(page_table, idx, q3, gates3, oc, ks_cols, kw_row, kw_cols, win_t, cache_t)


FF_CHUNK = 256


def _merge_body(x_ref, y_ref, bonus_ref, g_ref, yb_ref, mg_ref, p_ref, cbuf_ref,
                lng_ref, lnb_ref, bd_ref, woa_ref, wob_ref, wout_ref, n2_ref, wup_ref, cw_ref, cb_ref, wdn_ref,
                n3_ref, wpe_ref, wpg_ref, fg_ref, o_ref, cnew_ref, carry_ref, *, seq_mode, final, d_ff, tm):
    i = pl.program_id(0)
    d_model = x_ref.shape[1]
    y = y_ref[...]
    inv = 1.0 / A_HEAD_DIM
    mean = _dot_ones(y, bd_ref[...]) * inv
    d = y - mean
    var = _dot_ones(d * d, bd_ref[...]) * inv
    ya = (d * lax.rsqrt(var + LNX_EPS) * lng_ref[...] + lnb_ref[...] + bonus_ref[...]) * g_ref[...]
    mg = mg_ref[...]
    m = (jax.nn.sigmoid(mg[:, :d_model]) * _dot(ya.astype(BF16), woa_ref[...])
         + jax.nn.sigmoid(mg[:, d_model:]) * _dot(yb_ref[...].astype(BF16), wob_ref[...]))
    h = x_ref[...] + _dot(m.astype(BF16), wout_ref[...])
    xn = _rms(h, n2_ref[...]).astype(BF16)

    if seq_mode:
        @pl.when(i == 0)
        def _():
            carry_ref[...] = jnp.zeros_like(carry_ref)
        rid = lax.broadcasted_iota(jnp.int32, (tm, FF_CHUNK), 0)

    acc = jnp.zeros((tm, d_model), F32)
    for c in range(d_ff // FF_CHUNK):
        parts = []
        for half in range(2):
            cs = slice(half * d_ff + c * FF_CHUNK, half * d_ff + (c + 1) * FF_CHUNK)
            up = _dot(xn, wup_ref[:, cs])
            if seq_mode:
                t1 = carry_ref[SUBLANES - 1:SUBLANES, cs]
                t2 = carry_ref[SUBLANES - 2:SUBLANES - 1, cs]
                up1 = jnp.where(rid == 0, t1, pltpu.roll(up, 1, 0))
                up2 = jnp.where(rid == 0, t2, jnp.where(rid == 1, t1, pltpu.roll(up, 2, 0)))
                carry_ref[:, cs] = up[tm - SUBLANES:, :]
            else:
                up2 = cbuf_ref[:, cs]
                up1 = cbuf_ref[:, 2 * d_ff + cs.start:2 * d_ff + cs.stop]
                cnew_ref[:, cs] = up1
                cnew_ref[:, 2 * d_ff + cs.start:2 * d_ff + cs.stop] = up
            parts.append(cb_ref[:, cs] + cw_ref[0:1, cs] * up2 + cw_ref[1:2, cs] * up1 + cw_ref[2:3, cs] * up)
        a, gate = parts
        act = (a * jax.nn.sigmoid(a) * gate).astype(BF16)
        acc = acc + _dot(act, wdn_ref[c * FF_CHUNK:(c + 1) * FF_CHUNK, :])
    if seq_mode:
        cnew_ref[...] = carry_ref[...]
    h = h + acc
    pe = _dot(p_ref[...].astype(BF16), wpe_ref[...])
    h = h + pe * jax.nn.sigmoid(_dot(_rms(h, n3_ref[...]).astype(BF16), wpg_ref[...]))
    o_ref[...] = _rms(h, fg_ref[...]) if final else h


def _merge(x, y, bonus, g, yb, mg, p, cbuf, consts, tm, seq_mode, final):
    rows, d_model = x.shape
    d_ff = consts[10].shape[0]
    f2 = 2 * d_ff
    rowspec = lambda w: pl.BlockSpec((tm, w), lambda i: (i, 0))
    if seq_mode:
        cbuf_spec = _const_spec(cbuf.shape)
        cnew_shape, cnew_spec = (SUBLANES, f2), pl.BlockSpec((SUBLANES, f2), lambda i: (0, 0))
    else:
        cbuf_spec = rowspec(2 * f2)
        cnew_shape, cnew_spec = (rows, 2 * f2), rowspec(2 * f2)
    wbytes = sum(int(c.size) * c.dtype.itemsize for c in consts)
    act = tm * (d_model * 3 + C_A * 4 + 256) * 4 + (0 if seq_mode else 2 * tm * 2 * f2 * 4)
    vmem = wbytes + 2 * act + 8 * tm * d_model * 4 + (8 << 20)
    return pl.pallas_call(
        functools.partial(_merge_body, seq_mode=seq_mode, final=final, d_ff=d_ff, tm=tm),
        grid=(rows // tm,),
        in_specs=[rowspec(d_model), rowspec(C_A), rowspec(C_A), rowspec(C_A), rowspec(C_B), rowspec(2 * d_model),
                  rowspec(p.shape[1]), cbuf_spec] + [_const_spec(c.shape) for c in consts],
        out_specs=[rowspec(d_model), cnew_spec],
        out_shape=[jax.ShapeDtypeStruct((rows, d_model), F32), jax.ShapeDtypeStruct(cnew_shape, F32)],
        scratch_shapes=[pltpu.VMEM((SUBLANES, f2), F32)],
        compiler_params=pltpu.CompilerParams(dimension_semantics=("arbitrary",), vmem_limit_bytes=_vmem_limit(vmem)),
        name="merge_ffn",
    )(x, y, bonus, g, yb, mg, p, cbuf, *consts)


def _pick_tile(n, target):
    t = min(n, target)
    while n % t:
        t //= 2
    return t


def _head_block_diag():
    h = jnp.arange(C_A) // A_HEAD_DIM
    return (h[:, None] == h[None, :]).astype(BF16)


def _compress_weights(pe, w1, b1, w2):
    eye = jnp.eye(KV_HEADS, dtype=F32)
    eye_c = jnp.eye(2, dtype=F32)
    w_big = jnp.einsum('crdf,hg,ce->rhcdgef', w1, eye, eye_c).reshape(BLK * C_KV, KV_HEADS * 2 * D_CMP)
    w2_big = jnp.einsum('cfd,hg,ce->hcfged', w2, eye, eye_c).reshape(KV_HEADS * 2 * D_CMP, C_KV)
    pe_row = jnp.broadcast_to(jnp.transpose(pe, (1, 0, 2))[:, None], (BLK, KV_HEADS, 2, HEAD_DIM)).reshape(1, -1)
    b1_row = jnp.broadcast_to(b1[None], (KV_HEADS, 2, D_CMP)).reshape(1, -1)
    return w_big.astype(BF16), w2_big.astype(BF16), pe_row, b1_row


def kernel(x_prompt, x_sample, p_prompt, p_sample, cache_cmp_kv, cache_sel_kv, page_table, state_win_kv, state_wkv, state_shift, state_ffn_conv, norm1_g, w_in, shift_mu, rwkv_w0, rwkv_w2, rwkv_a0, rwkv_a2, rwkv_g2, rwkv_k_k, rwkv_k_a, rwkv_r_k, lnx_g, lnx_b, cmp_pe, cmp_w1, cmp_b1, cmp_w2, w_oa, w_ob, w_out, norm2_g, w_up, conv_w, conv_b, w_down, norm3_g, w_pe, w_pg, final_g):
    depth = w_in.shape[0]
    b, t, d_model = x_prompt.shape
    db, dt, _ = x_sample.shape
    n_pool, page = cache_cmp_kv.shape[1], cache_cmp_kv.shape[2]
    n_pages = page_table.shape[1]
    past = n_pages * page
    d_ff = w_down.shape[1]
    f2 = 2 * d_ff
    wb = state_win_kv.shape[2]
    assert b == 1 and dt == 1 and page == PAGE_BLOCKS * BLK
    assert t % 512 == 0 and t >= WINDOW + 256 and d_ff % FF_CHUNK == 0 and wb == WINDOW and past >= WINDOW
    assert N_SEL <= LANES and n_pages * PAGE_BLOCKS + 1 >= N_SEL

    bd = _head_block_diag()
    hp = x_prompt.reshape(t, d_model)
    hs = x_sample.reshape(db, d_model)
    outs = [[] for _ in range(12)]
    for i in range(depth):
        o_g = C_SHIFT + C_B + 3 * C_KV
        w_perm = jnp.concatenate(
            [w_in[i][:, :o_g], jnp.pad(w_in[i][:, o_g:o_g + 3 * Q_HEADS], ((0, 0), (0, GATE_PAD - 3 * Q_HEADS))),
             w_in[i][:, o_g + 3 * Q_HEADS:]], axis=1).astype(BF16)
        g1 = norm1_g[i].reshape(1, -1)
        rw = (shift_mu[i], rwkv_w0[i], rwkv_w2[i], rwkv_a0[i], rwkv_a2[i], rwkv_g2[i], rwkv_k_k[i], rwkv_k_a[i],
              rwkv_r_k[i].reshape(-1))
        w_big, w2_big, pe_row, b1_row = _compress_weights(cmp_pe[i], cmp_w1[i], cmp_b1[i], cmp_w2[i])
        zero_bias = jnp.zeros_like(b1_row)
        cmp_bias = _compress(jnp.broadcast_to(pe_row, (SUBLANES, pe_row.shape[1])), w_big, zero_bias, w2_big,
                             SUBLANES, 2048, mlp=False)[0:1] + b1_row
        eye_b = jnp.eye(PAGE_BLOCKS, dtype=F32)
        w_tok = jnp.einsum('crdf,eg->cdergf', cmp_w1[i], eye_b).reshape(2 * HEAD_DIM * PAGE_BLOCKS * BLK,
                                                                        PAGE_BLOCKS * D_CMP).astype(BF16)
        w2_pair = jnp.einsum('cfd,eg->cefgd', cmp_w2[i], eye_b).reshape(2, PAGE_BLOCKS * D_CMP,
                                                                       PAGE_BLOCKS * HEAD_DIM).astype(BF16)
        bias_pair = jnp.tile(cmp_bias[0, :2 * D_CMP].reshape(2, D_CMP), (1, PAGE_BLOCKS))
        row = lambda z: z.reshape(1, -1)
        mconsts = [row(lnx_g[i]), row(lnx_b[i]), bd, w_oa[i].astype(BF16), w_ob[i].astype(BF16),
                   w_out[i].astype(BF16), row(norm2_g[i]), w_up[i].astype(BF16), conv_w[i], row(conv_b[i]),
                   w_down[i].astype(BF16), row(norm3_g[i]), w_pe[i].astype(BF16), w_pg[i].astype(BF16), row(final_g)]
        final = i == depth - 1

        pa, q, kvc, kvs, kvs16, kvw, kvw16, gates, mg = _proj(hp, g1, w_perm, _pick_tile(t, 256))
        seqs = _rwkv_prep(pa, None, rw, bd, _pick_tile(t, 512), True)
        r_, w_, lw_, k_, v_, a_, b_, g_, bonus = seqs
        y, s_new = _rwkv_chunk_scan([z.reshape(1, t, C_A) for z in (r_, lw_, k_, v_, a_, b_)],
                                    jnp.zeros((1, A_HEADS, A_HEAD_DIM, A_HEAD_DIM), F32), _pick_tile(t, 512))
        kvc_blocks = _compress(kvc.reshape(t // BLK, BLK * C_KV), w_big, cmp_bias, w2_big,
                               _pick_tile(t // BLK, 256), 2048)
        yb = _nsa_prompt(q, gates, kvc_blocks, kvs16, kvw16, 256, 512)
        hp, conv_new = _merge(hp, y.reshape(t, C_A), bonus, g_, yb, mg, p_prompt[i].reshape(t, -1),
                              jnp.zeros((SUBLANES, LANES), F32), mconsts, _pick_tile(t, 256), True, final)
        kv6 = lambda z, n_: z.reshape(n_, -1, KV_HEADS, 2, HEAD_DIM)
        outs[0].append(kv6(kvc, 1))
        outs[2].append(kv6(kvs, 1))
        outs[4].append(kv6(kvw[t - min(WINDOW, t):], 1))
        outs[6].append(s_new)
        outs[8].append(pa[t - 1:t])
        outs[10].append(conv_new[SUBLANES - (CONV_W - 1):].reshape(1, CONV_W - 1, f2))

        pa, q, kvc, kvs, kvs16, kvw, kvw16, gates, mg = _proj(hs, g1, w_perm, _pick_tile(db, 128))
        seqs = _rwkv_prep(pa, state_shift[i], rw, bd, _pick_tile(db, 128), False)
        r_, w_, lw_, k_, v_, a_, b_, g_, bonus = seqs
        y, s_new = _rwkv_scan([z.reshape(db, 1, C_A) for z in (r_, w_, k_, v_, a_, b_)], state_wkv[i], 1)
        to_tok_minor = lambda z: jnp.transpose(z, (0, 2, 3, 4, 1))
        pool = _compress_pool(to_tok_minor(cache_cmp_kv[i]).reshape(n_pool * KV_HEADS, 2 * HEAD_DIM, page), w_tok, bias_pair,
                              w2_pair, _pick_tile(n_pool, 64))
        q3 = q.reshape(db, 1, C_B)
        oc, sel_scores = _samp_cmp(page_table, q3, pool.reshape(n_pool, KV_HEADS * C_KV), past)
        idx = _samp_topk(sel_scores.reshape(db * KV_HEADS, -1), n_pages)
        idx2 = idx[:, :N_SEL].reshape(db, KV_HEADS * N_SEL)
        cols = lambda z: z.reshape(db, 2 * KV_HEADS, HEAD_DIM, 1)
        yb, win_new_t = _samp_sel(page_table, idx2, q3, gates.reshape(db, 1, GATE_PAD), oc, cols(kvs),
                                  kvw.reshape(db, 1, C_KV), cols(kvw), to_tok_minor(state_win_kv[i]),
                                  to_tok_minor(cache_sel_kv[i]), past)
        win_new = jnp.transpose(win_new_t, (0, 4, 1, 2, 3))
        hs, conv_new = _merge(hs, y.reshape(db, C_A), bonus, g_, yb.reshape(db, C_B), mg, p_sample[i].reshape(db, -1),
                              state_ffn_conv[i].reshape(db, 2 * f2), mconsts, _pick_tile(db, 128), False, final)
        outs[1].append(kv6(kvc, db))
        outs[3].append(kv6(kvs, db))
        outs[5].append(win_new.reshape(db, wb, KV_HEADS, 2, HEAD_DIM))
        outs[7].append(s_new)
        outs[9].append(pa)
        outs[11].append(conv_new.reshape(db, CONV_W - 1, f2))

    stacked = [jnp.stack(o) for o in outs]
    return (hp.reshape(b, t, d_model), hs.reshape(db, dt, d_model), *stacked)
```

```python
import functools

import jax
import jax.numpy as jnp
from jax import lax
from jax.experimental import pallas as pl
from jax.experimental.pallas import tpu as pltpu

F32 = jnp.float32
BF16 = jnp.bfloat16
HI = lax.Precision.HIGHEST

A_HEADS = 8
A_HEAD_DIM = 64
C_A = A_HEADS * A_HEAD_DIM
R_W = 64
R_A = 64
R_G = 128
C_SHIFT = 3 * C_A + R_W + R_A + R_G
LNX_EPS = 64e-5
Q_HEADS = 8
KV_HEADS = 2
GROUP = Q_HEADS // KV_HEADS
HEAD_DIM = 64
C_B = Q_HEADS * HEAD_DIM
C_KV = KV_HEADS * 2 * HEAD_DIM
BLK = 64
N_SEL = 16
WINDOW = 512
D_CMP = 128
FORCE_SCORE = 1e4
CONV_W = 3
NORM_EPS = 1e-6

LANES = 128
SUBLANES = 8
VMEM_BYTES_V7X = 64 * 1024 * 1024

NEG = -1e30
M_FLOOR = -1e29
KV_LANES = 2 * HEAD_DIM

GATE_PAD = LANES


def _vmem_limit(nbytes):
    return int(min(max(nbytes, 16 * 1024 * 1024), VMEM_BYTES_V7X - 8 * 1024 * 1024))


def _const_spec(shape):
    nd = len(shape)
    return pl.BlockSpec(shape, lambda *_: (0,) * nd, pipeline_mode=pl.Buffered(1))


def _rms(x, g):
    return x * lax.rsqrt(jnp.mean(x * x, axis=-1, keepdims=True) + NORM_EPS) * g


def _dot(a, b, **kw):
    return jnp.dot(a, b, preferred_element_type=F32, **kw)


def _dot_ones(x, ones):
    hi = x.astype(BF16)
    lo = (x - hi.astype(F32)).astype(BF16)
    return _dot(hi, ones) + _dot(lo, ones)


def _dot_nt(a, b):
    return lax.dot_general(a, b, (((1,), (1,)), ((), ())), preferred_element_type=F32)


def _masked_softmax(s, mask):
    s = jnp.where(mask, s, NEG)
    m = jnp.max(s, axis=-1, keepdims=True)
    m = jnp.where(m > M_FLOOR, m, 0.0)
    e = jnp.exp(s - m)
    return e / jnp.maximum(jnp.sum(e, axis=-1, keepdims=True), 1e-30)


def _slope_col(hk, rows_per_head, pad_rows=0):
    cols = [jnp.full((rows_per_head, 1), 2.0 ** (-(hk * GROUP + g + 1)), F32) for g in range(GROUP)]
    if pad_rows:
        cols.append(jnp.zeros((pad_rows, 1), F32))
    return jnp.concatenate(cols, axis=0)


def _proj_body(x_ref, g_ref, w_ref, *o_refs, segs):
    xb = _rms(x_ref[...], g_ref[...]).astype(BF16)
    it = iter(o_refs)
    for off, width, outs in segs:
        r = _dot(xb, w_ref[:, off:off + width])
        for scale in outs:
            o_ref = next(it)
            o_ref[...] = (r * scale if scale != 1.0 else r).astype(o_ref.dtype)


def _proj(x, g, w_perm, tm):
    rows, d = x.shape
    d_model = d
    o = 0
    segs, shapes = [], []
    for width, outs in ((C_SHIFT, ((F32, 1.0),)), (C_B, ((BF16, HEAD_DIM ** -0.5),)), (C_KV, ((F32, 1.0),)),
                        (C_KV, ((F32, 1.0), (BF16, 1.0))), (C_KV, ((F32, 1.0), (BF16, 1.0))),
                        (GATE_PAD, ((F32, 1.0),)), (2 * d_model, ((F32, 1.0),))):
        segs.append((o, width, tuple(s for _, s in outs)))
        shapes += [(width, dt) for dt, _ in outs]
        o += width
    n_tot = o
    out_bytes = sum(tm * w * jnp.dtype(dt).itemsize for w, dt in shapes)
    vmem = 2 * tm * d * 4 + d * n_tot * 2 + 2 * out_bytes + (8 << 20)
    return pl.pallas_call(
        functools.partial(_proj_body, segs=tuple(segs)),
        grid=(rows // tm,),
        in_specs=[pl.BlockSpec((tm, d), lambda i: (i, 0)), _const_spec((1, d)), _const_spec((d, n_tot))],
        out_specs=[pl.BlockSpec((tm, w), lambda i: (i, 0)) for w, _ in shapes],
        out_shape=[jax.ShapeDtypeStruct((rows, w), dt) for w, dt in shapes],
        compiler_params=pltpu.CompilerParams(dimension_semantics=("arbitrary",), vmem_limit_bytes=_vmem_limit(vmem)),
        name="proj",
    )(x, g, w_perm)


def _prep_body(pa_ref, prev_ref, mu_ref, w0_ref, w2_ref, a0_ref, a2_ref, g2_ref, kk_ref, ka_ref, rk_ref, bd_ref,
               r_o, w_o, lw_o, k_o, v_o, a_o, b_o, g_o, bonus_o, *, seq_mode):
    pf = pa_ref[...]
    if seq_mode:
        i = pl.program_id(0)
        first = jnp.where(i > 0, prev_ref[SUBLANES - 1:SUBLANES, :], 0.0)
        rid = lax.broadcasted_iota(jnp.int32, pf.shape, 0)
        prev = jnp.where(rid == 0, first, pltpu.roll(pf, 1, 0))
    else:
        prev = prev_ref[...]
    xs = pf + (prev - pf) * mu_ref[...]
    r = xs[:, 0:C_A]
    k = xs[:, C_A:2 * C_A]
    v = xs[:, 2 * C_A:3 * C_A]
    o = 3 * C_A
    wd = xs[:, o:o + R_W]
    ad = xs[:, o + R_W:o + R_W + R_A]
    gd = xs[:, o + R_W + R_A:]
    nz = -(w0_ref[...] + _dot(jnp.tanh(wd), w2_ref[...], precision=HI))
    softplus = jnp.maximum(nz, 0.0) + jnp.log(1.0 + jnp.exp(-jnp.abs(nz)))
    w = -softplus - 0.5
    a = jax.nn.sigmoid(a0_ref[...] + _dot(ad, a2_ref[...], precision=HI))
    g = _dot(jax.nn.sigmoid(gd), g2_ref[...], precision=HI)
    kk = k * kk_ref[...]
    ss = _dot_ones(kk * kk, bd_ref[...])
    kkn = kk / jnp.maximum(jnp.sqrt(ss), 1e-12)
    k2 = k * (1.0 + (a - 1.0) * ka_ref[...])
    r_o[...] = r
    lw = -jnp.exp(w)
    lw_o[...] = lw
    w_o[...] = jnp.exp(lw)
    k_o[...] = k2
    v_o[...] = v
    a_o[...] = -kkn
    b_o[...] = kkn * a
    g_o[...] = g
    bonus_o[...] = _dot_ones(r * k2 * rk_ref[...], bd_ref[...]) * v


def _rwkv_prep(pa, prev, rw, bd, tm, seq_mode):
    rows = pa.shape[0]
    mu, w0, w2, a0, a2, g2, k_k, k_a, r_k = rw
    if seq_mode:
        tb = tm // SUBLANES
        prev_spec = pl.BlockSpec((SUBLANES, C_SHIFT), lambda i: (jnp.maximum(i * tb - 1, 0), 0))
        prev = pa
    else:
        prev_spec = pl.BlockSpec((tm, C_SHIFT), lambda i: (i, 0))
    row = lambda z: z.reshape(1, -1)
    consts = [row(mu), row(w0), w2, row(a0), a2, g2, row(k_k), row(k_a), row(r_k), bd]
    vmem = 4 * tm * C_SHIFT * 4 + 2 * 8 * tm * C_A * 4 + 16 * tm * C_A * 4 + (8 << 20)
    return pl.pallas_call(
        functools.partial(_prep_body, seq_mode=seq_mode),
        grid=(rows // tm,),
        in_specs=[pl.BlockSpec((tm, C_SHIFT), lambda i: (i, 0)), prev_spec] + [_const_spec(c.shape) for c in consts],
        out_specs=[pl.BlockSpec((tm, C_A), lambda i: (i, 0))] * 9,
        out_shape=[jax.ShapeDtypeStruct((rows, C_A), F32)] * 9,
        compiler_params=pltpu.CompilerParams(dimension_semantics=("arbitrary",), vmem_limit_bytes=_vmem_limit(vmem)),
        name="rwkv_prep",
    )(pa, prev, *consts)


N_PAIR = A_HEADS // 2


def _scan_body(r_ref, w_ref, k_ref, v_ref, a_ref, b_ref, s0_ref, y_ref, so_ref, st_ref, *, tc, nc, nb):
    c = pl.program_id(1)

    @pl.when(c == 0)
    def _():
        for q in range(nb):
            for p in range(N_PAIR):
                st_ref[q, p] = jnp.concatenate([s0_ref[q, 2 * p], s0_ref[q, 2 * p + 1]], axis=-1)

    shape = (A_HEAD_DIM, LANES)
    lane = lax.broadcasted_iota(jnp.int32, shape, 1)
    sub = lax.broadcasted_iota(jnp.int32, shape, 0)
    lo = lane < A_HEAD_DIM
    diag = (lane & (A_HEAD_DIM - 1)) == sub

    def seg_sum(x):
        s_lo = jnp.sum(jnp.where(lo, x, 0.0), axis=-1, keepdims=True)
        s_hi = jnp.sum(jnp.where(lo, 0.0, x), axis=-1, keepdims=True)
        return jnp.where(lo, s_lo, s_hi)

    grp = min(SUBLANES, tc)

    def token_group(gi, carry):
        base = pl.multiple_of(gi * grp, grp)
        for q in range(nb):
            for p in range(N_PAIR):
                sl = slice(LANES * p, LANES * (p + 1))
                rt, wt, kt, vt, at, bt = (ref[q, pl.ds(base, grp), sl]
                                          for ref in (r_ref, w_ref, k_ref, v_ref, a_ref, b_ref))
                s = st_ref[q, p]
                ys = []
                for j in range(grp):
                    row = lambda z: z[j:j + 1, :]
                    sa = seg_sum(s * row(at))
                    v_col = seg_sum(jnp.where(diag, row(vt), 0.0))
                    s = s * row(wt) + sa * row(bt) + v_col * row(kt)
                    y_col = seg_sum(s * row(rt))
                    ys.append(jnp.sum(jnp.where(diag, y_col, 0.0), axis=0, keepdims=True))
                st_ref[q, p] = s
                y_ref[q, pl.ds(base, grp), sl] = jnp.concatenate(ys, axis=0) if grp > 1 else ys[0]
        return carry

    lax.fori_loop(0, tc // grp, token_group, 0)

    @pl.when(c == nc - 1)
    def _():
        for q in range(nb):
            for p in range(N_PAIR):
                s = st_ref[q, p]
                so_ref[q, 2 * p] = s[:, :A_HEAD_DIM]
                so_ref[q, 2 * p + 1] = s[:, A_HEAD_DIM:]


def _rwkv_scan(seqs, s0, tc, nb):
    b, t, _ = seqs[0].shape
    nc = t // tc
    seq_spec = pl.BlockSpec((nb, tc, C_A), lambda i, c: (i, c, 0))
    st_spec = pl.BlockSpec((nb, A_HEADS, A_HEAD_DIM, A_HEAD_DIM), lambda i, c: (i, 0, 0, 0))
    return pl.pallas_call(
        functools.partial(_scan_body, tc=tc, nc=nc, nb=nb),
        grid=(b // nb, nc),
        in_specs=[seq_spec] * 6 + [st_spec],
        out_specs=[seq_spec, st_spec],
        out_shape=[jax.ShapeDtypeStruct((b, t, C_A), F32), jax.ShapeDtypeStruct(s0.shape, F32)],
        scratch_shapes=[pltpu.VMEM((nb, N_PAIR, A_HEAD_DIM, LANES), F32)],
        compiler_params=pltpu.CompilerParams(dimension_semantics=("arbitrary", "arbitrary")),
        name="rwkv_scan",
    )(*seqs, s0)


SCAN_CHUNK = 64
SCAN_CHUNKS_PER_STEP = 4


def _mm(a, b):
    return _dot(a.astype(BF16), b.astype(BF16))


def _chunk_scan_body(r_ref, lw_ref, k_ref, v_ref, a_ref, b_ref, s0_ref, y_ref, so_ref, st_ref, *, tc, nc):
    cidx = pl.program_id(1)
    c = SCAN_CHUNK
    d = A_HEAD_DIM

    @pl.when(cidx == 0)
    def _():
        for h in range(A_HEADS):
            st_ref[h] = s0_ref[0, h].T

    ri = lax.broadcasted_iota(jnp.int32, (c, c), 0)
    ci = lax.broadcasted_iota(jnp.int32, (c, c), 1)
    lower = ci <= ri
    strict = ci < ri
    eye = lax.broadcasted_iota(jnp.int32, (d, d), 0) == lax.broadcasted_iota(jnp.int32, (d, d), 1)
    n_double = (c - 1).bit_length()
    per = SCAN_CHUNKS_PER_STEP
    slab = per * c
    sr = lax.broadcasted_iota(jnp.int32, (slab, slab), 0)
    sc = lax.broadcasted_iota(jnp.int32, (slab, slab), 1)
    tri = ((sr // c == sc // c) & (sc <= sr)).astype(F32)

    def step_fn(step, carry):
        base = pl.multiple_of(step * slab, slab)
        r, lw, k, v, a, b = (ref[0, pl.ds(base, slab), :] for ref in (r_ref, lw_ref, k_ref, v_ref, a_ref, b_ref))
        cum = _dot(tri, lw, precision=HI)
        tots = [cum[(q + 1) * c - 1:(q + 1) * c, :] for q in range(per)]
        tot = jnp.concatenate([jnp.broadcast_to(t_, (c, C_A)) for t_ in tots], axis=0)
        e_inv = jnp.exp(-cum)
        e_rest = jnp.exp(tot - cum)
        at = a * jnp.exp(cum - lw)
        rt = r * jnp.exp(cum)
        bt = b * e_inv
        kt = k * e_inv
        bh = b * e_rest
        kh = k * e_rest
        g_tot = [jnp.exp(t_) for t_ in tots]
        units = [(q, h) for q in range(per) for h in range(A_HEADS)]
        cut = lambda z, u: z[u[0] * c:(u[0] + 1) * c, u[1] * d:(u[1] + 1) * d]
        g4 = [_dot_nt(jnp.concatenate([cut(at, u), cut(rt, u)], axis=0).astype(BF16),
                      jnp.concatenate([cut(bt, u), cut(kt, u)], axis=0).astype(BF16)) for u in units]
        lp = [jnp.where(strict, g[:c, :c], 0.0).astype(BF16) for g in g4]
        m_l = [jnp.where(strict, g[:c, c:], 0.0) for g in g4]
        p_b = [jnp.where(lower, g[c:, :c], 0.0) for g in g4]
        p_k = [jnp.where(lower, g[c:, c:], 0.0) for g in g4]
        vb = [cut(v, u).astype(BF16) for u in units]
        n_u = range(len(units))
        z = [jnp.concatenate([cut(at, units[i]), _mm(m_l[i], vb[i])], axis=-1) for i in n_u]
        for q in range(n_double):
            z = [z[i] + _mm(lp[i], z[i]) for i in n_u]
            if q < n_double - 1:
                lp = [_mm(lp[i], lp[i]).astype(BF16) for i in n_u]
        zb = [zz.astype(BF16) for zz in z]
        bz = [_mm(cut(bh, units[i]).T, zb[i]) for i in n_u]
        kv = [_mm(cut(kh, units[i]).T, vb[i]) for i in n_u]
        pz = [_mm(p_b[i], zb[i]) for i in n_u]
        pv = [_mm(p_k[i], vb[i]) for i in n_u]
        rows = []
        for q in range(per):
            ys = []
            for h in range(A_HEADS):
                i = q * A_HEADS + h
                st = st_ref[h]
                a_c = jnp.where(eye, g_tot[q][:, h * d:(h + 1) * d], 0.0) + bz[i][:, :d]
                ys.append(_dot(cut(rt, units[i]) + pz[i][:, :d], st, precision=HI) + pz[i][:, d:] + pv[i])
                st_ref[h] = _dot(a_c, st, precision=HI) + bz[i][:, d:] + kv[i]
            rows.append(jnp.concatenate(ys, axis=-1))
        y_ref[0, pl.ds(base, slab), :] = jnp.concatenate(rows, axis=0)
        return carry

    lax.fori_loop(0, tc // slab, step_fn, 0)

    @pl.when(cidx == nc - 1)
    def _():
        for h in range(A_HEADS):
            so_ref[0, h] = st_ref[h].T


def _rwkv_chunk_scan(seqs, s0, tc):
    b, t, _ = seqs[0].shape
    nc = t // tc
    seq_spec = pl.BlockSpec((1, tc, C_A), lambda i, c: (i, c, 0))
    st_spec = pl.BlockSpec((1, A_HEADS, A_HEAD_DIM, A_HEAD_DIM), lambda i, c: (i, 0, 0, 0))
    return pl.pallas_call(
        functools.partial(_chunk_scan_body, tc=tc, nc=nc),
        grid=(b, nc),
        in_specs=[seq_spec] * 6 + [st_spec],
        out_specs=[seq_spec, st_spec],
        out_shape=[jax.ShapeDtypeStruct((b, t, C_A), F32), jax.ShapeDtypeStruct(s0.shape, F32)],
        scratch_shapes=[pltpu.VMEM((A_HEADS, A_HEAD_DIM, A_HEAD_DIM), F32)],
        compiler_params=pltpu.CompilerParams(dimension_semantics=("arbitrary", "arbitrary")),
        name="rwkv_chunk_scan",
    )(*seqs, s0)


def _gelu_tanh(x):
    return 0.5 * x * (1.0 + jnp.tanh(0.7978845608028654 * (x + 0.044715 * (x * x * x))))


def _compress_body(x_ref, w_ref, b_ref, w2_ref, o_ref, acc_ref, *, nk, mlp):
    kk = pl.program_id(1)

    @pl.when(kk == 0)
    def _():
        acc_ref[...] = jnp.zeros_like(acc_ref)

    acc_ref[...] += _dot(x_ref[...].astype(BF16), w_ref[...])

    @pl.when(kk == nk - 1)
    def _():
        h = acc_ref[...] + b_ref[...]
        if mlp:
            o_ref[...] = _dot(_gelu_tanh(h).astype(BF16), w2_ref[...])
        else:
            o_ref[...] = h


def _compress(x2, w_big, bias, w2_big, tm, tk, mlp=True):
    m, kdim = x2.shape
    nh = w_big.shape[1]
    n_out = w2_big.shape[1] if mlp else nh
    nk = kdim // tk
    vmem = 2 * tm * tk * 4 + 2 * tk * nh * 2 + 3 * tm * nh * 4 + 2 * tm * n_out * 4 + (8 << 20)
    return pl.pallas_call(
        functools.partial(_compress_body, nk=nk, mlp=mlp),
        grid=(m // tm, nk),
        in_specs=[pl.BlockSpec((tm, tk), lambda i, k: (i, k)), pl.BlockSpec((tk, nh), lambda i, k: (k, 0)),
                  _const_spec(bias.shape), _const_spec(w2_big.shape)],
        out_specs=pl.BlockSpec((tm, n_out), lambda i, k: (i, 0)),
        out_shape=jax.ShapeDtypeStruct((m, n_out), F32),
        scratch_shapes=[pltpu.VMEM((tm, nh), F32)],
        compiler_params=pltpu.CompilerParams(dimension_semantics=("arbitrary", "arbitrary"),
                                             vmem_limit_bytes=_vmem_limit(vmem)),
        name="nsa_compress",
    )(x2, w_big, bias, w2_big)


PAGE_ROWS = KV_HEADS * 2 * HEAD_DIM


def _compress_pool_body(pt_ref, cache_ref, w_ref, b_ref, w2_ref, o_ref, xbuf, sems, *, pg, nseq):
    n = pl.program_id(0)
    slot = n % 2

    def page_copy(seq, buf, jp):
        return pltpu.make_async_copy(cache_ref.at[pt_ref[seq, jp]], xbuf.at[buf, pl.ds(jp * KV_HEADS, KV_HEADS)],
                                     sems.at[buf])

    def start_all(seq, buf):
        def start(jp, carry):
            page_copy(seq, buf, jp).start()
            return carry
        lax.fori_loop(0, pg, start, 0)

    def wait(jp, carry):
        page_copy(n, slot, jp).wait()
        return carry

    @pl.when(n == 0)
    def _():
        start_all(0, 0)

    @pl.when(n + 1 < nseq)
    def _():
        start_all(n + 1, 1 - slot)

    lax.fori_loop(0, pg, wait, 0)

    m = pg * KV_HEADS
    for c in range(2):
        xt = jnp.swapaxes(xbuf[slot, :, c * HEAD_DIM:(c + 1) * HEAD_DIM, :], 0, 1)
        acc = jnp.zeros((m, 2 * D_CMP), F32)
        for d in range(0, HEAD_DIM, 2):
            r0 = c * HEAD_DIM + d
            x_pair = jnp.concatenate([xt[d], xt[d + 1]], axis=-1).astype(BF16)
            acc = acc + _dot(x_pair, w_ref[r0 * 2 * BLK:(r0 + 2) * 2 * BLK, :])
        h = acc + b_ref[c:c + 1, :]
        o_ref[:, c * 2 * HEAD_DIM:(c + 1) * 2 * HEAD_DIM] = _dot(_gelu_tanh(h).astype(BF16), w2_ref[c])


def _compress_pool(page_table, cache_t, w_tok, bias2, w2_pair):
    nseq, pg = page_table.shape
    m = pg * KV_HEADS
    vmem = 3 * pg * PAGE_ROWS * 2 * BLK * 4 + int(w_tok.size) * 2 + 8 * m * 2 * D_CMP * 4 + (4 << 20)
    const = lambda z: pl.BlockSpec(z.shape, lambda i, pt: (0,) * z.ndim, pipeline_mode=pl.Buffered(1))
    grid_spec = pltpu.PrefetchScalarGridSpec(
        num_scalar_prefetch=1,
        grid=(nseq,),
        in_specs=[pl.BlockSpec(memory_space=pl.ANY), const(w_tok), const(bias2), const(w2_pair)],
        out_specs=pl.BlockSpec((m, C_KV), lambda i, pt: (i, 0)),
        scratch_shapes=[pltpu.VMEM((2, m, 2 * HEAD_DIM, 2 * BLK), F32), pltpu.SemaphoreType.DMA((2,))],
    )
    return pl.pallas_call(
        functools.partial(_compress_pool_body, pg=pg, nseq=nseq),
        grid_spec=grid_spec,
        out_shape=jax.ShapeDtypeStruct((nseq * m, C_KV), F32),
        compiler_params=pltpu.CompilerParams(dimension_semantics=("arbitrary",), vmem_limit_bytes=_vmem_limit(vmem)),
        name="nsa_compress_pool",
    )(page_table, cache_t, w_tok, bias2, w2_pair)


def _topk_mask(s, ids, k, axis):
    sel = jnp.zeros(s.shape, F32)
    for _ in range(k):
        m = jnp.max(s, axis=axis, keepdims=True)
        pick = jnp.min(jnp.where(s == m, ids, 1e9), axis=axis, keepdims=True)
        hit = ids == pick
        sel = jnp.where(hit, 1.0, sel)
        s = jnp.where(hit, -jnp.inf, s)
    return sel


def _head_rows(q_ref, hk, pad_rows=0, pad_lanes=True):
    parts = [q_ref[:, (hk * GROUP + g) * HEAD_DIM:(hk * GROUP + g + 1) * HEAD_DIM] for g in range(GROUP)]
    if pad_rows:
        parts.append(jnp.zeros((pad_rows, HEAD_DIM), parts[0].dtype))
    qh = jnp.concatenate(parts, axis=0)
    return jnp.concatenate([qh, jnp.zeros_like(qh)], axis=-1) if pad_lanes else qh


AUG_HI = HEAD_DIM
AUG_LO = HEAD_DIM + 1
AUG_ONE = HEAD_DIM + 2
AUG_BLK = HEAD_DIM + 3
POS_SPLIT = LANES


def _nsa_select_body(q_ref, kc_ref, kct_ref, oc_ref, ns_ref, flag_ref, *, tq, nbc):
    i = pl.program_id(0)
    t0 = i * tq
    qrow1 = t0 + lax.broadcasted_iota(jnp.int32, (1, tq), 1)
    qrow = jnp.concatenate([qrow1] * GROUP, axis=1)
    cur_row = qrow1 // BLK
    blkc = lax.broadcasted_iota(jnp.int32, (nbc, 1), 0)
    blkcf = blkc.astype(F32)
    pieces, masks, flags = [], [], []
    for hk in range(KV_HEADS):
        hl = slice(hk * KV_LANES, (hk + 1) * KV_LANES)
        qh = _head_rows(q_ref, hk)
        slope_row = jnp.concatenate([jnp.full((1, tq), 2.0 ** (-(hk * GROUP + g + 1)), F32) for g in range(GROUP)],
                                    axis=1)
        kcb = kc_ref[:, hl].astype(BF16)
        cposc = blkc * BLK + (BLK - 1)
        s_t = _dot_nt(kcb, qh) - slope_row * (qrow - cposc).astype(F32)
        s_t = jnp.where(cposc <= qrow, s_t, NEG)
        m_c = jnp.max(s_t, axis=0, keepdims=True)
        m_c = jnp.where(m_c > M_FLOOR, m_c, 0.0)
        e_c = jnp.exp(s_t - m_c)
        p_t = e_c / jnp.maximum(jnp.sum(e_c, axis=0, keepdims=True), 1e-30)
        o_c = _dot(kct_ref[hl, :].astype(BF16), p_t.astype(BF16))[HEAD_DIM:, :].T
        score = p_t[:, 0:tq]
        for g in range(1, GROUP):
            score = score + p_t[:, g * tq:(g + 1) * tq]
        forced = (blkc == 0) | (blkc == cur_row) | (blkc == cur_row - 1)
        sc = jnp.where(forced, FORCE_SCORE, score)
        sc = jnp.where(blkc > cur_row, -1.0, sc)
        sel_t = _topk_mask(sc, blkcf, min(N_SEL, nbc), axis=0)
        sel_q = jnp.where(blkc <= cur_row, sel_t, 0.0).T
        masks.append(1.0 - sel_q)
        flags.append(jnp.max(sel_q, axis=0, keepdims=True))
        pieces += [o_c[g * tq:(g + 1) * tq] for g in range(GROUP)]
    oc_ref[...] = jnp.concatenate(pieces, axis=-1)
    ns_ref[...] = jnp.concatenate(masks, axis=-1).astype(BF16)
    flag_ref[0] = jnp.concatenate(flags, axis=0)


def _nsa_select(q, kvc, tq):
    t = q.shape[0]
    nbc = kvc.shape[0]
    kct = kvc.T
    return pl.pallas_call(
        functools.partial(_nsa_select_body, tq=tq, nbc=nbc),
        grid=(t // tq,),
        in_specs=[pl.BlockSpec((tq, C_B), lambda i: (i, 0)), _const_spec(kvc.shape), _const_spec(kct.shape)],
        out_specs=[pl.BlockSpec((tq, C_B), lambda i: (i, 0)), pl.BlockSpec((tq, KV_HEADS * nbc), lambda i: (i, 0)),
                   pl.BlockSpec((1, KV_HEADS, nbc), lambda i: (i, 0, 0))],
        out_shape=[jax.ShapeDtypeStruct((t, C_B), F32), jax.ShapeDtypeStruct((t, KV_HEADS * nbc), BF16),
                   jax.ShapeDtypeStruct((t // tq, KV_HEADS, nbc), F32)],
        compiler_params=pltpu.CompilerParams(dimension_semantics=("arbitrary",)),
        name="nsa_select",
    )(q, kvc, kct)


def _nsa_prompt_body(cnt_ref, lst_ref, q_ref, gt_ref, oc_ref, ns_ref, ka_ref, vat_ref, kw_ref, em_ref, o_ref,
                     *, tq, tk, nbc, max_tiles):
    i = pl.program_id(0)
    t0 = i * tq
    rows = GROUP * tq
    qpos1 = t0 + lax.broadcasted_iota(jnp.int32, (tq, 1), 0)
    qpos = jnp.concatenate([qpos1] * GROUP, axis=0)
    qposf = qpos.astype(F32)
    qrow = jnp.concatenate([t0 + lax.broadcasted_iota(jnp.int32, (1, tq), 1)] * GROUP, axis=1)
    blk_row = lax.broadcasted_iota(jnp.int32, (1, nbc), 1)
    lane = lax.broadcasted_iota(jnp.int32, (1, KV_LANES), 1)
    gt = jax.nn.sigmoid(gt_ref[...])
    bpt = tk // BLK
    t0f = t0.astype(F32)
    o_ws, not_sels, q_augs = [], [], []
    for hk in range(KV_HEADS):
        hl = slice(hk * KV_LANES, (hk + 1) * KV_LANES)
        qh = _head_rows(q_ref, hk)
        slope = _slope_col(hk, tq)
        not_sels.append(ns_ref[:, hk * nbc:(hk + 1) * nbc])

        wl = WINDOW + tq
        ws = pl.multiple_of(jnp.maximum(t0 - WINDOW, 0), tq)
        kvw = kw_ref[pl.ds(ws, wl), hl]
        kposw = ws + lax.broadcasted_iota(jnp.int32, (1, wl), 1)
        s = _dot_nt(qh, kvw) - slope * (qposf - kposw.astype(F32))
        p_w = _masked_softmax(s, (kposw <= qpos) & (qpos - kposw < WINDOW))
        o_ws.append(_dot(p_w.astype(BF16), kvw)[:, HEAD_DIM:])

        q_augs.append(qh.astype(F32) + jnp.where(lane == AUG_HI, slope * POS_SPLIT, 0.0)
                      + jnp.where(lane == AUG_LO, slope, 0.0) + jnp.where(lane == AUG_ONE, -slope * t0f, 0.0))

    def sel_step(tiles, active, carry, causal):
        out = []
        for hk in range(KV_HEADS):
            m, acc = carry[hk]
            j = tiles[hk]
            k0 = pl.multiple_of(j * tk, tk)
            ns = not_sels[hk] if active[hk] is None else jnp.where(active[hk], not_sels[hk], jnp.ones_like(not_sels[hk]))
            mk = _dot(jnp.where(blk_row // bpt == j, ns, jnp.zeros_like(ns)), em_ref[...])
            q_j = (q_augs[hk] + jnp.concatenate([mk] * GROUP, axis=0)).astype(BF16)
            s = _dot_nt(ka_ref[hk, pl.ds(k0, tk), :], q_j)
            if causal:
                kpos = k0 + lax.broadcasted_iota(jnp.int32, (tk, 1), 0)
                s = jnp.where(kpos <= qrow, s, NEG)
            m_new = jnp.maximum(m, jnp.max(s, axis=0, keepdims=True))
            p = jnp.exp(s - m_new).astype(BF16)
            out.append((m_new, jnp.exp(m - m_new) * acc + _dot(vat_ref[hk, j], p)))
        return tuple(out)

    cnts = [cnt_ref[i * KV_HEADS + hk] for hk in range(KV_HEADS)]

    def skip_step(n, carry):
        active = [n < cnts[hk] - 1 for hk in range(KV_HEADS)]
        tiles = [jnp.where(active[hk], lst_ref[(i * KV_HEADS + hk) * max_tiles + n], 0) for hk in range(KV_HEADS)]
        return sel_step(tiles, active, carry, False)

    init = tuple((jnp.full((1, rows), M_FLOOR, F32), jnp.zeros((KV_LANES, rows), F32)) for _ in range(KV_HEADS))
    n_steps = functools.reduce(jnp.maximum, cnts) - 1
    carry = lax.fori_loop(0, n_steps, skip_step, init)
    last = (t0 + tq + tk - 1) // tk - 1
    carry = sel_step([last] * KV_HEADS, [None] * KV_HEADS, carry, True)

    pieces = []
    for hk in range(KV_HEADS):
        acc = carry[hk][1]
        o_s = (acc[:HEAD_DIM] / jnp.maximum(acc[HEAD_DIM:HEAD_DIM + 1], 1e-30)).T
        for g in range(GROUP):
            c0 = (hk * GROUP + g) * 3
            cs = slice((hk * GROUP + g) * HEAD_DIM, (hk * GROUP + g + 1) * HEAD_DIM)
            rs = slice(g * tq, (g + 1) * tq)
            pieces.append(gt[:, c0:c0 + 1] * oc_ref[:, cs] + gt[:, c0 + 1:c0 + 2] * o_s[rs]
                          + gt[:, c0 + 2:c0 + 3] * o_ws[hk][rs])
    o_ref[...] = jnp.concatenate(pieces, axis=-1)


def _nsa_prompt(q, gates, kvc, ks, kw, tq, tk):
    t = q.shape[0]
    nbc = kvc.shape[0]
    bpt = tk // BLK
    max_tiles = nbc // bpt
    assert t <= POS_SPLIT * 256 and AUG_BLK + bpt <= KV_LANES and t % tk == 0
    oc, not_sel, blk_any = _nsa_select(q, kvc, tq)
    tile_any = jnp.max(blk_any.reshape(t // tq, KV_HEADS, max_tiles, bpt), axis=-1) > 0.0
    rank = jnp.cumsum(tile_any.astype(jnp.int32), axis=-1) - 1
    slot = jnp.arange(max_tiles, dtype=jnp.int32)
    hit = tile_any[..., :, None] & (rank[..., :, None] == slot)
    lst = jnp.sum(jnp.where(hit, slot[:, None], 0), axis=-2).astype(jnp.int32).reshape(-1)
    cnt = jnp.sum(tile_any, axis=-1).astype(jnp.int32).reshape(-1)
    em = jnp.where(jnp.arange(KV_LANES)[None, :] == AUG_BLK + jnp.arange(nbc)[:, None] % bpt, NEG, 0.0).astype(BF16)
    pos = jnp.arange(t, dtype=jnp.int32)[:, None]
    aug_lane = jnp.arange(HEAD_DIM, dtype=jnp.int32)[None, :] + HEAD_DIM
    k_aug = (jnp.where(aug_lane == AUG_HI, pos // POS_SPLIT, 0) + jnp.where(aug_lane == AUG_LO, pos % POS_SPLIT, 0)
             + jnp.where(aug_lane == AUG_ONE, 1, 0)
             + jnp.where(aug_lane == AUG_BLK + (pos // BLK) % bpt, 1, 0)).astype(BF16)
    v_aug = jnp.broadcast_to(jnp.where(aug_lane == HEAD_DIM, 1, 0).astype(BF16), (t, HEAD_DIM))
    ka = jnp.stack([jnp.concatenate([ks[:, h * KV_LANES:h * KV_LANES + HEAD_DIM], k_aug], axis=1)
                    for h in range(KV_HEADS)])
    va = jnp.stack([jnp.concatenate([ks[:, h * KV_LANES + HEAD_DIM:(h + 1) * KV_LANES], v_aug], axis=1)
                    for h in range(KV_HEADS)])
    va = jnp.transpose(va.reshape(KV_HEADS, t // tk, tk, KV_LANES), (0, 1, 3, 2))
    vmem = 3 * t * C_KV * 2 + 24 * GROUP * tq * max(tk, WINDOW + tq) * 4 + (8 << 20)
    tile = lambda w: pl.BlockSpec((tq, w), lambda i, c, l: (i, 0))
    const = lambda z: pl.BlockSpec(z.shape, lambda i, c, l: (0,) * z.ndim, pipeline_mode=pl.Buffered(1))
    grid_spec = pltpu.PrefetchScalarGridSpec(
        num_scalar_prefetch=2,
        grid=(t // tq,),
        in_specs=[tile(C_B), tile(GATE_PAD), tile(C_B), tile(KV_HEADS * nbc), const(ka), const(va), const(kw), const(em)],
        out_specs=tile(C_B),
    )
    return pl.pallas_call(
        functools.partial(_nsa_prompt_body, tq=tq, tk=tk, nbc=nbc, max_tiles=max_tiles),
        grid_spec=grid_spec,
        out_shape=jax.ShapeDtypeStruct((t, C_B), F32),
        compiler_params=pltpu.CompilerParams(dimension_semantics=("arbitrary",), vmem_limit_bytes=_vmem_limit(vmem)),
        name="nsa_prompt",
    )(cnt, lst, q, gates, oc, not_sel, ka, va, kw, em)


PAGE_BLOCKS = 2
Q_ROWS = SUBLANES


def _samp_cmp_body(q_ref, kc_ref, oc_ref, sc_ref, *, n_pages, past):
    kbuf = kc_ref.at[0]
    nb_past = n_pages * PAGE_BLOCKS
    n_all = nb_past + 1
    qpos = past
    cur = qpos // BLK
    width = PAGE_BLOCKS * n_pages
    lane = lax.broadcasted_iota(jnp.int32, (1, width), 1)
    bid = jnp.where(lane < n_pages, PAGE_BLOCKS * lane, PAGE_BLOCKS * (lane - n_pages) + 1)
    cpos = bid * BLK + (BLK - 1)
    bid2 = jnp.concatenate([bid, nb_past + lane], axis=-1)
    scores = []
    for hk in range(KV_HEADS):
        q8 = _head_rows(q_ref.at[0], hk, pad_rows=Q_ROWS - GROUP, pad_lanes=False)
        z8 = jnp.zeros_like(q8)
        slope = _slope_col(hk, 1, pad_rows=Q_ROWS - GROUP)
        kmat = kbuf[:, hk * C_KV:hk * C_KV + KV_LANES].astype(BF16)
        vmat = kbuf[:, hk * C_KV + KV_LANES:(hk + 1) * C_KV].astype(BF16)
        s = jnp.concatenate([_dot_nt(jnp.concatenate([q8, z8], axis=-1), kmat),
                             _dot_nt(jnp.concatenate([z8, q8], axis=-1), kmat)], axis=-1)
        s = s - slope * (qpos - cpos).astype(F32)
        p_c = _masked_softmax(s, cpos <= qpos)
        o_even = _dot(p_c[:, :n_pages].astype(BF16), vmat)
        o_odd = _dot(p_c[:, n_pages:].astype(BF16), vmat)
        oc_ref[0, hk] = o_even[:, :HEAD_DIM] + o_odd[:, HEAD_DIM:]
        score = jnp.sum(p_c[0:GROUP], axis=0, keepdims=True)
        sc = jnp.concatenate([score, jnp.zeros((1, width), F32)], axis=-1)
        forced = (bid2 == 0) | (bid2 == cur) | (bid2 == cur - 1)
        sc = jnp.where(forced, FORCE_SCORE, sc)
        sc = jnp.where(bid2 > cur, -1.0, sc)
        scores.append(jnp.where(bid2 < n_all, sc, -jnp.inf))
    sc_ref[0] = jnp.concatenate(scores, axis=0)


def _samp_cmp(q3, kc3, past):
    db, n_pages, _ = kc3.shape
    wide = 2 * PAGE_BLOCKS * n_pages
    return pl.pallas_call(
        functools.partial(_samp_cmp_body, n_pages=n_pages, past=past),
        grid=(db,),
        in_specs=[pl.BlockSpec((1, 1, C_B), lambda n: (n, 0, 0)),
                  pl.BlockSpec((1, n_pages, KV_HEADS * C_KV), lambda n: (n, 0, 0))],
        out_specs=[pl.BlockSpec((1, KV_HEADS, Q_ROWS, HEAD_DIM), lambda n: (n, 0, 0, 0)),
                   pl.BlockSpec((1, KV_HEADS, wide), lambda n: (n, 0, 0))],
        out_shape=[jax.ShapeDtypeStruct((db, KV_HEADS, Q_ROWS, HEAD_DIM), F32),
                   jax.ShapeDtypeStruct((db, KV_HEADS, wide), F32)],
        compiler_params=pltpu.CompilerParams(dimension_semantics=("arbitrary",)),
        name="nsa_sample_cmp",
    )(q3, kc3)


def _samp_topk_body(sc_ref, idx_ref, *, n_pages, nb_past):
    s_t = sc_ref[...].T
    npos, nrow = s_t.shape
    width = PAGE_BLOCKS * n_pages
    pos = lax.broadcasted_iota(jnp.int32, (npos, 1), 0)
    ids = jnp.where(pos < n_pages, PAGE_BLOCKS * pos,
                    jnp.where(pos < width, PAGE_BLOCKS * (pos - n_pages) + 1, nb_past + pos - width)).astype(F32)
    picks = []
    for _ in range(N_SEL):
        m = jnp.max(s_t, axis=0, keepdims=True)
        pick = jnp.min(jnp.where(s_t == m, ids, 1e9), axis=0, keepdims=True)
        picks.append(pick)
        s_t = jnp.where(ids == pick, -jnp.inf, s_t)
    res = jnp.concatenate(picks + [jnp.zeros((LANES - N_SEL, nrow), F32)], axis=0)
    idx_ref[...] = res.T.astype(jnp.int32)


def _samp_topk(scores2, n_pages):
    rows = scores2.shape[0]
    return pl.pallas_call(
        functools.partial(_samp_topk_body, n_pages=n_pages, nb_past=n_pages * PAGE_BLOCKS),
        out_shape=jax.ShapeDtypeStruct((rows, LANES), jnp.int32),
        name="nsa_sample_topk",
    )(scores2)


def _samp_sel_body(pt_ref, idx_ref, q_ref, gt_ref, oc_ref, ksc_ref, kwr_ref, kwc_ref, win_ref, cache_ref,
                   o_ref, wino_ref, sbuf, sems, *, nb_past, past, wb, db):
    n = pl.program_id(0)
    qpos = past

    page_tok = PAGE_BLOCKS * BLK
    tok = lax.broadcasted_iota(jnp.int32, (HEAD_DIM, page_tok), 1)

    buf = n % 2

    def page_copy(sample, b, hk, s, ib):
        page = pt_ref[sample, ib // PAGE_BLOCKS]
        return pltpu.make_async_copy(cache_ref.at[page, hk], sbuf.at[b, hk, :, :, pl.ds(s * page_tok, page_tok)],
                                     sems.at[b, hk * N_SEL + s])

    def start_pages(sample, b):
        for hk in range(KV_HEADS):
            for s in range(N_SEL):
                ib = idx_ref[sample, hk * N_SEL + s]

                @pl.when(ib < nb_past)
                def _():
                    page_copy(sample, b, hk, s, ib).start()

    @pl.when(n == 0)
    def _():
        start_pages(0, 0)

    @pl.when(n + 1 < db)
    def _():
        start_pages(n + 1, 1 - buf)

    for hk in range(KV_HEADS):
        for s in range(N_SEL):
            ib = idx_ref[n, hk * N_SEL + s]

            @pl.when(ib >= nb_past)
            def _():
                for c in range(2):
                    sbuf[buf, hk, c, :, s * page_tok:(s + 1) * page_tok] = jnp.where(tok == 0, ksc_ref[0, 2 * hk + c],
                                                                                    0.0)

    wtok = lax.broadcasted_iota(jnp.int32, (HEAD_DIM, wb), 1)
    for hk in range(KV_HEADS):
        for c in range(2):
            wino_ref[0, hk, c] = jnp.where(wtok == wb - 1, kwc_ref[0, 2 * hk + c],
                                           pltpu.roll(win_ref[0, hk, c], wb - 1, 1))

    for hk in range(KV_HEADS):
        for s in range(N_SEL):
            ib = idx_ref[n, hk * N_SEL + s]

            @pl.when(ib < nb_past)
            def _():
                page_copy(n, buf, hk, s, ib).wait()

    gt = jax.nn.sigmoid(gt_ref[0])
    nk = N_SEL * page_tok
    lane = lax.broadcasted_iota(jnp.int32, (1, nk), 1)
    kposw = past - wb + lax.broadcasted_iota(jnp.int32, (1, wb), 1)
    wmask = (kposw <= qpos) & (qpos - kposw < WINDOW) & (kposw >= 0)
    pieces = []
    for hk in range(KV_HEADS):
        q8 = _head_rows(q_ref.at[0], hk, pad_rows=Q_ROWS - GROUP, pad_lanes=False)
        slope = _slope_col(hk, 1, pad_rows=Q_ROWS - GROUP)
        ibv = jnp.zeros((1, nk), jnp.int32)
        for s in range(N_SEL):
            ibv = jnp.where(lane // page_tok == s, idx_ref[n, hk * N_SEL + s], ibv)
        spos = (ibv - ibv % PAGE_BLOCKS) * BLK + lane % page_tok
        s_s = _dot(q8, sbuf[buf, hk, 0].astype(BF16)) - slope * (qpos - spos).astype(F32)
        p_s = _masked_softmax(s_s, (spos // BLK == ibv) & (spos <= qpos))
        o_s = _dot_nt(p_s.astype(BF16), sbuf[buf, hk, 1].astype(BF16))
        kwn = kwr_ref[0]
        k_new = kwn[:, hk * KV_LANES:hk * KV_LANES + HEAD_DIM]
        v_new = kwn[:, hk * KV_LANES + HEAD_DIM:(hk + 1) * KV_LANES]
        s_w = _dot(q8, win_ref[0, hk, 0].astype(BF16)) - slope * (qpos - kposw).astype(F32)
        s_w = jnp.where(wmask, s_w, NEG)
        s_n = jnp.sum(q8.astype(F32) * k_new, axis=-1, keepdims=True)
        m_w = jnp.maximum(jnp.max(s_w, axis=-1, keepdims=True), s_n)
        e_w = jnp.exp(s_w - m_w)
        e_n = jnp.exp(s_n - m_w)
        den = jnp.sum(e_w, axis=-1, keepdims=True) + e_n
        o_w = (_dot_nt(e_w.astype(BF16), win_ref[0, hk, 1].astype(BF16)) + e_n * v_new) / den
        o_c = oc_ref[0, hk]
        for g in range(GROUP):
            c0 = (hk * GROUP + g) * 3
            pieces.append(gt[:, c0:c0 + 1] * o_c[g:g + 1] + gt[:, c0 + 1:c0 + 2] * o_s[g:g + 1]
                          + gt[:, c0 + 2:c0 + 3] * o_w[g:g + 1])
    o_ref[0] = jnp.concatenate(pieces, axis=-1)


def _samp_sel(page_table, idx, q3, gates3, oc, ks_cols, kw_row, kw_cols, win_t, cache_t, past):
    db, n_pages = page_table.shape
    wb = win_t.shape[-1]
    nb_past = n_pages * PAGE_BLOCKS
    row3 = lambda w: pl.BlockSpec((1, 1, w), lambda n, pt, ix: (n, 0, 0))
    col4 = pl.BlockSpec((1, 2 * KV_HEADS, HEAD_DIM, 1), lambda n, pt, ix: (n, 0, 0, 0))
    win_spec = pl.BlockSpec((1, KV_HEADS, 2, HEAD_DIM, wb), lambda n, pt, ix: (n, 0, 0, 0, 0))
    grid_spec = pltpu.PrefetchScalarGridSpec(
        num_scalar_prefetch=2,
        grid=(db,),
        in_specs=[row3(C_B), row3(GATE_PAD),
                  pl.BlockSpec((1, KV_HEADS, Q_ROWS, HEAD_DIM), lambda n, pt, ix: (n, 0, 0, 0)),
                  col4, row3(C_KV), col4, win_spec, pl.BlockSpec(memory_space=pl.ANY)],
        out_specs=[row3(C_B), win_spec],
        scratch_shapes=[pltpu.VMEM((2, KV_HEADS, 2, HEAD_DIM, N_SEL * PAGE_BLOCKS * BLK), F32),
                        pltpu.SemaphoreType.DMA((2, KV_HEADS * N_SEL))],
    )
    return pl.pallas_call(
        functools.partial(_samp_sel_body, nb_past=nb_past, past=past, wb=wb, db=db),
        grid_spec=grid_spec,
        out_shape=[jax.ShapeDtypeStruct((db, 1, C_B), F32), jax.ShapeDtypeStruct(win_t.shape, F32)],
        compiler_params=pltpu.CompilerParams(dimension_semantics=("arbitrary",)),
        name="nsa_sample_sel",
    )(page_table, idx, q3, gates3, oc, ks_cols, kw_row, kw_cols, win_t, cache_t)


FF_CHUNK = 1408


def _merge_body(x_ref, y_ref, bonus_ref, g_ref, yb_ref, mg_ref, p_ref, cbuf_ref,
                lng_ref, lnb_ref, bd_ref, woa_ref, wob_ref, wout_ref, n2_ref, wup_ref, cw_ref, cb_ref, wdn_ref,
                n3_ref, wpe_ref, wpg_ref, fg_ref, o_ref, cnew_ref, carry_ref, *, seq_mode, final, d_ff, tm):
    i = pl.program_id(0)
    d_model = x_ref.shape[1]
    y = y_ref[...]
    inv = 1.0 / A_HEAD_DIM
    mean = _dot_ones(y, bd_ref[...]) * inv
    d = y - mean
    var = _dot_ones(d * d, bd_ref[...]) * inv
    ya = (d * lax.rsqrt(var + LNX_EPS) * lng_ref[...] + lnb_ref[...] + bonus_ref[...]) * g_ref[...]
    mg = mg_ref[...]
    m = (jax.nn.sigmoid(mg[:, :d_model]) * _dot(ya.astype(BF16), woa_ref[...])
         + jax.nn.sigmoid(mg[:, d_model:]) * _dot(yb_ref[...].astype(BF16), wob_ref[...]))
    h = x_ref[...] + _dot(m.astype(BF16), wout_ref[...])
    xn = _rms(h, n2_ref[...]).astype(BF16)

    if seq_mode:
        @pl.when(i == 0)
        def _():
            carry_ref[...] = jnp.zeros_like(carry_ref)
        rid = lax.broadcasted_iota(jnp.int32, (tm, FF_CHUNK), 0)

    acc = jnp.zeros((tm, d_model), F32)
    for c in range(d_ff // FF_CHUNK):
        parts = []
        for half in range(2):
            cs = slice(half * d_ff + c * FF_CHUNK, half * d_ff + (c + 1) * FF_CHUNK)
            up = _dot(xn, wup_ref[:, cs])
            if seq_mode:
                t1 = carry_ref[SUBLANES - 1:SUBLANES, cs]
                t2 = carry_ref[SUBLANES - 2:SUBLANES - 1, cs]
                up1 = jnp.where(rid == 0, t1, pltpu.roll(up, 1, 0))
                up2 = jnp.where(rid == 0, t2, jnp.where(rid == 1, t1, pltpu.roll(up, 2, 0)))
                carry_ref[:, cs] = up[tm - SUBLANES:, :]
            else:
                up2 = cbuf_ref[:, cs]
                up1 = cbuf_ref[:, 2 * d_ff + cs.start:2 * d_ff + cs.stop]
                cnew_ref[:, cs] = up1
                cnew_ref[:, 2 * d_ff + cs.start:2 * d_ff + cs.stop] = up
            parts.append(cb_ref[:, cs] + cw_ref[0:1, cs] * up2 + cw_ref[1:2, cs] * up1 + cw_ref[2:3, cs] * up)
        a, gate = parts
        act = (a * jax.nn.sigmoid(a) * gate).astype(BF16)
        acc = acc + _dot(act, wdn_ref[c * FF_CHUNK:(c + 1) * FF_CHUNK, :])
    if seq_mode:
        cnew_ref[...] = carry_ref[...]
    h = h + acc
    pe = _dot(p_ref[...].astype(BF16), wpe_ref[...])
    h = h + pe * jax.nn.sigmoid(_dot(_rms(h, n3_ref[...]).astype(BF16), wpg_ref[...]))
    o_ref[...] = _rms(h, fg_ref[...]) if final else h


def _merge(x, y, bonus, g, yb, mg, p, cbuf, consts, tm, seq_mode, final):
    rows, d_model = x.shape
    d_ff = consts[10].shape[0]
    f2 = 2 * d_ff
    rowspec = lambda w: pl.BlockSpec((tm, w), lambda i: (i, 0))
    if seq_mode:
        cbuf_spec = _const_spec(cbuf.shape)
        cnew_shape, cnew_spec = (SUBLANES, f2), pl.BlockSpec((SUBLANES, f2), lambda i: (0, 0))
    else:
        cbuf_spec = rowspec(2 * f2)
        cnew_shape, cnew_spec = (rows, 2 * f2), rowspec(2 * f2)
    wbytes = sum(int(c.size) * c.dtype.itemsize for c in consts)
    act = tm * (d_model * 3 + C_A * 4 + 256) * 4 + (0 if seq_mode else 2 * tm * 2 * f2 * 4)
    vmem = wbytes + 2 * act + 8 * tm * d_model * 4 + (8 << 20)
    return pl.pallas_call(
        functools.partial(_merge_body, seq_mode=seq_mode, final=final, d_ff=d_ff, tm=tm),
        grid=(rows // tm,),
        in_specs=[rowspec(d_model), rowspec(C_A), rowspec(C_A), rowspec(C_A), rowspec(C_B), rowspec(2 * d_model),
                  rowspec(p.shape[1]), cbuf_spec] + [_const_spec(c.shape) for c in consts],
        out_specs=[rowspec(d_model), cnew_spec],
        out_shape=[jax.ShapeDtypeStruct((rows, d_model), F32), jax.ShapeDtypeStruct(cnew_shape, F32)],
        scratch_shapes=[pltpu.VMEM((SUBLANES, f2), F32)],
        compiler_params=pltpu.CompilerParams(dimension_semantics=("arbitrary",), vmem_limit_bytes=_vmem_limit(vmem)),
        name="merge_ffn",
    )(x, y, bonus, g, yb, mg, p, cbuf, *consts)


def _pick_tile(n, target):
    t = min(n, target)
    while n % t:
        t //= 2
    return t


def _head_block_diag():
    h = jnp.arange(C_A) // A_HEAD_DIM
    return (h[:, None] == h[None, :]).astype(BF16)


def _compress_weights(pe, w1, b1, w2):
    eye = jnp.eye(KV_HEADS, dtype=F32)
    eye_c = jnp.eye(2, dtype=F32)
    w_big = jnp.einsum('crdf,hg,ce->rhcdgef', w1, eye, eye_c).reshape(BLK * C_KV, KV_HEADS * 2 * D_CMP)
    w2_big = jnp.einsum('cfd,hg,ce->hcfged', w2, eye, eye_c).reshape(KV_HEADS * 2 * D_CMP, C_KV)
    pe_row = jnp.broadcast_to(jnp.transpose(pe, (1, 0, 2))[:, None], (BLK, KV_HEADS, 2, HEAD_DIM)).reshape(1, -1)
    b1_row = jnp.broadcast_to(b1[None], (KV_HEADS, 2, D_CMP)).reshape(1, -1)
    return w_big.astype(BF16), w2_big.astype(BF16), pe_row, b1_row


def kernel(x_prompt, x_sample, p_prompt, p_sample, cache_cmp_kv, cache_sel_kv, page_table, state_win_kv, state_wkv, state_shift, state_ffn_conv, norm1_g, w_in, shift_mu, rwkv_w0, rwkv_w2, rwkv_a0, rwkv_a2, rwkv_g2, rwkv_k_k, rwkv_k_a, rwkv_r_k, lnx_g, lnx_b, cmp_pe, cmp_w1, cmp_b1, cmp_w2, w_oa, w_ob, w_out, norm2_g, w_up, conv_w, conv_b, w_down, norm3_g, w_pe, w_pg, final_g):
    depth = w_in.shape[0]
    b, t, d_model = x_prompt.shape
    db, dt, _ = x_sample.shape
    n_pool, page = cache_cmp_kv.shape[1], cache_cmp_kv.shape[2]
    n_pages = page_table.shape[1]
    past = n_pages * page
    d_ff = w_down.shape[1]
    f2 = 2 * d_ff
    wb = state_win_kv.shape[2]
    assert b == 1 and dt == 1 and page == PAGE_BLOCKS * BLK
    assert t % 512 == 0 and t >= WINDOW + 256 and d_ff % FF_CHUNK == 0 and wb == WINDOW and past >= WINDOW
    assert N_SEL <= LANES and n_pages * PAGE_BLOCKS + 1 >= N_SEL

    bd = _head_block_diag()
    hp = x_prompt.reshape(t, d_model)
    hs = x_sample.reshape(db, d_model)
    outs = [[] for _ in range(12)]
    for i in range(depth):
        o_g = C_SHIFT + C_B + 3 * C_KV
        w_perm = jnp.concatenate(
            [w_in[i][:, :o_g], jnp.pad(w_in[i][:, o_g:o_g + 3 * Q_HEADS], ((0, 0), (0, GATE_PAD - 3 * Q_HEADS))),
             w_in[i][:, o_g + 3 * Q_HEADS:]], axis=1).astype(BF16)
        g1 = norm1_g[i].reshape(1, -1)
        rw = (shift_mu[i], rwkv_w0[i], rwkv_w2[i], rwkv_a0[i], rwkv_a2[i], rwkv_g2[i], rwkv_k_k[i], rwkv_k_a[i],
              rwkv_r_k[i].reshape(-1))
        w_big, w2_big, pe_row, b1_row = _compress_weights(cmp_pe[i], cmp_w1[i], cmp_b1[i], cmp_w2[i])
        zero_bias = jnp.zeros_like(b1_row)
        cmp_bias = _compress(jnp.broadcast_to(pe_row, (SUBLANES, pe_row.shape[1])), w_big, zero_bias, w2_big,
                             SUBLANES, 2048, mlp=False)[0:1] + b1_row
        eye_b = jnp.eye(PAGE_BLOCKS, dtype=F32)
        w_tok = jnp.einsum('crdf,eg->cdergf', cmp_w1[i], eye_b).reshape(2 * HEAD_DIM * PAGE_BLOCKS * BLK,
                                                                        PAGE_BLOCKS * D_CMP).astype(BF16)
        w2_pair = jnp.einsum('cfd,eg->cefgd', cmp_w2[i], eye_b).reshape(2, PAGE_BLOCKS * D_CMP,
                                                                       PAGE_BLOCKS * HEAD_DIM).astype(BF16)
        bias_pair = jnp.tile(cmp_bias[0, :2 * D_CMP].reshape(2, D_CMP), (1, PAGE_BLOCKS))
        row = lambda z: z.reshape(1, -1)
        mconsts = [row(lnx_g[i]), row(lnx_b[i]), bd, w_oa[i].astype(BF16), w_ob[i].astype(BF16),
                   w_out[i].astype(BF16), row(norm2_g[i]), w_up[i].astype(BF16), conv_w[i], row(conv_b[i]),
                   w_down[i].astype(BF16), row(norm3_g[i]), w_pe[i].astype(BF16), w_pg[i].astype(BF16), row(final_g)]
        final = i == depth - 1

        pa, q, kvc, kvs, kvs16, kvw, kvw16, gates, mg = _proj(hp, g1, w_perm, _pick_tile(t, 256))
        seqs = _rwkv_prep(pa, None, rw, bd, _pick_tile(t, 512), True)
        r_, w_, lw_, k_, v_, a_, b_, g_, bonus = seqs
        y, s_new = _rwkv_chunk_scan([z.reshape(1, t, C_A) for z in (r_, lw_, k_, v_, a_, b_)],
                                    jnp.zeros((1, A_HEADS, A_HEAD_DIM, A_HEAD_DIM), F32), _pick_tile(t, 512))
        kvc_blocks = _compress(kvc.reshape(t // BLK, BLK * C_KV), w_big, cmp_bias, w2_big,
                               _pick_tile(t // BLK, 256), 2048)
        yb = _nsa_prompt(q, gates, kvc_blocks, kvs16, kvw16, 256, 512)
        hp, conv_new = _merge(hp, y.reshape(t, C_A), bonus, g_, yb, mg, p_prompt[i].reshape(t, -1),
                              jnp.zeros((SUBLANES, LANES), F32), mconsts, _pick_tile(t, 256), True, final)
        kv6 = lambda z, n_: z.reshape(n_, -1, KV_HEADS, 2, HEAD_DIM)
        outs[0].append(kv6(kvc, 1))
        outs[2].append(kv6(kvs, 1))
        outs[4].append(kv6(kvw[t - min(WINDOW, t):], 1))
        outs[6].append(s_new)
        outs[8].append(pa[t - 1:t])
        outs[10].append(conv_new[SUBLANES - (CONV_W - 1):].reshape(1, CONV_W - 1, f2))

        pa, q, kvc, kvs, kvs16, kvw, kvw16, gates, mg = _proj(hs, g1, w_perm, _pick_tile(db, 128))
        seqs = _rwkv_prep(pa, state_shift[i], rw, bd, _pick_tile(db, 128), False)
        r_, w_, lw_, k_, v_, a_, b_, g_, bonus = seqs
        y, s_new = _rwkv_scan([z.reshape(db, 1, C_A) for z in (r_, w_, k_, v_, a_, b_)], state_wkv[i], 1,
                              _pick_tile(db, SUBLANES))
        to_tok_minor = lambda z: jnp.transpose(z, (0, 2, 3, 4, 1))
        kc_pages = _compress_pool(page_table, to_tok_minor(cache_cmp_kv[i]).reshape(n_pool, KV_HEADS, 2 * HEAD_DIM, page),
                                  w_tok, bias_pair, w2_pair)
        q3 = q.reshape(db, 1, C_B)
        oc, sel_scores = _samp_cmp(q3, kc_pages.reshape(db, n_pages, KV_HEADS * C_KV), past)
        idx = _samp_topk(sel_scores.reshape(db * KV_HEADS, -1), n_pages)
        idx2 = idx[:, :N_SEL].reshape(db, KV_HEADS * N_SEL)
        cols = lambda z: z.reshape(db, 2 * KV_HEADS, HEAD_DIM, 1)
        yb, win_new_t = _samp_sel(page_table, idx2, q3, gates.reshape(db, 1, GATE_PAD), oc, cols(kvs),
                                  kvw.reshape(db, 1, C_KV), cols(kvw), to_tok_minor(state_win_kv[i]),
                                  to_tok_minor(cache_sel_kv[i]), past)
        win_new = jnp.transpose(win_new_t, (0, 4, 1, 2, 3))
        hs, conv_new = _merge(hs, y.reshape(db, C_A), bonus, g_, yb.reshape(db, C_B), mg, p_sample[i].reshape(db, -1),
                              state_ffn_conv[i].reshape(db, 2 * f2), mconsts, _pick_tile(db, 128), False, final)
        outs[1].append(kv6(kvc, db))
        outs[3].append(kv6(kvs, db))
        outs[5].append(win_new.reshape(db, wb, KV_HEADS, 2, HEAD_DIM))
        outs[7].append(s_new)
        outs[9].append(pa)
        outs[11].append(conv_new.reshape(db, CONV_W - 1, f2))

    stacked = [jnp.stack(o) for o in outs]
    return (hp.reshape(b, t, d_model), hs.reshape(db, dt, d_model), *stacked)
```

```python
import functools

import jax
import jax.numpy as jnp
from jax import lax
from jax.experimental import pallas as pl
from jax.experimental.pallas import tpu as pltpu

F32 = jnp.float32
BF16 = jnp.bfloat16
HI = lax.Precision.HIGHEST

A_HEADS = 8
A_HEAD_DIM = 64
C_A = A_HEADS * A_HEAD_DIM
R_W = 64
R_A = 64
R_G = 128
C_SHIFT = 3 * C_A + R_W + R_A + R_G
LNX_EPS = 64e-5
Q_HEADS = 8
KV_HEADS = 2
GROUP = Q_HEADS // KV_HEADS
HEAD_DIM = 64
C_B = Q_HEADS * HEAD_DIM
C_KV = KV_HEADS * 2 * HEAD_DIM
BLK = 64
N_SEL = 16
WINDOW = 512
D_CMP = 128
FORCE_SCORE = 1e4
CONV_W = 3
NORM_EPS = 1e-6

LANES = 128
SUBLANES = 8
VMEM_BYTES_V7X = 64 * 1024 * 1024

NEG = -1e30
M_FLOOR = -1e29
KV_LANES = 2 * HEAD_DIM

GATE_PAD = LANES


def _vmem_limit(nbytes):
    return int(min(max(nbytes, 16 * 1024 * 1024), VMEM_BYTES_V7X - 8 * 1024 * 1024))


def _const_spec(shape):
    nd = len(shape)
    return pl.BlockSpec(shape, lambda *_: (0,) * nd, pipeline_mode=pl.Buffered(1))


def _rms(x, g):
    return x * lax.rsqrt(jnp.mean(x * x, axis=-1, keepdims=True) + NORM_EPS) * g


def _dot(a, b, **kw):
    return jnp.dot(a, b, preferred_element_type=F32, **kw)


def _dot_ones(x, ones):
    hi = x.astype(BF16)
    lo = (x - hi.astype(F32)).astype(BF16)
    return _dot(hi, ones) + _dot(lo, ones)


def _dot_nt(a, b):
    return lax.dot_general(a, b, (((1,), (1,)), ((), ())), preferred_element_type=F32)


def _masked_softmax(s, mask):
    s = jnp.where(mask, s, NEG)
    m = jnp.max(s, axis=-1, keepdims=True)
    m = jnp.where(m > M_FLOOR, m, 0.0)
    e = jnp.exp(s - m)
    return e / jnp.maximum(jnp.sum(e, axis=-1, keepdims=True), 1e-30)


def _slope_col(hk, rows_per_head, pad_rows=0):
    cols = [jnp.full((rows_per_head, 1), 2.0 ** (-(hk * GROUP + g + 1)), F32) for g in range(GROUP)]
    if pad_rows:
        cols.append(jnp.zeros((pad_rows, 1), F32))
    return jnp.concatenate(cols, axis=0)


def _proj_body(x_ref, g_ref, w_ref, *o_refs, segs):
    xb = _rms(x_ref[...], g_ref[...]).astype(BF16)
    it = iter(o_refs)
    for off, width, outs in segs:
        r = _dot(xb, w_ref[:, off:off + width])
        for scale, transposed in outs:
            o_ref = next(it)
            val = r * scale if scale != 1.0 else r
            o_ref[...] = (val.T if transposed else val).astype(o_ref.dtype)


def _proj(x, g, w_perm, tm, kv_rows):
    rows, d = x.shape
    d_model = d
    kv = lambda name: ([(name, F32, 1.0, False)] if kv_rows or name == "kvc" else []) + \
        [(name + "16", BF16, 1.0, False), (name + "_t", F32, 1.0, True)]
    groups = ((C_SHIFT, [("pa", F32, 1.0, False)]), (C_B, [("q", BF16, HEAD_DIM ** -0.5, False)]),
              (C_KV, kv("kvc")), (C_KV, kv("kvs")), (C_KV, kv("kvw")),
              (GATE_PAD, [("gates", F32, 1.0, False)]), (2 * d_model, [("mg", F32, 1.0, False)]))
    o = 0
    segs, names, shapes, specs = [], [], [], []
    for width, outs in groups:
        segs.append((o, width, tuple((s, tr) for _, _, s, tr in outs)))
        for name, dt, _, tr in outs:
            names.append(name)
            shapes.append(jax.ShapeDtypeStruct((width, rows) if tr else (rows, width), dt))
            specs.append(pl.BlockSpec((width, tm), lambda i: (0, i)) if tr else pl.BlockSpec((tm, width), lambda i: (i, 0)))
        o += width
    n_tot = o
    out_bytes = sum(tm * s.shape[0 if s.shape[1] == rows else 1] * s.dtype.itemsize for s in shapes)
    vmem = 2 * tm * d * 4 + d * n_tot * 2 + 2 * out_bytes + (8 << 20)
    outs = pl.pallas_call(
        functools.partial(_proj_body, segs=tuple(segs)),
        grid=(rows // tm,),
        in_specs=[pl.BlockSpec((tm, d), lambda i: (i, 0)), _const_spec((1, d)), _const_spec((d, n_tot))],
        out_specs=specs,
        out_shape=shapes,
        compiler_params=pltpu.CompilerParams(dimension_semantics=("arbitrary",), vmem_limit_bytes=_vmem_limit(vmem)),
        name="proj",
    )(x, g, w_perm)
    return dict(zip(names, outs))


def _prep_body(pa_ref, prev_ref, mu_ref, w0_ref, w2_ref, a0_ref, a2_ref, g2_ref, kk_ref, ka_ref, rk_ref, bd_ref,
               r_o, w_o, lw_o, k_o, v_o, a_o, b_o, g_o, bonus_o, *, seq_mode):
    pf = pa_ref[...]
    if seq_mode:
        i = pl.program_id(0)
        first = jnp.where(i > 0, prev_ref[SUBLANES - 1:SUBLANES, :], 0.0)
        rid = lax.broadcasted_iota(jnp.int32, pf.shape, 0)
        prev = jnp.where(rid == 0, first, pltpu.roll(pf, 1, 0))
    else:
        prev = prev_ref[...]
    xs = pf + (prev - pf) * mu_ref[...]
    r = xs[:, 0:C_A]
    k = xs[:, C_A:2 * C_A]
    v = xs[:, 2 * C_A:3 * C_A]
    o = 3 * C_A
    wd = xs[:, o:o + R_W]
    ad = xs[:, o + R_W:o + R_W + R_A]
    gd = xs[:, o + R_W + R_A:]
    nz = -(w0_ref[...] + _dot(jnp.tanh(wd), w2_ref[...], precision=HI))
    softplus = jnp.maximum(nz, 0.0) + jnp.log(1.0 + jnp.exp(-jnp.abs(nz)))
    w = -softplus - 0.5
    a = jax.nn.sigmoid(a0_ref[...] + _dot(ad, a2_ref[...], precision=HI))
    g = _dot(jax.nn.sigmoid(gd), g2_ref[...], precision=HI)
    kk = k * kk_ref[...]
    ss = _dot_ones(kk * kk, bd_ref[...])
    kkn = kk / jnp.maximum(jnp.sqrt(ss), 1e-12)
    k2 = k * (1.0 + (a - 1.0) * ka_ref[...])
    r_o[...] = r
    lw = -jnp.exp(w)
    lw_o[...] = lw
    w_o[...] = jnp.exp(lw)
    k_o[...] = k2
    v_o[...] = v
    a_o[...] = -kkn
    b_o[...] = kkn * a
    g_o[...] = g
    bonus_o[...] = _dot_ones(r * k2 * rk_ref[...], bd_ref[...]) * v


def _rwkv_prep(pa, prev, rw, bd, tm, seq_mode):
    rows = pa.shape[0]
    mu, w0, w2, a0, a2, g2, k_k, k_a, r_k = rw
    if seq_mode:
        tb = tm // SUBLANES
        prev_spec = pl.BlockSpec((SUBLANES, C_SHIFT), lambda i: (jnp.maximum(i * tb - 1, 0), 0))
        prev = pa
    else:
        prev_spec = pl.BlockSpec((tm, C_SHIFT), lambda i: (i, 0))
    row = lambda z: z.reshape(1, -1)
    consts = [row(mu), row(w0), w2, row(a0), a2, g2, row(k_k), row(k_a), row(r_k), bd]
    vmem = 4 * tm * C_SHIFT * 4 + 2 * 8 * tm * C_A * 4 + 16 * tm * C_A * 4 + (8 << 20)
    return pl.pallas_call(
        functools.partial(_prep_body, seq_mode=seq_mode),
        grid=(rows // tm,),
        in_specs=[pl.BlockSpec((tm, C_SHIFT), lambda i: (i, 0)), prev_spec] + [_const_spec(c.shape) for c in consts],
        out_specs=[pl.BlockSpec((tm, C_A), lambda i: (i, 0))] * 9,
        out_shape=[jax.ShapeDtypeStruct((rows, C_A), F32)] * 9,
        compiler_params=pltpu.CompilerParams(dimension_semantics=("arbitrary",), vmem_limit_bytes=_vmem_limit(vmem)),
        name="rwkv_prep",
    )(pa, prev, *consts)


N_PAIR = A_HEADS // 2


def _scan_body(r_ref, w_ref, k_ref, v_ref, a_ref, b_ref, s0_ref, y_ref, so_ref, st_ref, *, tc, nc, nb):
    c = pl.program_id(1)

    @pl.when(c == 0)
    def _():
        for q in range(nb):
            for p in range(N_PAIR):
                st_ref[q, p] = jnp.concatenate([s0_ref[q, 2 * p], s0_ref[q, 2 * p + 1]], axis=-1)

    shape = (A_HEAD_DIM, LANES)
    lane = lax.broadcasted_iota(jnp.int32, shape, 1)
    sub = lax.broadcasted_iota(jnp.int32, shape, 0)
    lo = lane < A_HEAD_DIM
    diag = (lane & (A_HEAD_DIM - 1)) == sub

    def seg_sum(x):
        s_lo = jnp.sum(jnp.where(lo, x, 0.0), axis=-1, keepdims=True)
        s_hi = jnp.sum(jnp.where(lo, 0.0, x), axis=-1, keepdims=True)
        return jnp.where(lo, s_lo, s_hi)

    grp = min(SUBLANES, tc)

    def token_group(gi, carry):
        base = pl.multiple_of(gi * grp, grp)
        for q in range(nb):
            for p in range(N_PAIR):
                sl = slice(LANES * p, LANES * (p + 1))
                rt, wt, kt, vt, at, bt = (ref[q, pl.ds(base, grp), sl]
                                          for ref in (r_ref, w_ref, k_ref, v_ref, a_ref, b_ref))
                s = st_ref[q, p]
                ys = []
                for j in range(grp):
                    row = lambda z: z[j:j + 1, :]
                    sa = seg_sum(s * row(at))
                    v_col = seg_sum(jnp.where(diag, row(vt), 0.0))
                    s = s * row(wt) + sa * row(bt) + v_col * row(kt)
                    y_col = seg_sum(s * row(rt))
                    ys.append(jnp.sum(jnp.where(diag, y_col, 0.0), axis=0, keepdims=True))
                st_ref[q, p] = s
                y_ref[q, pl.ds(base, grp), sl] = jnp.concatenate(ys, axis=0) if grp > 1 else ys[0]
        return carry

    lax.fori_loop(0, tc // grp, token_group, 0)

    @pl.when(c == nc - 1)
    def _():
        for q in range(nb):
            for p in range(N_PAIR):
                s = st_ref[q, p]
                so_ref[q, 2 * p] = s[:, :A_HEAD_DIM]
                so_ref[q, 2 * p + 1] = s[:, A_HEAD_DIM:]


def _rwkv_scan(seqs, s0, tc, nb):
    b, t, _ = seqs[0].shape
    nc = t // tc
    seq_spec = pl.BlockSpec((nb, tc, C_A), lambda i, c: (i, c, 0))
    st_spec = pl.BlockSpec((nb, A_HEADS, A_HEAD_DIM, A_HEAD_DIM), lambda i, c: (i, 0, 0, 0))
    return pl.pallas_call(
        functools.partial(_scan_body, tc=tc, nc=nc, nb=nb),
        grid=(b // nb, nc),
        in_specs=[seq_spec] * 6 + [st_spec],
        out_specs=[seq_spec, st_spec],
        out_shape=[jax.ShapeDtypeStruct((b, t, C_A), F32), jax.ShapeDtypeStruct(s0.shape, F32)],
        scratch_shapes=[pltpu.VMEM((nb, N_PAIR, A_HEAD_DIM, LANES), F32)],
        compiler_params=pltpu.CompilerParams(dimension_semantics=("arbitrary", "arbitrary")),
        name="rwkv_scan",
    )(*seqs, s0)


SCAN_CHUNK = 64
SCAN_CHUNKS_PER_STEP = 4


def _mm(a, b):
    return _dot(a.astype(BF16), b.astype(BF16))


def _chunk_scan_body(r_ref, lw_ref, k_ref, v_ref, a_ref, b_ref, s0_ref, y_ref, so_ref, st_ref, *, tc, nc):
    cidx = pl.program_id(1)
    c = SCAN_CHUNK
    d = A_HEAD_DIM

    @pl.when(cidx == 0)
    def _():
        for h in range(A_HEADS):
            st_ref[h] = s0_ref[0, h].T

    ri = lax.broadcasted_iota(jnp.int32, (c, c), 0)
    ci = lax.broadcasted_iota(jnp.int32, (c, c), 1)
    lower = ci <= ri
    strict = ci < ri
    eye = lax.broadcasted_iota(jnp.int32, (d, d), 0) == lax.broadcasted_iota(jnp.int32, (d, d), 1)
    n_double = (c - 1).bit_length()
    per = SCAN_CHUNKS_PER_STEP
    slab = per * c
    sr = lax.broadcasted_iota(jnp.int32, (slab, slab), 0)
    sc = lax.broadcasted_iota(jnp.int32, (slab, slab), 1)
    tri = ((sr // c == sc // c) & (sc <= sr)).astype(F32)

    def step_fn(step, carry):
        base = pl.multiple_of(step * slab, slab)
        r, lw, k, v, a, b = (ref[0, pl.ds(base, slab), :] for ref in (r_ref, lw_ref, k_ref, v_ref, a_ref, b_ref))
        cum = _dot(tri, lw, precision=HI)
        tots = [cum[(q + 1) * c - 1:(q + 1) * c, :] for q in range(per)]
        tot = jnp.concatenate([jnp.broadcast_to(t_, (c, C_A)) for t_ in tots], axis=0)
        e_inv = jnp.exp(-cum)
        e_rest = jnp.exp(tot - cum)
        at = a * jnp.exp(cum - lw)
        rt = r * jnp.exp(cum)
        bt = b * e_inv
        kt = k * e_inv
        bh = b * e_rest
        kh = k * e_rest
        g_tot = [jnp.exp(t_) for t_ in tots]
        units = [(q, h) for q in range(per) for h in range(A_HEADS)]
        cut = lambda z, u: z[u[0] * c:(u[0] + 1) * c, u[1] * d:(u[1] + 1) * d]
        g4 = [_dot_nt(jnp.concatenate([cut(at, u), cut(rt, u)], axis=0).astype(BF16),
                      jnp.concatenate([cut(bt, u), cut(kt, u)], axis=0).astype(BF16)) for u in units]
        lp = [jnp.where(strict, g[:c, :c], 0.0).astype(BF16) for g in g4]
        m_l = [jnp.where(strict, g[:c, c:], 0.0) for g in g4]
        p_b = [jnp.where(lower, g[c:, :c], 0.0) for g in g4]
        p_k = [jnp.where(lower, g[c:, c:], 0.0) for g in g4]
        vb = [cut(v, u).astype(BF16) for u in units]
        n_u = range(len(units))
        z = [jnp.concatenate([cut(at, units[i]), _mm(m_l[i], vb[i])], axis=-1) for i in n_u]
        for q in range(n_double):
            z = [z[i] + _mm(lp[i], z[i]) for i in n_u]
            if q < n_double - 1:
                lp = [_mm(lp[i], lp[i]).astype(BF16) for i in n_u]
        zb = [zz.astype(BF16) for zz in z]
        bz = [_mm(cut(bh, units[i]).T, zb[i]) for i in n_u]
        kv = [_mm(cut(kh, units[i]).T, vb[i]) for i in n_u]
        pz = [_mm(p_b[i], zb[i]) for i in n_u]
        pv = [_mm(p_k[i], vb[i]) for i in n_u]
        rows = []
        for q in range(per):
            ys = []
            for h in range(A_HEADS):
                i = q * A_HEADS + h
                st = st_ref[h]
                a_c = jnp.where(eye, g_tot[q][:, h * d:(h + 1) * d], 0.0) + bz[i][:, :d]
                ys.append(_dot(cut(rt, units[i]) + pz[i][:, :d], st, precision=HI) + pz[i][:, d:] + pv[i])
                st_ref[h] = _dot(a_c, st, precision=HI) + bz[i][:, d:] + kv[i]
            rows.append(jnp.concatenate(ys, axis=-1))
        y_ref[0, pl.ds(base, slab), :] = jnp.concatenate(rows, axis=0)
        return carry

    lax.fori_loop(0, tc // slab, step_fn, 0)

    @pl.when(cidx == nc - 1)
    def _():
        for h in range(A_HEADS):
            so_ref[0, h] = st_ref[h].T


def _rwkv_chunk_scan(seqs, s0, tc):
    b, t, _ = seqs[0].shape
    nc = t // tc
    seq_spec = pl.BlockSpec((1, tc, C_A), lambda i, c: (i, c, 0))
    st_spec = pl.BlockSpec((1, A_HEADS, A_HEAD_DIM, A_HEAD_DIM), lambda i, c: (i, 0, 0, 0))
    return pl.pallas_call(
        functools.partial(_chunk_scan_body, tc=tc, nc=nc),
        grid=(b, nc),
        in_specs=[seq_spec] * 6 + [st_spec],
        out_specs=[seq_spec, st_spec],
        out_shape=[jax.ShapeDtypeStruct((b, t, C_A), F32), jax.ShapeDtypeStruct(s0.shape, F32)],
        scratch_shapes=[pltpu.VMEM((A_HEADS, A_HEAD_DIM, A_HEAD_DIM), F32)],
        compiler_params=pltpu.CompilerParams(dimension_semantics=("arbitrary", "arbitrary")),
        name="rwkv_chunk_scan",
    )(*seqs, s0)


def _gelu_tanh(x):
    return 0.5 * x * (1.0 + jnp.tanh(0.7978845608028654 * (x + 0.044715 * (x * x * x))))


def _compress_body(x_ref, pe_ref, w_ref, b1_ref, w2_ref, o_ref, bias_ref, *, nb):
    d = HEAD_DIM
    bias = [_dot(pe_ref[:, c * BLK * d:(c + 1) * BLK * d].astype(BF16), w_ref[c * BLK * d:(c + 1) * BLK * d, :])[0:1]
            + b1_ref[c:c + 1, :] for c in range(2)]
    bias_ref[...] = jnp.concatenate([jnp.broadcast_to(b_, (SUBLANES, D_CMP)) for b_ in bias], axis=-1)
    xt = jnp.swapaxes(x_ref[...].reshape(nb, BLK, C_KV), 0, 1)
    acc = [jnp.zeros((nb, D_CMP), F32) for _ in range(2 * KV_HEADS)]
    for r in range(BLK):
        xr = xt[r].astype(BF16)
        for hc in range(2 * KV_HEADS):
            c = hc % 2
            acc[hc] = acc[hc] + _dot(xr[:, hc * d:(hc + 1) * d], w_ref[(c * BLK + r) * d:(c * BLK + r + 1) * d, :])
    o_ref[...] = jnp.concatenate(
        [_dot(_gelu_tanh(acc[hc] + bias[hc % 2]).astype(BF16), w2_ref[hc % 2]) for hc in range(2 * KV_HEADS)], axis=-1)


def _compress(kv_rows, pe_flat, w1_rows, b1, w2b, nb):
    t = kv_rows.shape[0]
    consts = [pe_flat, w1_rows, b1, w2b]
    return pl.pallas_call(
        functools.partial(_compress_body, nb=nb),
        grid=(t // (nb * BLK),),
        in_specs=[pl.BlockSpec((nb * BLK, C_KV), lambda i: (i, 0))] + [_const_spec(c.shape) for c in consts],
        out_specs=[pl.BlockSpec((nb, C_KV), lambda i: (i, 0)), pl.BlockSpec((SUBLANES, 2 * D_CMP), lambda i: (0, 0))],
        out_shape=[jax.ShapeDtypeStruct((t // BLK, C_KV), F32), jax.ShapeDtypeStruct((SUBLANES, 2 * D_CMP), F32)],
        compiler_params=pltpu.CompilerParams(dimension_semantics=("arbitrary",)),
        name="nsa_compress",
    )(kv_rows, *consts)


PAGE_ROWS = KV_HEADS * 2 * HEAD_DIM


def _compress_pool_body(pt_ref, cache_ref, w_ref, b_ref, w2_ref, o_ref, xbuf, sems, *, pg, nseq):
    n = pl.program_id(0)
    slot = n % 2

    def page_copy(seq, buf, jp):
        return pltpu.make_async_copy(cache_ref.at[pt_ref[seq, jp]], xbuf.at[buf, pl.ds(jp * KV_HEADS, KV_HEADS)],
                                     sems.at[buf])

    @pl.when(n == 0)
    def _():
        for jp in range(pg):
            page_copy(0, 0, jp).start()

    for jp in range(pg):
        page_copy(n, slot, jp).wait()

    nxt = jnp.where(n + 1 < nseq, n + 1, 0)
    n_mm = HEAD_DIM
    per_mm = -(-pg // n_mm)
    issued = 0
    m = pg * KV_HEADS
    for c in range(2):
        xt = jnp.swapaxes(xbuf[slot, :, c * HEAD_DIM:(c + 1) * HEAD_DIM, :], 0, 1)
        acc = jnp.zeros((m, 2 * D_CMP), F32)
        for d in range(0, HEAD_DIM, 2):
            r0 = c * HEAD_DIM + d
            x_pair = jnp.concatenate([xt[d], xt[d + 1]], axis=-1).astype(BF16)
            acc = acc + _dot(x_pair, w_ref[r0 * 2 * BLK:(r0 + 2) * 2 * BLK, :])
            for jp in range(issued, min(issued + per_mm, pg)):
                page_copy(nxt, 1 - slot, jp).start()
            issued = min(issued + per_mm, pg)
        h = acc + b_ref[c:c + 1, :]
        o_ref[:, c * 2 * HEAD_DIM:(c + 1) * 2 * HEAD_DIM] = _dot(_gelu_tanh(h).astype(BF16), w2_ref[c])
    assert issued == pg

    @pl.when(n == nseq - 1)
    def _():
        for jp in range(pg):
            page_copy(nxt, 1 - slot, jp).wait()


def _compress_pool(page_table, cache_t, w_tok, bias2, w2_pair):
    nseq, pg = page_table.shape
    m = pg * KV_HEADS
    vmem = 3 * pg * PAGE_ROWS * 2 * BLK * 4 + int(w_tok.size) * 2 + 8 * m * 2 * D_CMP * 4 + (4 << 20)
    const = lambda z: pl.BlockSpec(z.shape, lambda i, pt: (0,) * z.ndim, pipeline_mode=pl.Buffered(1))
    grid_spec = pltpu.PrefetchScalarGridSpec(
        num_scalar_prefetch=1,
        grid=(nseq,),
        in_specs=[pl.BlockSpec(memory_space=pl.ANY), const(w_tok), const(bias2), const(w2_pair)],
        out_specs=pl.BlockSpec((m, C_KV), lambda i, pt: (i, 0)),
        scratch_shapes=[pltpu.VMEM((2, m, 2 * HEAD_DIM, 2 * BLK), F32), pltpu.SemaphoreType.DMA((2,))],
    )
    return pl.pallas_call(
        functools.partial(_compress_pool_body, pg=pg, nseq=nseq),
        grid_spec=grid_spec,
        out_shape=jax.ShapeDtypeStruct((nseq * m, C_KV), F32),
        compiler_params=pltpu.CompilerParams(dimension_semantics=("arbitrary",), vmem_limit_bytes=_vmem_limit(vmem)),
        name="nsa_compress_pool",
    )(page_table, cache_t, w_tok, bias2, w2_pair)


def _topk_mask(s, ids, k, axis):
    sel = jnp.zeros(s.shape, F32)
    for _ in range(k):
        m = jnp.max(s, axis=axis, keepdims=True)
        pick = jnp.min(jnp.where(s == m, ids, 1e9), axis=axis, keepdims=True)
        hit = ids == pick
        sel = jnp.where(hit, 1.0, sel)
        s = jnp.where(hit, -jnp.inf, s)
    return sel


def _head_rows(q_ref, hk, pad_rows=0, pad_lanes=True):
    parts = [q_ref[:, (hk * GROUP + g) * HEAD_DIM:(hk * GROUP + g + 1) * HEAD_DIM] for g in range(GROUP)]
    if pad_rows:
        parts.append(jnp.zeros((pad_rows, HEAD_DIM), parts[0].dtype))
    qh = jnp.concatenate(parts, axis=0)
    return jnp.concatenate([qh, jnp.zeros_like(qh)], axis=-1) if pad_lanes else qh


AUG_HI = HEAD_DIM
AUG_LO = HEAD_DIM + 1
AUG_ONE = HEAD_DIM + 2
AUG_BLK = HEAD_DIM + 3
POS_SPLIT = LANES


def _nsa_select_body(q_ref, kc_ref, kct_ref, oc_ref, ns_ref, flag_ref, *, tq, nbc):
    i = pl.program_id(0)
    t0 = i * tq
    qrow1 = t0 + lax.broadcasted_iota(jnp.int32, (1, tq), 1)
    qrow = jnp.concatenate([qrow1] * GROUP, axis=1)
    cur_row = qrow1 // BLK
    blkc = lax.broadcasted_iota(jnp.int32, (nbc, 1), 0)
    blkcf = blkc.astype(F32)
    pieces, masks, flags = [], [], []
    for hk in range(KV_HEADS):
        hl = slice(hk * KV_LANES, (hk + 1) * KV_LANES)
        qh = _head_rows(q_ref, hk)
        slope_row = jnp.concatenate([jnp.full((1, tq), 2.0 ** (-(hk * GROUP + g + 1)), F32) for g in range(GROUP)],
                                    axis=1)
        kcb = kc_ref[:, hl].astype(BF16)
        cposc = blkc * BLK + (BLK - 1)
        s_t = _dot_nt(kcb, qh) - slope_row * (qrow - cposc).astype(F32)
        s_t = jnp.where(cposc <= qrow, s_t, NEG)
        m_c = jnp.max(s_t, axis=0, keepdims=True)
        m_c = jnp.where(m_c > M_FLOOR, m_c, 0.0)
        e_c = jnp.exp(s_t - m_c)
        p_t = e_c / jnp.maximum(jnp.sum(e_c, axis=0, keepdims=True), 1e-30)
        o_c = _dot(kct_ref[hl, :].astype(BF16), p_t.astype(BF16))[HEAD_DIM:, :].T
        score = p_t[:, 0:tq]
        for g in range(1, GROUP):
            score = score + p_t[:, g * tq:(g + 1) * tq]
        forced = (blkc == 0) | (blkc == cur_row) | (blkc == cur_row - 1)
        sc = jnp.where(forced, FORCE_SCORE, score)
        sc = jnp.where(blkc > cur_row, -1.0, sc)
        sel_t = _topk_mask(sc, blkcf, min(N_SEL, nbc), axis=0)
        sel_q = jnp.where(blkc <= cur_row, sel_t, 0.0).T
        masks.append(1.0 - sel_q)
        flags.append(jnp.max(sel_q, axis=0, keepdims=True))
        pieces += [o_c[g * tq:(g + 1) * tq] for g in range(GROUP)]
    oc_ref[...] = jnp.concatenate(pieces, axis=-1)
    ns_ref[...] = jnp.concatenate(masks, axis=-1).astype(BF16)
    flag_ref[0] = jnp.concatenate(flags, axis=0)


def _nsa_select(q, kvc, tq):
    t = q.shape[0]
    nbc = kvc.shape[0]
    kct = kvc.T
    return pl.pallas_call(
        functools.partial(_nsa_select_body, tq=tq, nbc=nbc),
        grid=(t // tq,),
        in_specs=[pl.BlockSpec((tq, C_B), lambda i: (i, 0)), _const_spec(kvc.shape), _const_spec(kct.shape)],
        out_specs=[pl.BlockSpec((tq, C_B), lambda i: (i, 0)), pl.BlockSpec((tq, KV_HEADS * nbc), lambda i: (i, 0)),
                   pl.BlockSpec((1, KV_HEADS, nbc), lambda i: (i, 0, 0))],
        out_shape=[jax.ShapeDtypeStruct((t, C_B), F32), jax.ShapeDtypeStruct((t, KV_HEADS * nbc), BF16),
                   jax.ShapeDtypeStruct((t // tq, KV_HEADS, nbc), F32)],
        compiler_params=pltpu.CompilerParams(dimension_semantics=("arbitrary",)),
        name="nsa_select",
    )(q, kvc, kct)


def _nsa_prompt_body(cnt_ref, lst_ref, q_ref, gt_ref, oc_ref, ns_ref, ka_ref, vat_ref, kw_ref, em_ref, o_ref,
                     *, tq, tk, nbc, max_tiles):
    i = pl.program_id(0)
    t0 = i * tq
    rows = GROUP * tq
    qpos1 = t0 + lax.broadcasted_iota(jnp.int32, (tq, 1), 0)
    qpos = jnp.concatenate([qpos1] * GROUP, axis=0)
    qposf = qpos.astype(F32)
    qrow = jnp.concatenate([t0 + lax.broadcasted_iota(jnp.int32, (1, tq), 1)] * GROUP, axis=1)
    blk_row = lax.broadcasted_iota(jnp.int32, (1, nbc), 1)
    lane = lax.broadcasted_iota(jnp.int32, (1, KV_LANES), 1)
    gt = jax.nn.sigmoid(gt_ref[...])
    bpt = tk // BLK
    t0f = t0.astype(F32)
    o_ws, not_sels, q_augs = [], [], []
    for hk in range(KV_HEADS):
        hl = slice(hk * KV_LANES, (hk + 1) * KV_LANES)
        qh = _head_rows(q_ref, hk)
        slope = _slope_col(hk, tq)
        not_sels.append(ns_ref[:, hk * nbc:(hk + 1) * nbc])

        wl = WINDOW + tq
        ws = pl.multiple_of(jnp.maximum(t0 - WINDOW, 0), tq)
        kvw = kw_ref[pl.ds(ws, wl), hl]
        kposw = ws + lax.broadcasted_iota(jnp.int32, (1, wl), 1)
        s = _dot_nt(qh, kvw) - slope * (qposf - kposw.astype(F32))
        p_w = _masked_softmax(s, (kposw <= qpos) & (qpos - kposw < WINDOW))
        o_ws.append(_dot(p_w.astype(BF16), kvw)[:, HEAD_DIM:])

        q_augs.append(qh.astype(F32) + jnp.where(lane == AUG_HI, slope * POS_SPLIT, 0.0)
                      + jnp.where(lane == AUG_LO, slope, 0.0) + jnp.where(lane == AUG_ONE, -slope * t0f, 0.0))

    def sel_step(tiles, active, carry, causal):
        out = []
        for hk in range(KV_HEADS):
            m, acc = carry[hk]
            j = tiles[hk]
            k0 = pl.multiple_of(j * tk, tk)
            ns = not_sels[hk] if active[hk] is None else jnp.where(active[hk], not_sels[hk], jnp.ones_like(not_sels[hk]))
            mk = _dot(jnp.where(blk_row // bpt == j, ns, jnp.zeros_like(ns)), em_ref[...])
            q_j = (q_augs[hk] + jnp.concatenate([mk] * GROUP, axis=0)).astype(BF16)
            s = _dot_nt(ka_ref[hk, pl.ds(k0, tk), :], q_j)
            if causal:
                kpos = k0 + lax.broadcasted_iota(jnp.int32, (tk, 1), 0)
                s = jnp.where(kpos <= qrow, s, NEG)
            m_new = jnp.maximum(m, jnp.max(s, axis=0, keepdims=True))
            p = jnp.exp(s - m_new).astype(BF16)
            out.append((m_new, jnp.exp(m - m_new) * acc + _dot(vat_ref[hk, j], p)))
        return tuple(out)

    cnts = [cnt_ref[i * KV_HEADS + hk] for hk in range(KV_HEADS)]

    def skip_step(n, carry):
        active = [n < cnts[hk] - 1 for hk in range(KV_HEADS)]
        tiles = [jnp.where(active[hk], lst_ref[(i * KV_HEADS + hk) * max_tiles + n], 0) for hk in range(KV_HEADS)]
        return sel_step(tiles, active, carry, False)

    init = tuple((jnp.full((1, rows), M_FLOOR, F32), jnp.zeros((KV_LANES, rows), F32)) for _ in range(KV_HEADS))
    n_steps = functools.reduce(jnp.maximum, cnts) - 1
    carry = lax.fori_loop(0, n_steps, skip_step, init)
    last = (t0 + tq + tk - 1) // tk - 1
    carry = sel_step([last] * KV_HEADS, [None] * KV_HEADS, carry, True)

    pieces = []
    for hk in range(KV_HEADS):
        acc = carry[hk][1]
        o_s = (acc[:HEAD_DIM] / jnp.maximum(acc[HEAD_DIM:HEAD_DIM + 1], 1e-30)).T
        for g in range(GROUP):
            c0 = (hk * GROUP + g) * 3
            cs = slice((hk * GROUP + g) * HEAD_DIM, (hk * GROUP + g + 1) * HEAD_DIM)
            rs = slice(g * tq, (g + 1) * tq)
            pieces.append(gt[:, c0:c0 + 1] * oc_ref[:, cs] + gt[:, c0 + 1:c0 + 2] * o_s[rs]
                          + gt[:, c0 + 2:c0 + 3] * o_ws[hk][rs])
    o_ref[...] = jnp.concatenate(pieces, axis=-1)


def _nsa_prompt(q, gates, kvc, ks, kw, tq, tk):
    t = q.shape[0]
    nbc = kvc.shape[0]
    bpt = tk // BLK
    max_tiles = nbc // bpt
    assert t <= POS_SPLIT * 256 and AUG_BLK + bpt <= KV_LANES and t % tk == 0
    oc, not_sel, blk_any = _nsa_select(q, kvc, tq)
    tile_any = jnp.max(blk_any.reshape(t // tq, KV_HEADS, max_tiles, bpt), axis=-1) > 0.0
    rank = jnp.cumsum(tile_any.astype(jnp.int32), axis=-1) - 1
    slot = jnp.arange(max_tiles, dtype=jnp.int32)
    hit = tile_any[..., :, None] & (rank[..., :, None] == slot)
    lst = jnp.sum(jnp.where(hit, slot[:, None], 0), axis=-2).astype(jnp.int32).reshape(-1)
    cnt = jnp.sum(tile_any, axis=-1).astype(jnp.int32).reshape(-1)
    em = jnp.where(jnp.arange(KV_LANES)[None, :] == AUG_BLK + jnp.arange(nbc)[:, None] % bpt, NEG, 0.0).astype(BF16)
    pos = jnp.arange(t, dtype=jnp.int32)[:, None]
    aug_lane = jnp.arange(HEAD_DIM, dtype=jnp.int32)[None, :] + HEAD_DIM
    k_aug = (jnp.where(aug_lane == AUG_HI, pos // POS_SPLIT, 0) + jnp.where(aug_lane == AUG_LO, pos % POS_SPLIT, 0)
             + jnp.where(aug_lane == AUG_ONE, 1, 0)
             + jnp.where(aug_lane == AUG_BLK + (pos // BLK) % bpt, 1, 0)).astype(BF16)
    v_aug = jnp.broadcast_to(jnp.where(aug_lane == HEAD_DIM, 1, 0).astype(BF16), (t, HEAD_DIM))
    ka = jnp.stack([jnp.concatenate([ks[:, h * KV_LANES:h * KV_LANES + HEAD_DIM], k_aug], axis=1)
                    for h in range(KV_HEADS)])
    va = jnp.stack([jnp.concatenate([ks[:, h * KV_LANES + HEAD_DIM:(h + 1) * KV_LANES], v_aug], axis=1)
                    for h in range(KV_HEADS)])
    va = jnp.transpose(va.reshape(KV_HEADS, t // tk, tk, KV_LANES), (0, 1, 3, 2))
    vmem = 3 * t * C_KV * 2 + 24 * GROUP * tq * max(tk, WINDOW + tq) * 4 + (8 << 20)
    tile = lambda w: pl.BlockSpec((tq, w), lambda i, c, l: (i, 0))
    const = lambda z: pl.BlockSpec(z.shape, lambda i, c, l: (0,) * z.ndim, pipeline_mode=pl.Buffered(1))
    grid_spec = pltpu.PrefetchScalarGridSpec(
        num_scalar_prefetch=2,
        grid=(t // tq,),
        in_specs=[tile(C_B), tile(GATE_PAD), tile(C_B), tile(KV_HEADS * nbc), const(ka), const(va), const(kw), const(em)],
        out_specs=tile(C_B),
    )
    return pl.pallas_call(
        functools.partial(_nsa_prompt_body, tq=tq, tk=tk, nbc=nbc, max_tiles=max_tiles),
        grid_spec=grid_spec,
        out_shape=jax.ShapeDtypeStruct((t, C_B), F32),
        compiler_params=pltpu.CompilerParams(dimension_semantics=("arbitrary",), vmem_limit_bytes=_vmem_limit(vmem)),
        name="nsa_prompt",
    )(cnt, lst, q, gates, oc, not_sel, ka, va, kw, em)


PAGE_BLOCKS = 2
Q_ROWS = SUBLANES


def _samp_cmp_body(q_ref, kc_ref, oc_ref, sc_ref, *, n_pages, past):
    kbuf = kc_ref.at[0]
    nb_past = n_pages * PAGE_BLOCKS
    n_all = nb_past + 1
    qpos = past
    cur = qpos // BLK
    width = PAGE_BLOCKS * n_pages
    lane = lax.broadcasted_iota(jnp.int32, (1, width), 1)
    bid = jnp.where(lane < n_pages, PAGE_BLOCKS * lane, PAGE_BLOCKS * (lane - n_pages) + 1)
    cpos = bid * BLK + (BLK - 1)
    bid2 = jnp.concatenate([bid, nb_past + lane], axis=-1)
    scores = []
    for hk in range(KV_HEADS):
        q8 = _head_rows(q_ref.at[0], hk, pad_rows=Q_ROWS - GROUP, pad_lanes=False)
        z8 = jnp.zeros_like(q8)
        slope = _slope_col(hk, 1, pad_rows=Q_ROWS - GROUP)
        kmat = kbuf[:, hk * C_KV:hk * C_KV + KV_LANES].astype(BF16)
        vmat = kbuf[:, hk * C_KV + KV_LANES:(hk + 1) * C_KV].astype(BF16)
        s = jnp.concatenate([_dot_nt(jnp.concatenate([q8, z8], axis=-1), kmat),
                             _dot_nt(jnp.concatenate([z8, q8], axis=-1), kmat)], axis=-1)
        s = s - slope * (qpos - cpos).astype(F32)
        p_c = _masked_softmax(s, cpos <= qpos)
        o_even = _dot(p_c[:, :n_pages].astype(BF16), vmat)
        o_odd = _dot(p_c[:, n_pages:].astype(BF16), vmat)
        oc_ref[0, hk] = o_even[:, :HEAD_DIM] + o_odd[:, HEAD_DIM:]
        score = jnp.sum(p_c[0:GROUP], axis=0, keepdims=True)
        sc = jnp.concatenate([score, jnp.zeros((1, width), F32)], axis=-1)
        forced = (bid2 == 0) | (bid2 == cur) | (bid2 == cur - 1)
        sc = jnp.where(forced, FORCE_SCORE, sc)
        sc = jnp.where(bid2 > cur, -1.0, sc)
        scores.append(jnp.where(bid2 < n_all, sc, -jnp.inf))
    sc_ref[0] = jnp.concatenate(scores, axis=0)


def _samp_cmp(q3, kc3, past):
    db, n_pages, _ = kc3.shape
    wide = 2 * PAGE_BLOCKS * n_pages
    return pl.pallas_call(
        functools.partial(_samp_cmp_body, n_pages=n_pages, past=past),
        grid=(db,),
        in_specs=[pl.BlockSpec((1, 1, C_B), lambda n: (n, 0, 0)),
                  pl.BlockSpec((1, n_pages, KV_HEADS * C_KV), lambda n: (n, 0, 0))],
        out_specs=[pl.BlockSpec((1, KV_HEADS, Q_ROWS, HEAD_DIM), lambda n: (n, 0, 0, 0)),
                   pl.BlockSpec((1, KV_HEADS, wide), lambda n: (n, 0, 0))],
        out_shape=[jax.ShapeDtypeStruct((db, KV_HEADS, Q_ROWS, HEAD_DIM), F32),
                   jax.ShapeDtypeStruct((db, KV_HEADS, wide), F32)],
        compiler_params=pltpu.CompilerParams(dimension_semantics=("arbitrary",)),
        name="nsa_sample_cmp",
    )(q3, kc3)


def _samp_topk_body(sc_ref, idx_ref, *, n_pages, nb_past):
    s_t = sc_ref[...].T
    npos, nrow = s_t.shape
    width = PAGE_BLOCKS * n_pages
    pos = lax.broadcasted_iota(jnp.int32, (npos, 1), 0)
    ids = jnp.where(pos < n_pages, PAGE_BLOCKS * pos,
                    jnp.where(pos < width, PAGE_BLOCKS * (pos - n_pages) + 1, nb_past + pos - width)).astype(F32)
    picks = []
    for _ in range(N_SEL):
        m = jnp.max(s_t, axis=0, keepdims=True)
        pick = jnp.min(jnp.where(s_t == m, ids, 1e9), axis=0, keepdims=True)
        picks.append(pick)
        s_t = jnp.where(ids == pick, -jnp.inf, s_t)
    res = jnp.concatenate(picks + [jnp.zeros((LANES - N_SEL, nrow), F32)], axis=0)
    idx_ref[...] = res.T.astype(jnp.int32)


def _samp_topk(scores2, n_pages):
    rows = scores2.shape[0]
    return pl.pallas_call(
        functools.partial(_samp_topk_body, n_pages=n_pages, nb_past=n_pages * PAGE_BLOCKS),
        out_shape=jax.ShapeDtypeStruct((rows, LANES), jnp.int32),
        name="nsa_sample_topk",
    )(scores2)


def _samp_sel_body(pt_ref, idx_ref, q_ref, gt_ref, oc_ref, ksc_ref, kwr_ref, kwc_ref, win_ref, cache_ref,
                   o_ref, wino_ref, sbuf, sems, *, nb_past, past, wb, db):
    n = pl.program_id(0)
    qpos = past

    page_tok = PAGE_BLOCKS * BLK
    tok = lax.broadcasted_iota(jnp.int32, (HEAD_DIM, page_tok), 1)

    buf = n % 2

    def page_copy(sample, b, hk, s, ib):
        page = pt_ref[sample, ib // PAGE_BLOCKS]
        return pltpu.make_async_copy(cache_ref.at[page, hk], sbuf.at[b, hk, :, :, pl.ds(s * page_tok, page_tok)],
                                     sems.at[b, hk * N_SEL + s])

    def start_pages(sample, b):
        for hk in range(KV_HEADS):
            for s in range(N_SEL):
                ib = idx_ref[sample, hk * N_SEL + s]

                @pl.when(ib < nb_past)
                def _():
                    page_copy(sample, b, hk, s, ib).start()

    @pl.when(n == 0)
    def _():
        start_pages(0, 0)

    @pl.when(n + 1 < db)
    def _():
        start_pages(n + 1, 1 - buf)

    for hk in range(KV_HEADS):
        for s in range(N_SEL):
            ib = idx_ref[n, hk * N_SEL + s]

            @pl.when(ib >= nb_past)
            def _():
                for c in range(2):
                    sbuf[buf, hk, c, :, s * page_tok:(s + 1) * page_tok] = jnp.where(tok == 0, ksc_ref[0, 2 * hk + c],
                                                                                    0.0)

    wtok = lax.broadcasted_iota(jnp.int32, (HEAD_DIM, wb), 1)
    for hk in range(KV_HEADS):
        for c in range(2):
            wino_ref[0, hk, c] = jnp.where(wtok == wb - 1, kwc_ref[0, 2 * hk + c],
                                           pltpu.roll(win_ref[0, hk, c], wb - 1, 1))

    for hk in range(KV_HEADS):
        for s in range(N_SEL):
            ib = idx_ref[n, hk * N_SEL + s]

            @pl.when(ib < nb_past)
            def _():
                page_copy(n, buf, hk, s, ib).wait()

    gt = jax.nn.sigmoid(gt_ref[0])
    nk = N_SEL * page_tok
    lane = lax.broadcasted_iota(jnp.int32, (1, nk), 1)
    kposw = past - wb + lax.broadcasted_iota(jnp.int32, (1, wb), 1)
    wmask = (kposw <= qpos) & (qpos - kposw < WINDOW) & (kposw >= 0)
    pieces = []
    for hk in range(KV_HEADS):
        q8 = _head_rows(q_ref.at[0], hk, pad_rows=Q_ROWS - GROUP, pad_lanes=False)
        slope = _slope_col(hk, 1, pad_rows=Q_ROWS - GROUP)
        ibv = jnp.zeros((1, nk), jnp.int32)
        for s in range(N_SEL):
            ibv = jnp.where(lane // page_tok == s, idx_ref[n, hk * N_SEL + s], ibv)
        spos = (ibv - ibv % PAGE_BLOCKS) * BLK + lane % page_tok
        s_s = _dot(q8, sbuf[buf, hk, 0].astype(BF16)) - slope * (qpos - spos).astype(F32)
        p_s = _masked_softmax(s_s, (spos // BLK == ibv) & (spos <= qpos))
        o_s = _dot_nt(p_s.astype(BF16), sbuf[buf, hk, 1].astype(BF16))
        kwn = kwr_ref[0]
        k_new = kwn[:, hk * KV_LANES:hk * KV_LANES + HEAD_DIM]
        v_new = kwn[:, hk * KV_LANES + HEAD_DIM:(hk + 1) * KV_LANES]
        s_w = _dot(q8, win_ref[0, hk, 0].astype(BF16)) - slope * (qpos - kposw).astype(F32)
        s_w = jnp.where(wmask, s_w, NEG)
        s_n = jnp.sum(q8.astype(F32) * k_new, axis=-1, keepdims=True)
        m_w = jnp.maximum(jnp.max(s_w, axis=-1, keepdims=True), s_n)
        e_w = jnp.exp(s_w - m_w)
        e_n = jnp.exp(s_n - m_w)
        den = jnp.sum(e_w, axis=-1, keepdims=True) + e_n
        o_w = (_dot_nt(e_w.astype(BF16), win_ref[0, hk, 1].astype(BF16)) + e_n * v_new) / den
        o_c = oc_ref[0, hk]
        for g in range(GROUP):
            c0 = (hk * GROUP + g) * 3
            pieces.append(gt[:, c0:c0 + 1] * o_c[g:g + 1] + gt[:, c0 + 1:c0 + 2] * o_s[g:g + 1]
                          + gt[:, c0 + 2:c0 + 3] * o_w[g:g + 1])
    o_ref[0] = jnp.concatenate(pieces, axis=-1)


def _samp_sel(page_table, idx, q3, gates3, oc, ks_cols, kw_row, kw_cols, win_t, cache_t, past):
    db, n_pages = page_table.shape
    wb = win_t.shape[-1]
    nb_past = n_pages * PAGE_BLOCKS
    row3 = lambda w: pl.BlockSpec((1, 1, w), lambda n, pt, ix: (n, 0, 0))
    col4 = pl.BlockSpec((1, 2 * KV_HEADS, HEAD_DIM, 1), lambda n, pt, ix: (n, 0, 0, 0))
    win_spec = pl.BlockSpec((1, KV_HEADS, 2, HEAD_DIM, wb), lambda n, pt, ix: (n, 0, 0, 0, 0))
    grid_spec = pltpu.PrefetchScalarGridSpec(
        num_scalar_prefetch=2,
        grid=(db,),
        in_specs=[row3(C_B), row3(GATE_PAD),
                  pl.BlockSpec((1, KV_HEADS, Q_ROWS, HEAD_DIM), lambda n, pt, ix: (n, 0, 0, 0)),
                  col4, row3(C_KV), col4, win_spec, pl.BlockSpec(memory_space=pl.ANY)],
        out_specs=[row3(C_B), win_spec],
        scratch_shapes=[pltpu.VMEM((2, KV_HEADS, 2, HEAD_DIM, N_SEL * PAGE_BLOCKS * BLK), F32),
                        pltpu.SemaphoreType.DMA((2, KV_HEADS * N_SEL))],
    )
    return pl.pallas_call(
        functools.partial(_samp_sel_body, nb_past=nb_past, past=past, wb=wb, db=db),
        grid_spec=grid_spec,
        out_shape=[jax.ShapeDtypeStruct((db, 1, C_B), F32), jax.ShapeDtypeStruct(win_t.shape, F32)],
        compiler_params=pltpu.CompilerParams(dimension_semantics=("arbitrary",)),
        name="nsa_sample_sel",
    )(page_table, idx, q3, gates3, oc, ks_cols, kw_row, kw_cols, win_t, cache_t)


FF_CHUNK = 1408


def _merge_body(x_ref, y_ref, bonus_ref, g_ref, yb_ref, mg_ref, p_ref, cbuf_ref,
                lng_ref, lnb_ref, bd_ref, woa_ref, wob_ref, wout_ref, n2_ref, wup_ref, cw_ref, cb_ref, wdn_ref,
                n3_ref, wpe_ref, wpg_ref, fg_ref, o_ref, cnew_ref, carry_ref, *, seq_mode, final, d_ff, tm):
    i = pl.program_id(0)
    d_model = x_ref.shape[1]
    y = y_ref[...]
    inv = 1.0 / A_HEAD_DIM
    mean = _dot_ones(y, bd_ref[...]) * inv
    d = y - mean
    var = _dot_ones(d * d, bd_ref[...]) * inv
    ya = (d * lax.rsqrt(var + LNX_EPS) * lng_ref[...] + lnb_ref[...] + bonus_ref[...]) * g_ref[...]
    mg = mg_ref[...]
    m = (jax.nn.sigmoid(mg[:, :d_model]) * _dot(ya.astype(BF16), woa_ref[...])
         + jax.nn.sigmoid(mg[:, d_model:]) * _dot(yb_ref[...].astype(BF16), wob_ref[...]))
    h = x_ref[...] + _dot(m.astype(BF16), wout_ref[...])
    xn = _rms(h, n2_ref[...]).astype(BF16)

    if seq_mode:
        @pl.when(i == 0)
        def _():
            carry_ref[...] = jnp.zeros_like(carry_ref)
        rid = lax.broadcasted_iota(jnp.int32, (tm, FF_CHUNK), 0)

    acc = jnp.zeros((tm, d_model), F32)
    for c in range(d_ff // FF_CHUNK):
        parts = []
        for half in range(2):
            cs = slice(half * d_ff + c * FF_CHUNK, half * d_ff + (c + 1) * FF_CHUNK)
            up = _dot(xn, wup_ref[:, cs])
            if seq_mode:
                t1 = carry_ref[SUBLANES - 1:SUBLANES, cs]
                t2 = carry_ref[SUBLANES - 2:SUBLANES - 1, cs]
                up1 = jnp.where(rid == 0, t1, pltpu.roll(up, 1, 0))
                up2 = jnp.where(rid == 0, t2, jnp.where(rid == 1, t1, pltpu.roll(up, 2, 0)))
                carry_ref[:, cs] = up[tm - SUBLANES:, :]
            else:
                up2 = cbuf_ref[:, cs]
                up1 = cbuf_ref[:, 2 * d_ff + cs.start:2 * d_ff + cs.stop]
                cnew_ref[:, cs] = up1
                cnew_ref[:, 2 * d_ff + cs.start:2 * d_ff + cs.stop] = up
            parts.append(cb_ref[:, cs] + cw_ref[0:1, cs] * up2 + cw_ref[1:2, cs] * up1 + cw_ref[2:3, cs] * up)
        a, gate = parts
        act = (a * jax.nn.sigmoid(a) * gate).astype(BF16)
        acc = acc + _dot(act, wdn_ref[c * FF_CHUNK:(c + 1) * FF_CHUNK, :])
    if seq_mode:
        cnew_ref[...] = carry_ref[...]
    h = h + acc
    pe = _dot(p_ref[...].astype(BF16), wpe_ref[...])
    h = h + pe * jax.nn.sigmoid(_dot(_rms(h, n3_ref[...]).astype(BF16), wpg_ref[...]))
    o_ref[...] = _rms(h, fg_ref[...]) if final else h


def _merge(x, y, bonus, g, yb, mg, p, cbuf, consts, tm, seq_mode, final):
    rows, d_model = x.shape
    d_ff = consts[10].shape[0]
    f2 = 2 * d_ff
    rowspec = lambda w: pl.BlockSpec((tm, w), lambda i: (i, 0))
    if seq_mode:
        cbuf_spec = _const_spec(cbuf.shape)
        cnew_shape, cnew_spec = (SUBLANES, f2), pl.BlockSpec((SUBLANES, f2), lambda i: (0, 0))
    else:
        cbuf_spec = rowspec(2 * f2)
        cnew_shape, cnew_spec = (rows, 2 * f2), rowspec(2 * f2)
    wbytes = sum(int(c.size) * c.dtype.itemsize for c in consts)
    act = tm * (d_model * 3 + C_A * 4 + 256) * 4 + (0 if seq_mode else 2 * tm * 2 * f2 * 4)
    vmem = wbytes + 2 * act + 8 * tm * d_model * 4 + (8 << 20)
    return pl.pallas_call(
        functools.partial(_merge_body, seq_mode=seq_mode, final=final, d_ff=d_ff, tm=tm),
        grid=(rows // tm,),
        in_specs=[rowspec(d_model), rowspec(C_A), rowspec(C_A), rowspec(C_A), rowspec(C_B), rowspec(2 * d_model),
                  rowspec(p.shape[1]), cbuf_spec] + [_const_spec(c.shape) for c in consts],
        out_specs=[rowspec(d_model), cnew_spec],
        out_shape=[jax.ShapeDtypeStruct((rows, d_model), F32), jax.ShapeDtypeStruct(cnew_shape, F32)],
        scratch_shapes=[pltpu.VMEM((SUBLANES, f2), F32)],
        compiler_params=pltpu.CompilerParams(dimension_semantics=("arbitrary",), vmem_limit_bytes=_vmem_limit(vmem)),
        name="merge_ffn",
    )(x, y, bonus, g, yb, mg, p, cbuf, *consts)


def _pick_tile(n, target):
    t = min(n, target)
    while n % t:
        t //= 2
    return t


def _head_block_diag():
    h = jnp.arange(C_A) // A_HEAD_DIM
    return (h[:, None] == h[None, :]).astype(BF16)


def kernel(x_prompt, x_sample, p_prompt, p_sample, cache_cmp_kv, cache_sel_kv, page_table, state_win_kv, state_wkv, state_shift, state_ffn_conv, norm1_g, w_in, shift_mu, rwkv_w0, rwkv_w2, rwkv_a0, rwkv_a2, rwkv_g2, rwkv_k_k, rwkv_k_a, rwkv_r_k, lnx_g, lnx_b, cmp_pe, cmp_w1, cmp_b1, cmp_w2, w_oa, w_ob, w_out, norm2_g, w_up, conv_w, conv_b, w_down, norm3_g, w_pe, w_pg, final_g):
    depth = w_in.shape[0]
    b, t, d_model = x_prompt.shape
    db, dt, _ = x_sample.shape
    n_pool, page = cache_cmp_kv.shape[1], cache_cmp_kv.shape[2]
    n_pages = page_table.shape[1]
    past = n_pages * page
    d_ff = w_down.shape[1]
    f2 = 2 * d_ff
    wb = state_win_kv.shape[2]
    assert b == 1 and dt == 1 and page == PAGE_BLOCKS * BLK
    assert t % 512 == 0 and t >= WINDOW + 256 and d_ff % FF_CHUNK == 0 and wb == WINDOW and past >= WINDOW
    assert N_SEL <= LANES and n_pages * PAGE_BLOCKS + 1 >= N_SEL

    bd = _head_block_diag()
    hp = x_prompt.reshape(t, d_model)
    hs = x_sample.reshape(db, d_model)
    outs = [[] for _ in range(12)]
    for i in range(depth):
        o_g = C_SHIFT + C_B + 3 * C_KV
        w_perm = jnp.concatenate(
            [w_in[i][:, :o_g], jnp.pad(w_in[i][:, o_g:o_g + 3 * Q_HEADS], ((0, 0), (0, GATE_PAD - 3 * Q_HEADS))),
             w_in[i][:, o_g + 3 * Q_HEADS:]], axis=1).astype(BF16)
        g1 = norm1_g[i].reshape(1, -1)
        rw = (shift_mu[i], rwkv_w0[i], rwkv_w2[i], rwkv_a0[i], rwkv_a2[i], rwkv_g2[i], rwkv_k_k[i], rwkv_k_a[i],
              rwkv_r_k[i].reshape(-1))
        pe_flat = jnp.pad(cmp_pe[i].reshape(1, -1), ((0, SUBLANES - 1), (0, 0)))
        w1_rows = cmp_w1[i].reshape(2 * BLK * HEAD_DIM, D_CMP).astype(BF16)
        eye_b = jnp.eye(PAGE_BLOCKS, dtype=F32)
        w_tok = jnp.einsum('crdf,eg->cdergf', cmp_w1[i], eye_b).reshape(2 * HEAD_DIM * PAGE_BLOCKS * BLK,
                                                                        PAGE_BLOCKS * D_CMP).astype(BF16)
        w2_pair = jnp.einsum('cfd,eg->cefgd', cmp_w2[i], eye_b).reshape(2, PAGE_BLOCKS * D_CMP,
                                                                       PAGE_BLOCKS * HEAD_DIM).astype(BF16)
        row = lambda z: z.reshape(1, -1)
        mconsts = [row(lnx_g[i]), row(lnx_b[i]), bd, w_oa[i].astype(BF16), w_ob[i].astype(BF16),
                   w_out[i].astype(BF16), row(norm2_g[i]), w_up[i].astype(BF16), conv_w[i], row(conv_b[i]),
                   w_down[i].astype(BF16), row(norm3_g[i]), w_pe[i].astype(BF16), w_pg[i].astype(BF16), row(final_g)]
        final = i == depth - 1

        pj = _proj(hp, g1, w_perm, _pick_tile(t, 256), False)
        pa, q, kvc, kvs16, kvw16, gates, mg = (pj[k_] for k_ in ("pa", "q", "kvc", "kvs16", "kvw16", "gates", "mg"))
        seqs = _rwkv_prep(pa, None, rw, bd, _pick_tile(t, 512), True)
        r_, w_, lw_, k_, v_, a_, b_, g_, bonus = seqs
        y, s_new = _rwkv_chunk_scan([z.reshape(1, t, C_A) for z in (r_, lw_, k_, v_, a_, b_)],
                                    jnp.zeros((1, A_HEADS, A_HEAD_DIM, A_HEAD_DIM), F32), _pick_tile(t, 512))
        kvc_blocks, cmp_bias = _compress(kvc, pe_flat, w1_rows, cmp_b1[i], cmp_w2[i].astype(BF16),
                                         _pick_tile(t // BLK, 64))
        bias_pair = jnp.tile(cmp_bias[0].reshape(2, D_CMP), (1, PAGE_BLOCKS))
        yb = _nsa_prompt(q, gates, kvc_blocks, kvs16, kvw16, 256, 512)
        hp, conv_new = _merge(hp, y.reshape(t, C_A), bonus, g_, yb, mg, p_prompt[i].reshape(t, -1),
                              jnp.zeros((SUBLANES, LANES), F32), mconsts, _pick_tile(t, 256), True, final)
        kv6 = lambda z_t, n_: jnp.transpose(z_t.reshape(KV_HEADS, 2, HEAD_DIM, z_t.shape[1]),
                                            (3, 0, 1, 2)).reshape(n_, -1, KV_HEADS, 2, HEAD_DIM)
        outs[0].append(kv6(pj["kvc_t"], 1))
        outs[2].append(kv6(pj["kvs_t"], 1))
        outs[4].append(kv6(pj["kvw_t"][:, t - min(WINDOW, t):], 1))
        outs[6].append(s_new)
        outs[8].append(pa[t - 1:t])
        outs[10].append(conv_new[SUBLANES - (CONV_W - 1):].reshape(1, CONV_W - 1, f2))

        pj = _proj(hs, g1, w_perm, _pick_tile(db, 128), True)
        pa, q, kvs, kvw, gates, mg = (pj[k_] for k_ in ("pa", "q", "kvs", "kvw", "gates", "mg"))
        seqs = _rwkv_prep(pa, state_shift[i], rw, bd, _pick_tile(db, 128), False)
        r_, w_, lw_, k_, v_, a_, b_, g_, bonus = seqs
        y, s_new = _rwkv_scan([z.reshape(db, 1, C_A) for z in (r_, w_, k_, v_, a_, b_)], state_wkv[i], 1,
                              _pick_tile(db, SUBLANES))
        to_tok_minor = lambda z: jnp.transpose(z, (0, 2, 3, 4, 1))
        kc_pages = _compress_pool(page_table, to_tok_minor(cache_cmp_kv[i]).reshape(n_pool, KV_HEADS, 2 * HEAD_DIM, page),
                                  w_tok, bias_pair, w2_pair)
        q3 = q.reshape(db, 1, C_B)
        oc, sel_scores = _samp_cmp(q3, kc_pages.reshape(db, n_pages, KV_HEADS * C_KV), past)
        idx = _samp_topk(sel_scores.reshape(db * KV_HEADS, -1), n_pages)
        idx2 = idx[:, :N_SEL].reshape(db, KV_HEADS * N_SEL)
        cols = lambda z: z.reshape(db, 2 * KV_HEADS, HEAD_DIM, 1)
        yb, win_new_t = _samp_sel(page_table, idx2, q3, gates.reshape(db, 1, GATE_PAD), oc, cols(kvs),
                                  kvw.reshape(db, 1, C_KV), cols(kvw), to_tok_minor(state_win_kv[i]),
                                  to_tok_minor(cache_sel_kv[i]), past)
        win_new = jnp.transpose(win_new_t, (0, 4, 1, 2, 3))
        hs, conv_new = _merge(hs, y.reshape(db, C_A), bonus, g_, yb.reshape(db, C_B), mg, p_sample[i].reshape(db, -1),
                              state_ffn_conv[i].reshape(db, 2 * f2), mconsts, _pick_tile(db, 128), False, final)
        outs[1].append(kv6(pj["kvc_t"], db))
        outs[3].append(kv6(pj["kvs_t"], db))
        outs[5].append(win_new.reshape(db, wb, KV_HEADS, 2, HEAD_DIM))
        outs[7].append(s_new)
        outs[9].append(pa)
        outs[11].append(conv_new.reshape(db, CONV_W - 1, f2))

    stacked = [jnp.stack(o) for o in outs]
    return (hp.reshape(b, t, d_model), hs.reshape(db, dt, d_model), *stacked)
```

```python
import functools

import jax
import jax.numpy as jnp
from jax import lax
from jax.experimental import pallas as pl
from jax.experimental.pallas import tpu as pltpu

F32 = jnp.float32
BF16 = jnp.bfloat16
HI = lax.Precision.HIGHEST

A_HEADS = 8
A_HEAD_DIM = 64
C_A = A_HEADS * A_HEAD_DIM
R_W = 64
R_A = 64
R_G = 128
C_SHIFT = 3 * C_A + R_W + R_A + R_G
LNX_EPS = 64e-5
Q_HEADS = 8
KV_HEADS = 2
GROUP = Q_HEADS // KV_HEADS
HEAD_DIM = 64
C_B = Q_HEADS * HEAD_DIM
C_KV = KV_HEADS * 2 * HEAD_DIM
BLK = 64
N_SEL = 16
WINDOW = 512
D_CMP = 128
FORCE_SCORE = 1e4
CONV_W = 3
NORM_EPS = 1e-6

LANES = 128
SUBLANES = 8
VMEM_BYTES_V7X = 64 * 1024 * 1024

NEG = -1e30
M_FLOOR = -1e29
KV_LANES = 2 * HEAD_DIM

GATE_PAD = LANES


def _vmem_limit(nbytes):
    return int(min(max(nbytes, 16 * 1024 * 1024), VMEM_BYTES_V7X - 8 * 1024 * 1024))


def _const_spec(shape):
    nd = len(shape)
    return pl.BlockSpec(shape, lambda *_: (0,) * nd, pipeline_mode=pl.Buffered(1))


def _rms(x, g):
    return x * lax.rsqrt(jnp.mean(x * x, axis=-1, keepdims=True) + NORM_EPS) * g


def _dot(a, b, **kw):
    return jnp.dot(a, b, preferred_element_type=F32, **kw)


def _dot_ones(x, ones):
    hi = x.astype(BF16)
    lo = (x - hi.astype(F32)).astype(BF16)
    return _dot(hi, ones) + _dot(lo, ones)


def _dot_nt(a, b):
    return lax.dot_general(a, b, (((1,), (1,)), ((), ())), preferred_element_type=F32)


def _masked_softmax(s, mask):
    s = jnp.where(mask, s, NEG)
    m = jnp.max(s, axis=-1, keepdims=True)
    m = jnp.where(m > M_FLOOR, m, 0.0)
    e = jnp.exp(s - m)
    return e / jnp.maximum(jnp.sum(e, axis=-1, keepdims=True), 1e-30)


def _slope_col(hk, rows_per_head, pad_rows=0):
    cols = [jnp.full((rows_per_head, 1), 2.0 ** (-(hk * GROUP + g + 1)), F32) for g in range(GROUP)]
    if pad_rows:
        cols.append(jnp.zeros((pad_rows, 1), F32))
    return jnp.concatenate(cols, axis=0)


def _proj_body(x_ref, g_ref, w_ref, *o_refs, segs):
    xb = _rms(x_ref[...], g_ref[...]).astype(BF16)
    it = iter(o_refs)
    for off, width, outs in segs:
        r = _dot(xb, w_ref[:, off:off + width])
        for scale, transposed in outs:
            o_ref = next(it)
            val = r * scale if scale != 1.0 else r
            o_ref[...] = (val.T if transposed else val).astype(o_ref.dtype)


def _proj(x, g, w_perm, tm, kv_rows):
    rows, d = x.shape
    d_model = d
    kv = lambda name: ([(name, F32, 1.0, False)] if kv_rows or name == "kvc" else []) + \
        [(name + "16", BF16, 1.0, False), (name + "_t", F32, 1.0, True)]
    groups = ((C_SHIFT, [("pa", F32, 1.0, False)]), (C_B, [("q", BF16, HEAD_DIM ** -0.5, False)]),
              (C_KV, kv("kvc")), (C_KV, kv("kvs")), (C_KV, kv("kvw")),
              (GATE_PAD, [("gates", F32, 1.0, False)]), (2 * d_model, [("mg", F32, 1.0, False)]))
    o = 0
    segs, names, shapes, specs = [], [], [], []
    for width, outs in groups:
        segs.append((o, width, tuple((s, tr) for _, _, s, tr in outs)))
        for name, dt, _, tr in outs:
            names.append(name)
            shapes.append(jax.ShapeDtypeStruct((width, rows) if tr else (rows, width), dt))
            specs.append(pl.BlockSpec((width, tm), lambda i: (0, i)) if tr else pl.BlockSpec((tm, width), lambda i: (i, 0)))
        o += width
    n_tot = o
    out_bytes = sum(tm * s.shape[0 if s.shape[1] == rows else 1] * s.dtype.itemsize for s in shapes)
    vmem = 2 * tm * d * 4 + d * n_tot * 2 + 2 * out_bytes + (8 << 20)
    outs = pl.pallas_call(
        functools.partial(_proj_body, segs=tuple(segs)),
        grid=(rows // tm,),
        in_specs=[pl.BlockSpec((tm, d), lambda i: (i, 0)), _const_spec((1, d)), _const_spec((d, n_tot))],
        out_specs=specs,
        out_shape=shapes,
        compiler_params=pltpu.CompilerParams(dimension_semantics=("arbitrary",), vmem_limit_bytes=_vmem_limit(vmem)),
        name="proj",
    )(x, g, w_perm)
    return dict(zip(names, outs))


def _prep_body(pa_ref, prev_ref, mu_ref, w0_ref, w2_ref, a0_ref, a2_ref, g2_ref, kk_ref, ka_ref, rk_ref, bd_ref,
               r_o, w_o, lw_o, k_o, v_o, a_o, b_o, g_o, bonus_o, *, seq_mode):
    pf = pa_ref[...]
    if seq_mode:
        i = pl.program_id(0)
        first = jnp.where(i > 0, prev_ref[SUBLANES - 1:SUBLANES, :], 0.0)
        rid = lax.broadcasted_iota(jnp.int32, pf.shape, 0)
        prev = jnp.where(rid == 0, first, pltpu.roll(pf, 1, 0))
    else:
        prev = prev_ref[...]
    xs = pf + (prev - pf) * mu_ref[...]
    r = xs[:, 0:C_A]
    k = xs[:, C_A:2 * C_A]
    v = xs[:, 2 * C_A:3 * C_A]
    o = 3 * C_A
    wd = xs[:, o:o + R_W]
    ad = xs[:, o + R_W:o + R_W + R_A]
    gd = xs[:, o + R_W + R_A:]
    nz = -(w0_ref[...] + _dot(jnp.tanh(wd), w2_ref[...], precision=HI))
    softplus = jnp.maximum(nz, 0.0) + jnp.log(1.0 + jnp.exp(-jnp.abs(nz)))
    w = -softplus - 0.5
    a = jax.nn.sigmoid(a0_ref[...] + _dot(ad, a2_ref[...], precision=HI))
    g = _dot(jax.nn.sigmoid(gd), g2_ref[...], precision=HI)
    kk = k * kk_ref[...]
    ss = _dot_ones(kk * kk, bd_ref[...])
    kkn = kk / jnp.maximum(jnp.sqrt(ss), 1e-12)
    k2 = k * (1.0 + (a - 1.0) * ka_ref[...])
    r_o[...] = r
    lw = -jnp.exp(w)
    lw_o[...] = lw
    w_o[...] = jnp.exp(lw)
    k_o[...] = k2
    v_o[...] = v
    a_o[...] = -kkn
    b_o[...] = kkn * a
    g_o[...] = g
    bonus_o[...] = _dot_ones(r * k2 * rk_ref[...], bd_ref[...]) * v


def _rwkv_prep(pa, prev, rw, bd, tm, seq_mode):
    rows = pa.shape[0]
    mu, w0, w2, a0, a2, g2, k_k, k_a, r_k = rw
    if seq_mode:
        tb = tm // SUBLANES
        prev_spec = pl.BlockSpec((SUBLANES, C_SHIFT), lambda i: (jnp.maximum(i * tb - 1, 0), 0))
        prev = pa
    else:
        prev_spec = pl.BlockSpec((tm, C_SHIFT), lambda i: (i, 0))
    row = lambda z: z.reshape(1, -1)
    consts = [row(mu), row(w0), w2, row(a0), a2, g2, row(k_k), row(k_a), row(r_k), bd]
    vmem = 4 * tm * C_SHIFT * 4 + 2 * 8 * tm * C_A * 4 + 16 * tm * C_A * 4 + (8 << 20)
    return pl.pallas_call(
        functools.partial(_prep_body, seq_mode=seq_mode),
        grid=(rows // tm,),
        in_specs=[pl.BlockSpec((tm, C_SHIFT), lambda i: (i, 0)), prev_spec] + [_const_spec(c.shape) for c in consts],
        out_specs=[pl.BlockSpec((tm, C_A), lambda i: (i, 0))] * 9,
        out_shape=[jax.ShapeDtypeStruct((rows, C_A), F32)] * 9,
        compiler_params=pltpu.CompilerParams(dimension_semantics=("arbitrary",), vmem_limit_bytes=_vmem_limit(vmem)),
        name="rwkv_prep",
    )(pa, prev, *consts)


N_PAIR = A_HEADS // 2


def _scan_body(r_ref, w_ref, k_ref, v_ref, a_ref, b_ref, s0_ref, y_ref, so_ref, st_ref, *, tc, nc, nb):
    c = pl.program_id(1)

    @pl.when(c == 0)
    def _():
        for q in range(nb):
            for p in range(N_PAIR):
                st_ref[q, p] = jnp.concatenate([s0_ref[q, 2 * p], s0_ref[q, 2 * p + 1]], axis=-1)

    shape = (A_HEAD_DIM, LANES)
    lane = lax.broadcasted_iota(jnp.int32, shape, 1)
    sub = lax.broadcasted_iota(jnp.int32, shape, 0)
    lo = lane < A_HEAD_DIM
    diag = (lane & (A_HEAD_DIM - 1)) == sub

    def seg_sum(x):
        s_lo = jnp.sum(jnp.where(lo, x, 0.0), axis=-1, keepdims=True)
        s_hi = jnp.sum(jnp.where(lo, 0.0, x), axis=-1, keepdims=True)
        return jnp.where(lo, s_lo, s_hi)

    grp = min(SUBLANES, tc)

    def token_group(gi, carry):
        base = pl.multiple_of(gi * grp, grp)
        for q in range(nb):
            for p in range(N_PAIR):
                sl = slice(LANES * p, LANES * (p + 1))
                rt, wt, kt, vt, at, bt = (ref[q, pl.ds(base, grp), sl]
                                          for ref in (r_ref, w_ref, k_ref, v_ref, a_ref, b_ref))
                s = st_ref[q, p]
                ys = []
                for j in range(grp):
                    row = lambda z: z[j:j + 1, :]
                    sa = seg_sum(s * row(at))
                    v_col = seg_sum(jnp.where(diag, row(vt), 0.0))
                    s = s * row(wt) + sa * row(bt) + v_col * row(kt)
                    y_col = seg_sum(s * row(rt))
                    ys.append(jnp.sum(jnp.where(diag, y_col, 0.0), axis=0, keepdims=True))
                st_ref[q, p] = s
                y_ref[q, pl.ds(base, grp), sl] = jnp.concatenate(ys, axis=0) if grp > 1 else ys[0]
        return carry

    lax.fori_loop(0, tc // grp, token_group, 0)

    @pl.when(c == nc - 1)
    def _():
        for q in range(nb):
            for p in range(N_PAIR):
                s = st_ref[q, p]
                so_ref[q, 2 * p] = s[:, :A_HEAD_DIM]
                so_ref[q, 2 * p + 1] = s[:, A_HEAD_DIM:]


def _rwkv_scan(seqs, s0, tc, nb):
    b, t, _ = seqs[0].shape
    nc = t // tc
    seq_spec = pl.BlockSpec((nb, tc, C_A), lambda i, c: (i, c, 0))
    st_spec = pl.BlockSpec((nb, A_HEADS, A_HEAD_DIM, A_HEAD_DIM), lambda i, c: (i, 0, 0, 0))
    return pl.pallas_call(
        functools.partial(_scan_body, tc=tc, nc=nc, nb=nb),
        grid=(b // nb, nc),
        in_specs=[seq_spec] * 6 + [st_spec],
        out_specs=[seq_spec, st_spec],
        out_shape=[jax.ShapeDtypeStruct((b, t, C_A), F32), jax.ShapeDtypeStruct(s0.shape, F32)],
        scratch_shapes=[pltpu.VMEM((nb, N_PAIR, A_HEAD_DIM, LANES), F32)],
        compiler_params=pltpu.CompilerParams(dimension_semantics=("arbitrary", "arbitrary")),
        name="rwkv_scan",
    )(*seqs, s0)


SCAN_CHUNK = 64
SCAN_CHUNKS_PER_STEP = 4


def _mm(a, b):
    return _dot(a.astype(BF16), b.astype(BF16))


def _chunk_scan_body(r_ref, lw_ref, k_ref, v_ref, a_ref, b_ref, s0_ref, y_ref, so_ref, st_ref, *, tc, nc):
    cidx = pl.program_id(1)
    c = SCAN_CHUNK
    d = A_HEAD_DIM

    @pl.when(cidx == 0)
    def _():
        for h in range(A_HEADS):
            st_ref[h] = s0_ref[0, h].T

    ri = lax.broadcasted_iota(jnp.int32, (c, c), 0)
    ci = lax.broadcasted_iota(jnp.int32, (c, c), 1)
    lower = ci <= ri
    strict = ci < ri
    eye = lax.broadcasted_iota(jnp.int32, (d, d), 0) == lax.broadcasted_iota(jnp.int32, (d, d), 1)
    n_double = (c - 1).bit_length()
    per = SCAN_CHUNKS_PER_STEP
    slab = per * c
    sr = lax.broadcasted_iota(jnp.int32, (slab, slab), 0)
    sc = lax.broadcasted_iota(jnp.int32, (slab, slab), 1)
    tri = ((sr // c == sc // c) & (sc <= sr)).astype(F32)

    def step_fn(step, carry):
        base = pl.multiple_of(step * slab, slab)
        r, lw, k, v, a, b = (ref[0, pl.ds(base, slab), :] for ref in (r_ref, lw_ref, k_ref, v_ref, a_ref, b_ref))
        cum = _dot(tri, lw, precision=HI)
        tots = [cum[(q + 1) * c - 1:(q + 1) * c, :] for q in range(per)]
        tot = jnp.concatenate([jnp.broadcast_to(t_, (c, C_A)) for t_ in tots], axis=0)
        e_inv = jnp.exp(-cum)
        e_rest = jnp.exp(tot - cum)
        at = a * jnp.exp(cum - lw)
        rt = r * jnp.exp(cum)
        bt = b * e_inv
        kt = k * e_inv
        bh = b * e_rest
        kh = k * e_rest
        g_tot = [jnp.exp(t_) for t_ in tots]
        units = [(q, h) for q in range(per) for h in range(A_HEADS)]
        cut = lambda z, u: z[u[0] * c:(u[0] + 1) * c, u[1] * d:(u[1] + 1) * d]
        g4 = [_dot_nt(jnp.concatenate([cut(at, u), cut(rt, u)], axis=0).astype(BF16),
                      jnp.concatenate([cut(bt, u), cut(kt, u)], axis=0).astype(BF16)) for u in units]
        lp = [jnp.where(strict, g[:c, :c], 0.0).astype(BF16) for g in g4]
        m_l = [jnp.where(strict, g[:c, c:], 0.0) for g in g4]
        p_b = [jnp.where(lower, g[c:, :c], 0.0) for g in g4]
        p_k = [jnp.where(lower, g[c:, c:], 0.0) for g in g4]
        vb = [cut(v, u).astype(BF16) for u in units]
        n_u = range(len(units))
        z = [jnp.concatenate([cut(at, units[i]), _mm(m_l[i], vb[i])], axis=-1) for i in n_u]
        for q in range(n_double):
            z = [z[i] + _mm(lp[i], z[i]) for i in n_u]
            if q < n_double - 1:
                lp = [_mm(lp[i], lp[i]).astype(BF16) for i in n_u]
        zb = [zz.astype(BF16) for zz in z]
        bz = [_mm(cut(bh, units[i]).T, zb[i]) for i in n_u]
        kv = [_mm(cut(kh, units[i]).T, vb[i]) for i in n_u]
        pz = [_mm(p_b[i], zb[i]) for i in n_u]
        pv = [_mm(p_k[i], vb[i]) for i in n_u]
        rows = []
        for q in range(per):
            ys = []
            for h in range(A_HEADS):
                i = q * A_HEADS + h
                st = st_ref[h]
                a_c = jnp.where(eye, g_tot[q][:, h * d:(h + 1) * d], 0.0) + bz[i][:, :d]
                ys.append(_dot(cut(rt, units[i]) + pz[i][:, :d], st, precision=HI) + pz[i][:, d:] + pv[i])
                st_ref[h] = _dot(a_c, st, precision=HI) + bz[i][:, d:] + kv[i]
            rows.append(jnp.concatenate(ys, axis=-1))
        y_ref[0, pl.ds(base, slab), :] = jnp.concatenate(rows, axis=0)
        return carry

    lax.fori_loop(0, tc // slab, step_fn, 0)

    @pl.when(cidx == nc - 1)
    def _():
        for h in range(A_HEADS):
            so_ref[0, h] = st_ref[h].T


def _rwkv_chunk_scan(seqs, s0, tc):
    b, t, _ = seqs[0].shape
    nc = t // tc
    seq_spec = pl.BlockSpec((1, tc, C_A), lambda i, c: (i, c, 0))
    st_spec = pl.BlockSpec((1, A_HEADS, A_HEAD_DIM, A_HEAD_DIM), lambda i, c: (i, 0, 0, 0))
    return pl.pallas_call(
        functools.partial(_chunk_scan_body, tc=tc, nc=nc),
        grid=(b, nc),
        in_specs=[seq_spec] * 6 + [st_spec],
        out_specs=[seq_spec, st_spec],
        out_shape=[jax.ShapeDtypeStruct((b, t, C_A), F32), jax.ShapeDtypeStruct(s0.shape, F32)],
        scratch_shapes=[pltpu.VMEM((A_HEADS, A_HEAD_DIM, A_HEAD_DIM), F32)],
        compiler_params=pltpu.CompilerParams(dimension_semantics=("arbitrary", "arbitrary")),
        name="rwkv_chunk_scan",
    )(*seqs, s0)


def _gelu_tanh(x):
    return 0.5 * x * (1.0 + jnp.tanh(0.7978845608028654 * (x + 0.044715 * (x * x * x))))


def _compress_body(x_ref, pe_ref, w_ref, b1_ref, w2_ref, o_ref, bias_ref, *, nb):
    d = HEAD_DIM
    bias = [_dot(pe_ref[:, c * BLK * d:(c + 1) * BLK * d].astype(BF16), w_ref[c * BLK * d:(c + 1) * BLK * d, :])[0:1]
            + b1_ref[c:c + 1, :] for c in range(2)]
    bias_ref[...] = jnp.concatenate([jnp.broadcast_to(b_, (SUBLANES, D_CMP)) for b_ in bias], axis=-1)
    xt = jnp.swapaxes(x_ref[...].reshape(nb, BLK, C_KV), 0, 1)
    acc = [jnp.zeros((nb, D_CMP), F32) for _ in range(2 * KV_HEADS)]
    for r in range(BLK):
        xr = xt[r].astype(BF16)
        for hc in range(2 * KV_HEADS):
            c = hc % 2
            acc[hc] = acc[hc] + _dot(xr[:, hc * d:(hc + 1) * d], w_ref[(c * BLK + r) * d:(c * BLK + r + 1) * d, :])
    o_ref[...] = jnp.concatenate(
        [_dot(_gelu_tanh(acc[hc] + bias[hc % 2]).astype(BF16), w2_ref[hc % 2]) for hc in range(2 * KV_HEADS)], axis=-1)


def _compress(kv_rows, pe_flat, w1_rows, b1, w2b, nb):
    t = kv_rows.shape[0]
    consts = [pe_flat, w1_rows, b1, w2b]
    return pl.pallas_call(
        functools.partial(_compress_body, nb=nb),
        grid=(t // (nb * BLK),),
        in_specs=[pl.BlockSpec((nb * BLK, C_KV), lambda i: (i, 0))] + [_const_spec(c.shape) for c in consts],
        out_specs=[pl.BlockSpec((nb, C_KV), lambda i: (i, 0)), pl.BlockSpec((SUBLANES, 2 * D_CMP), lambda i: (0, 0))],
        out_shape=[jax.ShapeDtypeStruct((t // BLK, C_KV), F32), jax.ShapeDtypeStruct((SUBLANES, 2 * D_CMP), F32)],
        compiler_params=pltpu.CompilerParams(dimension_semantics=("arbitrary",)),
        name="nsa_compress",
    )(kv_rows, *consts)


PAGE_ROWS = KV_HEADS * 2 * HEAD_DIM


def _compress_pool_body(pt_ref, cache_ref, w_ref, b_ref, w2_ref, o_ref, xbuf, sems, *, pg, nseq):
    n = pl.program_id(0)
    slot = n % 2

    def page_copy(seq, buf, jp):
        return pltpu.make_async_copy(cache_ref.at[pt_ref[seq, jp]], xbuf.at[buf, pl.ds(jp * KV_HEADS, KV_HEADS)],
                                     sems.at[buf])

    def start_all(seq, buf):
        def start(jp, carry):
            page_copy(seq, buf, jp).start()
            return carry
        lax.fori_loop(0, pg, start, 0)

    def wait(jp, carry):
        page_copy(n, slot, jp).wait()
        return carry

    @pl.when(n == 0)
    def _():
        start_all(0, 0)

    @pl.when(n + 1 < nseq)
    def _():
        start_all(n + 1, 1 - slot)

    lax.fori_loop(0, pg, wait, 0)

    m = pg * KV_HEADS
    for c in range(2):
        xt = jnp.swapaxes(xbuf[slot, :, c * HEAD_DIM:(c + 1) * HEAD_DIM, :], 0, 1)
        acc = jnp.zeros((m, 2 * D_CMP), F32)
        for d in range(0, HEAD_DIM, 2):
            r0 = c * HEAD_DIM + d
            x_pair = jnp.concatenate([xt[d], xt[d + 1]], axis=-1).astype(BF16)
            acc = acc + _dot(x_pair, w_ref[r0 * 2 * BLK:(r0 + 2) * 2 * BLK, :])
        h = acc + b_ref[c:c + 1, :]
        o_ref[:, c * 2 * HEAD_DIM:(c + 1) * 2 * HEAD_DIM] = _dot(_gelu_tanh(h).astype(BF16), w2_ref[c])


def _compress_pool(page_table, cache_t, w_tok, bias2, w2_pair):
    nseq, pg = page_table.shape
    m = pg * KV_HEADS
    vmem = 3 * pg * PAGE_ROWS * 2 * BLK * 4 + int(w_tok.size) * 2 + 8 * m * 2 * D_CMP * 4 + (4 << 20)
    const = lambda z: pl.BlockSpec(z.shape, lambda i, pt: (0,) * z.ndim, pipeline_mode=pl.Buffered(1))
    grid_spec = pltpu.PrefetchScalarGridSpec(
        num_scalar_prefetch=1,
        grid=(nseq,),
        in_specs=[pl.BlockSpec(memory_space=pl.ANY), const(w_tok), const(bias2), const(w2_pair)],
        out_specs=pl.BlockSpec((m, C_KV), lambda i, pt: (i, 0)),
        scratch_shapes=[pltpu.VMEM((2, m, 2 * HEAD_DIM, 2 * BLK), F32), pltpu.SemaphoreType.DMA((2,))],
    )
    return pl.pallas_call(
        functools.partial(_compress_pool_body, pg=pg, nseq=nseq),
        grid_spec=grid_spec,
        out_shape=jax.ShapeDtypeStruct((nseq * m, C_KV), F32),
        compiler_params=pltpu.CompilerParams(dimension_semantics=("arbitrary",), vmem_limit_bytes=_vmem_limit(vmem)),
        name="nsa_compress_pool",
    )(page_table, cache_t, w_tok, bias2, w2_pair)


def _topk_mask(s, ids, k, axis):
    sel = jnp.zeros(s.shape, F32)
    for _ in range(k):
        m = jnp.max(s, axis=axis, keepdims=True)
        pick = jnp.min(jnp.where(s == m, ids, 1e9), axis=axis, keepdims=True)
        hit = ids == pick
        sel = jnp.where(hit, 1.0, sel)
        s = jnp.where(hit, -jnp.inf, s)
    return sel


def _head_rows(q_ref, hk, pad_rows=0, pad_lanes=True):
    parts = [q_ref[:, (hk * GROUP + g) * HEAD_DIM:(hk * GROUP + g + 1) * HEAD_DIM] for g in range(GROUP)]
    if pad_rows:
        parts.append(jnp.zeros((pad_rows, HEAD_DIM), parts[0].dtype))
    qh = jnp.concatenate(parts, axis=0)
    return jnp.concatenate([qh, jnp.zeros_like(qh)], axis=-1) if pad_lanes else qh


AUG_HI = HEAD_DIM
AUG_LO = HEAD_DIM + 1
AUG_ONE = HEAD_DIM + 2
AUG_BLK = HEAD_DIM + 3
POS_SPLIT = LANES


def _nsa_select_body(q_ref, kc_ref, kct_ref, oc_ref, ns_ref, flag_ref, *, tq, nbc):
    i = pl.program_id(0)
    t0 = i * tq
    qrow1 = t0 + lax.broadcasted_iota(jnp.int32, (1, tq), 1)
    qrow = jnp.concatenate([qrow1] * GROUP, axis=1)
    cur_row = qrow1 // BLK
    blkc = lax.broadcasted_iota(jnp.int32, (nbc, 1), 0)
    blkcf = blkc.astype(F32)
    pieces, masks, flags = [], [], []
    for hk in range(KV_HEADS):
        hl = slice(hk * KV_LANES, (hk + 1) * KV_LANES)
        qh = _head_rows(q_ref, hk)
        slope_row = jnp.concatenate([jnp.full((1, tq), 2.0 ** (-(hk * GROUP + g + 1)), F32) for g in range(GROUP)],
                                    axis=1)
        kcb = kc_ref[:, hl].astype(BF16)
        cposc = blkc * BLK + (BLK - 1)
        s_t = _dot_nt(kcb, qh) - slope_row * (qrow - cposc).astype(F32)
        s_t = jnp.where(cposc <= qrow, s_t, NEG)
        m_c = jnp.max(s_t, axis=0, keepdims=True)
        m_c = jnp.where(m_c > M_FLOOR, m_c, 0.0)
        e_c = jnp.exp(s_t - m_c)
        p_t = e_c / jnp.maximum(jnp.sum(e_c, axis=0, keepdims=True), 1e-30)
        o_c = _dot(kct_ref[hl, :].astype(BF16), p_t.astype(BF16))[HEAD_DIM:, :].T
        score = p_t[:, 0:tq]
        for g in range(1, GROUP):
            score = score + p_t[:, g * tq:(g + 1) * tq]
        forced = (blkc == 0) | (blkc == cur_row) | (blkc == cur_row - 1)
        sc = jnp.where(forced, FORCE_SCORE, score)
        sc = jnp.where(blkc > cur_row, -1.0, sc)
        sel_t = _topk_mask(sc, blkcf, min(N_SEL, nbc), axis=0)
        sel_q = jnp.where(blkc <= cur_row, sel_t, 0.0).T
        masks.append(1.0 - sel_q)
        flags.append(jnp.max(sel_q, axis=0, keepdims=True))
        pieces += [o_c[g * tq:(g + 1) * tq] for g in range(GROUP)]
    oc_ref[...] = jnp.concatenate(pieces, axis=-1)
    ns_ref[...] = jnp.concatenate(masks, axis=-1).astype(BF16)
    flag_ref[0] = jnp.concatenate(flags, axis=0)


def _nsa_select(q, kvc, tq):
    t = q.shape[0]
    nbc = kvc.shape[0]
    kct = kvc.T
    return pl.pallas_call(
        functools.partial(_nsa_select_body, tq=tq, nbc=nbc),
        grid=(t // tq,),
        in_specs=[pl.BlockSpec((tq, C_B), lambda i: (i, 0)), _const_spec(kvc.shape), _const_spec(kct.shape)],
        out_specs=[pl.BlockSpec((tq, C_B), lambda i: (i, 0)), pl.BlockSpec((tq, KV_HEADS * nbc), lambda i: (i, 0)),
                   pl.BlockSpec((1, KV_HEADS, nbc), lambda i: (i, 0, 0))],
        out_shape=[jax.ShapeDtypeStruct((t, C_B), F32), jax.ShapeDtypeStruct((t, KV_HEADS * nbc), BF16),
                   jax.ShapeDtypeStruct((t // tq, KV_HEADS, nbc), F32)],
        compiler_params=pltpu.CompilerParams(dimension_semantics=("arbitrary",)),
        name="nsa_select",
    )(q, kvc, kct)


def _nsa_prompt_body(cnt_ref, lst_ref, q_ref, gt_ref, oc_ref, ns_ref, ka_ref, vat_ref, kw_ref, em_ref, o_ref,
                     *, tq, tk, nbc, max_tiles):
    i = pl.program_id(0)
    t0 = i * tq
    rows = GROUP * tq
    qpos1 = t0 + lax.broadcasted_iota(jnp.int32, (tq, 1), 0)
    qpos = jnp.concatenate([qpos1] * GROUP, axis=0)
    qposf = qpos.astype(F32)
    qrow = jnp.concatenate([t0 + lax.broadcasted_iota(jnp.int32, (1, tq), 1)] * GROUP, axis=1)
    blk_row = lax.broadcasted_iota(jnp.int32, (1, nbc), 1)
    lane = lax.broadcasted_iota(jnp.int32, (1, KV_LANES), 1)
    gt = jax.nn.sigmoid(gt_ref[...])
    bpt = tk // BLK
    t0f = t0.astype(F32)
    o_ws, not_sels, q_augs = [], [], []
    for hk in range(KV_HEADS):
        hl = slice(hk * KV_LANES, (hk + 1) * KV_LANES)
        qh = _head_rows(q_ref, hk)
        slope = _slope_col(hk, tq)
        not_sels.append(ns_ref[:, hk * nbc:(hk + 1) * nbc])

        wl = WINDOW + tq
        ws = pl.multiple_of(jnp.maximum(t0 - WINDOW, 0), tq)
        kvw = kw_ref[pl.ds(ws, wl), hl]
        kposw = ws + lax.broadcasted_iota(jnp.int32, (1, wl), 1)
        s = _dot_nt(qh, kvw) - slope * (qposf - kposw.astype(F32))
        p_w = _masked_softmax(s, (kposw <= qpos) & (qpos - kposw < WINDOW))
        o_ws.append(_dot(p_w.astype(BF16), kvw)[:, HEAD_DIM:])

        q_augs.append(qh.astype(F32) + jnp.where(lane == AUG_HI, slope * POS_SPLIT, 0.0)
                      + jnp.where(lane == AUG_LO, slope, 0.0) + jnp.where(lane == AUG_ONE, -slope * t0f, 0.0))

    def sel_step(tiles, active, carry, causal):
        out = []
        for hk in range(KV_HEADS):
            m, acc = carry[hk]
            j = tiles[hk]
            k0 = pl.multiple_of(j * tk, tk)
            ns = not_sels[hk] if active[hk] is None else jnp.where(active[hk], not_sels[hk], jnp.ones_like(not_sels[hk]))
            mk = _dot(jnp.where(blk_row // bpt == j, ns, jnp.zeros_like(ns)), em_ref[...])
            q_j = (q_augs[hk] + jnp.concatenate([mk] * GROUP, axis=0)).astype(BF16)
            s = _dot_nt(ka_ref[hk, pl.ds(k0, tk), :], q_j)
            if causal:
                kpos = k0 + lax.broadcasted_iota(jnp.int32, (tk, 1), 0)
                s = jnp.where(kpos <= qrow, s, NEG)
            m_new = jnp.maximum(m, jnp.max(s, axis=0, keepdims=True))
            p = jnp.exp(s - m_new).astype(BF16)
            out.append((m_new, jnp.exp(m - m_new) * acc + _dot(vat_ref[hk, j], p)))
        return tuple(out)

    cnts = [cnt_ref[i * KV_HEADS + hk] for hk in range(KV_HEADS)]

    def skip_step(n, carry):
        active = [n < cnts[hk] - 1 for hk in range(KV_HEADS)]
        tiles = [jnp.where(active[hk], lst_ref[(i * KV_HEADS + hk) * max_tiles + n], 0) for hk in range(KV_HEADS)]
        return sel_step(tiles, active, carry, False)

    init = tuple((jnp.full((1, rows), M_FLOOR, F32), jnp.zeros((KV_LANES, rows), F32)) for _ in range(KV_HEADS))
    n_steps = functools.reduce(jnp.maximum, cnts) - 1
    carry = lax.fori_loop(0, n_steps, skip_step, init)
    last = (t0 + tq + tk - 1) // tk - 1
    carry = sel_step([last] * KV_HEADS, [None] * KV_HEADS, carry, True)

    pieces = []
    for hk in range(KV_HEADS):
        acc = carry[hk][1]
        o_s = (acc[:HEAD_DIM] / jnp.maximum(acc[HEAD_DIM:HEAD_DIM + 1], 1e-30)).T
        for g in range(GROUP):
            c0 = (hk * GROUP + g) * 3
            cs = slice((hk * GROUP + g) * HEAD_DIM, (hk * GROUP + g + 1) * HEAD_DIM)
            rs = slice(g * tq, (g + 1) * tq)
            pieces.append(gt[:, c0:c0 + 1] * oc_ref[:, cs] + gt[:, c0 + 1:c0 + 2] * o_s[rs]
                          + gt[:, c0 + 2:c0 + 3] * o_ws[hk][rs])
    o_ref[...] = jnp.concatenate(pieces, axis=-1)


def _nsa_prompt(q, gates, kvc, ks, kw, tq, tk):
    t = q.shape[0]
    nbc = kvc.shape[0]
    bpt = tk // BLK
    max_tiles = nbc // bpt
    assert t <= POS_SPLIT * 256 and AUG_BLK + bpt <= KV_LANES and t % tk == 0
    oc, not_sel, blk_any = _nsa_select(q, kvc, tq)
    tile_any = jnp.max(blk_any.reshape(t // tq, KV_HEADS, max_tiles, bpt), axis=-1) > 0.0
    rank = jnp.cumsum(tile_any.astype(jnp.int32), axis=-1) - 1
    slot = jnp.arange(max_tiles, dtype=jnp.int32)
    hit = tile_any[..., :, None] & (rank[..., :, None] == slot)
    lst = jnp.sum(jnp.where(hit, slot[:, None], 0), axis=-2).astype(jnp.int32).reshape(-1)
    cnt = jnp.sum(tile_any, axis=-1).astype(jnp.int32).reshape(-1)
    em = jnp.where(jnp.arange(KV_LANES)[None, :] == AUG_BLK + jnp.arange(nbc)[:, None] % bpt, NEG, 0.0).astype(BF16)
    pos = jnp.arange(t, dtype=jnp.int32)[:, None]
    aug_lane = jnp.arange(HEAD_DIM, dtype=jnp.int32)[None, :] + HEAD_DIM
    k_aug = (jnp.where(aug_lane == AUG_HI, pos // POS_SPLIT, 0) + jnp.where(aug_lane == AUG_LO, pos % POS_SPLIT, 0)
             + jnp.where(aug_lane == AUG_ONE, 1, 0)
             + jnp.where(aug_lane == AUG_BLK + (pos // BLK) % bpt, 1, 0)).astype(BF16)
    v_aug = jnp.broadcast_to(jnp.where(aug_lane == HEAD_DIM, 1, 0).astype(BF16), (t, HEAD_DIM))
    ka = jnp.stack([jnp.concatenate([ks[:, h * KV_LANES:h * KV_LANES + HEAD_DIM], k_aug], axis=1)
                    for h in range(KV_HEADS)])
    va = jnp.stack([jnp.concatenate([ks[:, h * KV_LANES + HEAD_DIM:(h + 1) * KV_LANES], v_aug], axis=1)
                    for h in range(KV_HEADS)])
    va = jnp.transpose(va.reshape(KV_HEADS, t // tk, tk, KV_LANES), (0, 1, 3, 2))
    vmem = 3 * t * C_KV * 2 + 24 * GROUP * tq * max(tk, WINDOW + tq) * 4 + (8 << 20)
    tile = lambda w: pl.BlockSpec((tq, w), lambda i, c, l: (i, 0))
    const = lambda z: pl.BlockSpec(z.shape, lambda i, c, l: (0,) * z.ndim, pipeline_mode=pl.Buffered(1))
    grid_spec = pltpu.PrefetchScalarGridSpec(
        num_scalar_prefetch=2,
        grid=(t // tq,),
        in_specs=[tile(C_B), tile(GATE_PAD), tile(C_B), tile(KV_HEADS * nbc), const(ka), const(va), const(kw), const(em)],
        out_specs=tile(C_B),
    )
    return pl.pallas_call(
        functools.partial(_nsa_prompt_body, tq=tq, tk=tk, nbc=nbc, max_tiles=max_tiles),
        grid_spec=grid_spec,
        out_shape=jax.ShapeDtypeStruct((t, C_B), F32),
        compiler_params=pltpu.CompilerParams(dimension_semantics=("arbitrary",), vmem_limit_bytes=_vmem_limit(vmem)),
        name="nsa_prompt",
    )(cnt, lst, q, gates, oc, not_sel, ka, va, kw, em)


PAGE_BLOCKS = 2
Q_ROWS = SUBLANES


def _samp_cmp_body(q_ref, kc_ref, oc_ref, sc_ref, *, n_pages, past):
    kbuf = kc_ref.at[0]
    nb_past = n_pages * PAGE_BLOCKS
    n_all = nb_past + 1
    qpos = past
    cur = qpos // BLK
    width = PAGE_BLOCKS * n_pages
    lane = lax.broadcasted_iota(jnp.int32, (1, width), 1)
    bid = jnp.where(lane < n_pages, PAGE_BLOCKS * lane, PAGE_BLOCKS * (lane - n_pages) + 1)
    cpos = bid * BLK + (BLK - 1)
    bid2 = jnp.concatenate([bid, nb_past + lane], axis=-1)
    scores = []
    for hk in range(KV_HEADS):
        q8 = _head_rows(q_ref.at[0], hk, pad_rows=Q_ROWS - GROUP, pad_lanes=False)
        z8 = jnp.zeros_like(q8)
        slope = _slope_col(hk, 1, pad_rows=Q_ROWS - GROUP)
        kmat = kbuf[:, hk * C_KV:hk * C_KV + KV_LANES].astype(BF16)
        vmat = kbuf[:, hk * C_KV + KV_LANES:(hk + 1) * C_KV].astype(BF16)
        s = jnp.concatenate([_dot_nt(jnp.concatenate([q8, z8], axis=-1), kmat),
                             _dot_nt(jnp.concatenate([z8, q8], axis=-1), kmat)], axis=-1)
        s = s - slope * (qpos - cpos).astype(F32)
        p_c = _masked_softmax(s, cpos <= qpos)
        o_even = _dot(p_c[:, :n_pages].astype(BF16), vmat)
        o_odd = _dot(p_c[:, n_pages:].astype(BF16), vmat)
        oc_ref[0, hk] = o_even[:, :HEAD_DIM] + o_odd[:, HEAD_DIM:]
        score = jnp.sum(p_c[0:GROUP], axis=0, keepdims=True)
        sc = jnp.concatenate([score, jnp.zeros((1, width), F32)], axis=-1)
        forced = (bid2 == 0) | (bid2 == cur) | (bid2 == cur - 1)
        sc = jnp.where(forced, FORCE_SCORE, sc)
        sc = jnp.where(bid2 > cur, -1.0, sc)
        scores.append(jnp.where(bid2 < n_all, sc, -jnp.inf))
    sc_ref[0] = jnp.concatenate(scores, axis=0)


def _samp_cmp(q3, kc3, past):
    db, n_pages, _ = kc3.shape
    wide = 2 * PAGE_BLOCKS * n_pages
    return pl.pallas_call(
        functools.partial(_samp_cmp_body, n_pages=n_pages, past=past),
        grid=(db,),
        in_specs=[pl.BlockSpec((1, 1, C_B), lambda n: (n, 0, 0)),
                  pl.BlockSpec((1, n_pages, KV_HEADS * C_KV), lambda n: (n, 0, 0))],
        out_specs=[pl.BlockSpec((1, KV_HEADS, Q_ROWS, HEAD_DIM), lambda n: (n, 0, 0, 0)),
                   pl.BlockSpec((1, KV_HEADS, wide), lambda n: (n, 0, 0))],
        out_shape=[jax.ShapeDtypeStruct((db, KV_HEADS, Q_ROWS, HEAD_DIM), F32),
                   jax.ShapeDtypeStruct((db, KV_HEADS, wide), F32)],
        compiler_params=pltpu.CompilerParams(dimension_semantics=("arbitrary",)),
        name="nsa_sample_cmp",
    )(q3, kc3)


def _samp_topk_body(sc_ref, idx_ref, *, n_pages, nb_past):
    s_t = sc_ref[...].T
    npos, nrow = s_t.shape
    width = PAGE_BLOCKS * n_pages
    pos = lax.broadcasted_iota(jnp.int32, (npos, 1), 0)
    ids = jnp.where(pos < n_pages, PAGE_BLOCKS * pos,
                    jnp.where(pos < width, PAGE_BLOCKS * (pos - n_pages) + 1, nb_past + pos - width)).astype(F32)
    picks = []
    for _ in range(N_SEL):
        m = jnp.max(s_t, axis=0, keepdims=True)
        pick = jnp.min(jnp.where(s_t == m, ids, 1e9), axis=0, keepdims=True)
        picks.append(pick)
        s_t = jnp.where(ids == pick, -jnp.inf, s_t)
    res = jnp.concatenate(picks + [jnp.zeros((LANES - N_SEL, nrow), F32)], axis=0)
    idx_ref[...] = res.T.astype(jnp.int32)


def _samp_topk(scores2, n_pages):
    rows = scores2.shape[0]
    return pl.pallas_call(
        functools.partial(_samp_topk_body, n_pages=n_pages, nb_past=n_pages * PAGE_BLOCKS),
        out_shape=jax.ShapeDtypeStruct((rows, LANES), jnp.int32),
        name="nsa_sample_topk",
    )(scores2)


def _samp_sel_body(pt_ref, idx_ref, q_ref, gt_ref, oc_ref, ksc_ref, kwr_ref, kwc_ref, win_ref, cache_ref,
                   o_ref, wino_ref, sbuf, sems, *, nb_past, past, wb, db):
    n = pl.program_id(0)
    qpos = past

    page_tok = PAGE_BLOCKS * BLK
    tok = lax.broadcasted_iota(jnp.int32, (HEAD_DIM, page_tok), 1)

    buf = n % 2

    def page_copy(sample, b, hk, s, ib):
        page = pt_ref[sample, ib // PAGE_BLOCKS]
        return pltpu.make_async_copy(cache_ref.at[page, hk], sbuf.at[b, hk, :, :, pl.ds(s * page_tok, page_tok)],
                                     sems.at[b, hk * N_SEL + s])

    def start_pages(sample, b):
        for hk in range(KV_HEADS):
            for s in range(N_SEL):
                ib = idx_ref[sample, hk * N_SEL + s]

                @pl.when(ib < nb_past)
                def _():
                    page_copy(sample, b, hk, s, ib).start()

    @pl.when(n == 0)
    def _():
        start_pages(0, 0)

    @pl.when(n + 1 < db)
    def _():
        start_pages(n + 1, 1 - buf)

    for hk in range(KV_HEADS):
        for s in range(N_SEL):
            ib = idx_ref[n, hk * N_SEL + s]

            @pl.when(ib >= nb_past)
            def _():
                for c in range(2):
                    sbuf[buf, hk, c, :, s * page_tok:(s + 1) * page_tok] = jnp.where(tok == 0, ksc_ref[0, 2 * hk + c],
                                                                                    0.0)

    wtok = lax.broadcasted_iota(jnp.int32, (HEAD_DIM, wb), 1)
    for hk in range(KV_HEADS):
        for c in range(2):
            wino_ref[0, hk, c] = jnp.where(wtok == wb - 1, kwc_ref[0, 2 * hk + c],
                                           pltpu.roll(win_ref[0, hk, c], wb - 1, 1))

    for hk in range(KV_HEADS):
        for s in range(N_SEL):
            ib = idx_ref[n, hk * N_SEL + s]

            @pl.when(ib < nb_past)
            def _():
                page_copy(n, buf, hk, s, ib).wait()

    gt = jax.nn.sigmoid(gt_ref[0])
    nk = N_SEL * page_tok
    lane = lax.broadcasted_iota(jnp.int32, (1, nk), 1)
    kposw = past - wb + lax.broadcasted_iota(jnp.int32, (1, wb), 1)
    wmask = (kposw <= qpos) & (qpos - kposw < WINDOW) & (kposw >= 0)
    pieces = []
    for hk in range(KV_HEADS):
        q8 = _head_rows(q_ref.at[0], hk, pad_rows=Q_ROWS - GROUP, pad_lanes=False)
        slope = _slope_col(hk, 1, pad_rows=Q_ROWS - GROUP)
        ibv = jnp.zeros((1, nk), jnp.int32)
        for s in range(N_SEL):
            ibv = jnp.where(lane // page_tok == s, idx_ref[n, hk * N_SEL + s], ibv)
        spos = (ibv - ibv % PAGE_BLOCKS) * BLK + lane % page_tok
        s_s = _dot(q8, sbuf[buf, hk, 0].astype(BF16)) - slope * (qpos - spos).astype(F32)
        p_s = _masked_softmax(s_s, (spos // BLK == ibv) & (spos <= qpos))
        o_s = _dot_nt(p_s.astype(BF16), sbuf[buf, hk, 1].astype(BF16))
        kwn = kwr_ref[0]
        k_new = kwn[:, hk * KV_LANES:hk * KV_LANES + HEAD_DIM]
        v_new = kwn[:, hk * KV_LANES + HEAD_DIM:(hk + 1) * KV_LANES]
        s_w = _dot(q8, win_ref[0, hk, 0].astype(BF16)) - slope * (qpos - kposw).astype(F32)
        s_w = jnp.where(wmask, s_w, NEG)
        s_n = jnp.sum(q8.astype(F32) * k_new, axis=-1, keepdims=True)
        m_w = jnp.maximum(jnp.max(s_w, axis=-1, keepdims=True), s_n)
        e_w = jnp.exp(s_w - m_w)
        e_n = jnp.exp(s_n - m_w)
        den = jnp.sum(e_w, axis=-1, keepdims=True) + e_n
        o_w = (_dot_nt(e_w.astype(BF16), win_ref[0, hk, 1].astype(BF16)) + e_n * v_new) / den
        o_c = oc_ref[0, hk]
        for g in range(GROUP):
            c0 = (hk * GROUP + g) * 3
            pieces.append(gt[:, c0:c0 + 1] * o_c[g:g + 1] + gt[:, c0 + 1:c0 + 2] * o_s[g:g + 1]
                          + gt[:, c0 + 2:c0 + 3] * o_w[g:g + 1])
    o_ref[0] = jnp.concatenate(pieces, axis=-1)


def _samp_sel(page_table, idx, q3, gates3, oc, ks_cols, kw_row, kw_cols, win_t, cache_t, past):
    db, n_pages = page_table.shape
    wb = win_t.shape[-1]
    nb_past = n_pages * PAGE_BLOCKS
    row3 = lambda w: pl.BlockSpec((1, 1, w), lambda n, pt, ix: (n, 0, 0))
    col4 = pl.BlockSpec((1, 2 * KV_HEADS, HEAD_DIM, 1), lambda n, pt, ix: (n, 0, 0, 0))
    win_spec = pl.BlockSpec((1, KV_HEADS, 2, HEAD_DIM, wb), lambda n, pt, ix: (n, 0, 0, 0, 0))
    grid_spec = pltpu.PrefetchScalarGridSpec(
        num_scalar_prefetch=2,
        grid=(db,),
        in_specs=[row3(C_B), row3(GATE_PAD),
                  pl.BlockSpec((1, KV_HEADS, Q_ROWS, HEAD_DIM), lambda n, pt, ix: (n, 0, 0, 0)),
                  col4, row3(C_KV), col4, win_spec, pl.BlockSpec(memory_space=pl.ANY)],
        out_specs=[row3(C_B), win_spec],
        scratch_shapes=[pltpu.VMEM((2, KV_HEADS, 2, HEAD_DIM, N_SEL * PAGE_BLOCKS * BLK), F32),
                        pltpu.SemaphoreType.DMA((2, KV_HEADS * N_SEL))],
    )
    return pl.pallas_call(
        functools.partial(_samp_sel_body, nb_past=nb_past, past=past, wb=wb, db=db),
        grid_spec=grid_spec,
        out_shape=[jax.ShapeDtypeStruct((db, 1, C_B), F32), jax.ShapeDtypeStruct(win_t.shape, F32)],
        compiler_params=pltpu.CompilerParams(dimension_semantics=("arbitrary",)),
        name="nsa_sample_sel",
    )(page_table, idx, q3, gates3, oc, ks_cols, kw_row, kw_cols, win_t, cache_t)


FF_CHUNK = 1408


def _merge_body(x_ref, y_ref, bonus_ref, g_ref, yb_ref, mg_ref, p_ref, cbuf_ref,
                lng_ref, lnb_ref, bd_ref, woa_ref, wob_ref, wout_ref, n2_ref, wup_ref, cw_ref, cb_ref, wdn_ref,
                n3_ref, wpe_ref, wpg_ref, fg_ref, o_ref, cnew_ref, carry_ref, *, seq_mode, final, d_ff, tm):
    i = pl.program_id(0)
    d_model = x_ref.shape[1]
    y = y_ref[...]
    inv = 1.0 / A_HEAD_DIM
    mean = _dot_ones(y, bd_ref[...]) * inv
    d = y - mean
    var = _dot_ones(d * d, bd_ref[...]) * inv
    ya = (d * lax.rsqrt(var + LNX_EPS) * lng_ref[...] + lnb_ref[...] + bonus_ref[...]) * g_ref[...]
    mg = mg_ref[...]
    m = (jax.nn.sigmoid(mg[:, :d_model]) * _dot(ya.astype(BF16), woa_ref[...])
         + jax.nn.sigmoid(mg[:, d_model:]) * _dot(yb_ref[...].astype(BF16), wob_ref[...]))
    h = x_ref[...] + _dot(m.astype(BF16), wout_ref[...])
    xn = _rms(h, n2_ref[...]).astype(BF16)

    if seq_mode:
        @pl.when(i == 0)
        def _():
            carry_ref[...] = jnp.zeros_like(carry_ref)
        rid = lax.broadcasted_iota(jnp.int32, (tm, FF_CHUNK), 0)

    acc = jnp.zeros((tm, d_model), F32)
    for c in range(d_ff // FF_CHUNK):
        parts = []
        for half in range(2):
            cs = slice(half * d_ff + c * FF_CHUNK, half * d_ff + (c + 1) * FF_CHUNK)
            up = _dot(xn, wup_ref[:, cs])
            if seq_mode:
                t1 = carry_ref[SUBLANES - 1:SUBLANES, cs]
                t2 = carry_ref[SUBLANES - 2:SUBLANES - 1, cs]
                up1 = jnp.where(rid == 0, t1, pltpu.roll(up, 1, 0))
                up2 = jnp.where(rid == 0, t2, jnp.where(rid == 1, t1, pltpu.roll(up, 2, 0)))
                carry_ref[:, cs] = up[tm - SUBLANES:, :]
            else:
                up2 = cbuf_ref[:, cs]
                up1 = cbuf_ref[:, 2 * d_ff + cs.start:2 * d_ff + cs.stop]
                cnew_ref[:, cs] = up1
                cnew_ref[:, 2 * d_ff + cs.start:2 * d_ff + cs.stop] = up
            parts.append(cb_ref[:, cs] + cw_ref[0:1, cs] * up2 + cw_ref[1:2, cs] * up1 + cw_ref[2:3, cs] * up)
        a, gate = parts
        act = (a * jax.nn.sigmoid(a) * gate).astype(BF16)
        acc = acc + _dot(act, wdn_ref[c * FF_CHUNK:(c + 1) * FF_CHUNK, :])
    if seq_mode:
        cnew_ref[...] = carry_ref[...]
    h = h + acc
    pe = _dot(p_ref[...].astype(BF16), wpe_ref[...])
    h = h + pe * jax.nn.sigmoid(_dot(_rms(h, n3_ref[...]).astype(BF16), wpg_ref[...]))
    o_ref[...] = _rms(h, fg_ref[...]) if final else h


def _merge(x, y, bonus, g, yb, mg, p, cbuf, consts, tm, seq_mode, final):
    rows, d_model = x.shape
    d_ff = consts[10].shape[0]
    f2 = 2 * d_ff
    rowspec = lambda w: pl.BlockSpec((tm, w), lambda i: (i, 0))
    if seq_mode:
        cbuf_spec = _const_spec(cbuf.shape)
        cnew_shape, cnew_spec = (SUBLANES, f2), pl.BlockSpec((SUBLANES, f2), lambda i: (0, 0))
    else:
        cbuf_spec = rowspec(2 * f2)
        cnew_shape, cnew_spec = (rows, 2 * f2), rowspec(2 * f2)
    wbytes = sum(int(c.size) * c.dtype.itemsize for c in consts)
    act = tm * (d_model * 3 + C_A * 4 + 256) * 4 + (0 if seq_mode else 2 * tm * 2 * f2 * 4)
    vmem = wbytes + 2 * act + 8 * tm * d_model * 4 + (8 << 20)
    return pl.pallas_call(
        functools.partial(_merge_body, seq_mode=seq_mode, final=final, d_ff=d_ff, tm=tm),
        grid=(rows // tm,),
        in_specs=[rowspec(d_model), rowspec(C_A), rowspec(C_A), rowspec(C_A), rowspec(C_B), rowspec(2 * d_model),
                  rowspec(p.shape[1]), cbuf_spec] + [_const_spec(c.shape) for c in consts],
        out_specs=[rowspec(d_model), cnew_spec],
        out_shape=[jax.ShapeDtypeStruct((rows, d_model), F32), jax.ShapeDtypeStruct(cnew_shape, F32)],
        scratch_shapes=[pltpu.VMEM((SUBLANES, f2), F32)],
        compiler_params=pltpu.CompilerParams(dimension_semantics=("arbitrary",), vmem_limit_bytes=_vmem_limit(vmem)),
        name="merge_ffn",
    )(x, y, bonus, g, yb, mg, p, cbuf, *consts)


def _pick_tile(n, target):
    t = min(n, target)
    while n % t:
        t //= 2
    return t


def _head_block_diag():
    h = jnp.arange(C_A) // A_HEAD_DIM
    return (h[:, None] == h[None, :]).astype(BF16)


def kernel(x_prompt, x_sample, p_prompt, p_sample, cache_cmp_kv, cache_sel_kv, page_table, state_win_kv, state_wkv, state_shift, state_ffn_conv, norm1_g, w_in, shift_mu, rwkv_w0, rwkv_w2, rwkv_a0, rwkv_a2, rwkv_g2, rwkv_k_k, rwkv_k_a, rwkv_r_k, lnx_g, lnx_b, cmp_pe, cmp_w1, cmp_b1, cmp_w2, w_oa, w_ob, w_out, norm2_g, w_up, conv_w, conv_b, w_down, norm3_g, w_pe, w_pg, final_g):
    depth = w_in.shape[0]
    b, t, d_model = x_prompt.shape
    db, dt, _ = x_sample.shape
    n_pool, page = cache_cmp_kv.shape[1], cache_cmp_kv.shape[2]
    n_pages = page_table.shape[1]
    past = n_pages * page
    d_ff = w_down.shape[1]
    f2 = 2 * d_ff
    wb = state_win_kv.shape[2]
    assert b == 1 and dt == 1 and page == PAGE_BLOCKS * BLK
    assert t % 512 == 0 and t >= WINDOW + 256 and d_ff % FF_CHUNK == 0 and wb == WINDOW and past >= WINDOW
    assert N_SEL <= LANES and n_pages * PAGE_BLOCKS + 1 >= N_SEL

    bd = _head_block_diag()
    hp = x_prompt.reshape(t, d_model)
    hs = x_sample.reshape(db, d_model)
    outs = [[] for _ in range(12)]
    for i in range(depth):
        o_g = C_SHIFT + C_B + 3 * C_KV
        w_perm = jnp.concatenate(
            [w_in[i][:, :o_g], jnp.pad(w_in[i][:, o_g:o_g + 3 * Q_HEADS], ((0, 0), (0, GATE_PAD - 3 * Q_HEADS))),
             w_in[i][:, o_g + 3 * Q_HEADS:]], axis=1).astype(BF16)
        g1 = norm1_g[i].reshape(1, -1)
        rw = (shift_mu[i], rwkv_w0[i], rwkv_w2[i], rwkv_a0[i], rwkv_a2[i], rwkv_g2[i], rwkv_k_k[i], rwkv_k_a[i],
              rwkv_r_k[i].reshape(-1))
        pe_flat = jnp.pad(cmp_pe[i].reshape(1, -1), ((0, SUBLANES - 1), (0, 0)))
        w1_rows = cmp_w1[i].reshape(2 * BLK * HEAD_DIM, D_CMP).astype(BF16)
        eye_b = jnp.eye(PAGE_BLOCKS, dtype=F32)
        w_tok = jnp.einsum('crdf,eg->cdergf', cmp_w1[i], eye_b).reshape(2 * HEAD_DIM * PAGE_BLOCKS * BLK,
                                                                        PAGE_BLOCKS * D_CMP).astype(BF16)
        w2_pair = jnp.einsum('cfd,eg->cefgd', cmp_w2[i], eye_b).reshape(2, PAGE_BLOCKS * D_CMP,
                                                                       PAGE_BLOCKS * HEAD_DIM).astype(BF16)
        row = lambda z: z.reshape(1, -1)
        mconsts = [row(lnx_g[i]), row(lnx_b[i]), bd, w_oa[i].astype(BF16), w_ob[i].astype(BF16),
                   w_out[i].astype(BF16), row(norm2_g[i]), w_up[i].astype(BF16), conv_w[i], row(conv_b[i]),
                   w_down[i].astype(BF16), row(norm3_g[i]), w_pe[i].astype(BF16), w_pg[i].astype(BF16), row(final_g)]
        final = i == depth - 1

        pj = _proj(hp, g1, w_perm, _pick_tile(t, 256), False)
        pa, q, kvc, kvs16, kvw16, gates, mg = (pj[k_] for k_ in ("pa", "q", "kvc", "kvs16", "kvw16", "gates", "mg"))
        seqs = _rwkv_prep(pa, None, rw, bd, _pick_tile(t, 512), True)
        r_, w_, lw_, k_, v_, a_, b_, g_, bonus = seqs
        y, s_new = _rwkv_chunk_scan([z.reshape(1, t, C_A) for z in (r_, lw_, k_, v_, a_, b_)],
                                    jnp.zeros((1, A_HEADS, A_HEAD_DIM, A_HEAD_DIM), F32), _pick_tile(t, 512))
        kvc_blocks, cmp_bias = _compress(kvc, pe_flat, w1_rows, cmp_b1[i], cmp_w2[i].astype(BF16),
                                         _pick_tile(t // BLK, 64))
        bias_pair = jnp.tile(cmp_bias[0].reshape(2, D_CMP), (1, PAGE_BLOCKS))
        yb = _nsa_prompt(q, gates, kvc_blocks, kvs16, kvw16, 256, 512)
        hp, conv_new = _merge(hp, y.reshape(t, C_A), bonus, g_, yb, mg, p_prompt[i].reshape(t, -1),
                              jnp.zeros((SUBLANES, LANES), F32), mconsts, _pick_tile(t, 256), True, final)
        kv6 = lambda z_t, n_: jnp.transpose(z_t.reshape(KV_HEADS, 2, HEAD_DIM, z_t.shape[1]),
                                            (3, 0, 1, 2)).reshape(n_, -1, KV_HEADS, 2, HEAD_DIM)
        outs[0].append(kv6(pj["kvc_t"], 1))
        outs[2].append(kv6(pj["kvs_t"], 1))
        outs[4].append(kv6(pj["kvw_t"][:, t - min(WINDOW, t):], 1))
        outs[6].append(s_new)
        outs[8].append(pa[t - 1:t])
        outs[10].append(conv_new[SUBLANES - (CONV_W - 1):].reshape(1, CONV_W - 1, f2))

        pj = _proj(hs, g1, w_perm, _pick_tile(db, 128), True)
        pa, q, kvs, kvw, gates, mg = (pj[k_] for k_ in ("pa", "q", "kvs", "kvw", "gates", "mg"))
        seqs = _rwkv_prep(pa, state_shift[i], rw, bd, _pick_tile(db, 128), False)
        r_, w_, lw_, k_, v_, a_, b_, g_, bonus = seqs
        y, s_new = _rwkv_scan([z.reshape(db, 1, C_A) for z in (r_, w_, k_, v_, a_, b_)], state_wkv[i], 1,
                              _pick_tile(db, SUBLANES))
        to_tok_minor = lambda z: jnp.transpose(z, (0, 2, 3, 4, 1))
        kc_pages = _compress_pool(page_table, to_tok_minor(cache_cmp_kv[i]).reshape(n_pool, KV_HEADS, 2 * HEAD_DIM, page),
                                  w_tok, bias_pair, w2_pair)
        q3 = q.reshape(db, 1, C_B)
        oc, sel_scores = _samp_cmp(q3, kc_pages.reshape(db, n_pages, KV_HEADS * C_KV), past)
        idx = _samp_topk(sel_scores.reshape(db * KV_HEADS, -1), n_pages)
        idx2 = idx[:, :N_SEL].reshape(db, KV_HEADS * N_SEL)
        cols = lambda z: z.reshape(db, 2 * KV_HEADS, HEAD_DIM, 1)
        yb, win_new_t = _samp_sel(page_table, idx2, q3, gates.reshape(db, 1, GATE_PAD), oc, cols(kvs),
                                  kvw.reshape(db, 1, C_KV), cols(kvw), to_tok_minor(state_win_kv[i]),
                                  to_tok_minor(cache_sel_kv[i]), past)
        win_new = jnp.transpose(win_new_t, (0, 4, 1, 2, 3))
        hs, conv_new = _merge(hs, y.reshape(db, C_A), bonus, g_, yb.reshape(db, C_B), mg, p_sample[i].reshape(db, -1),
                              state_ffn_conv[i].reshape(db, 2 * f2), mconsts, _pick_tile(db, 128), False, final)
        outs[1].append(kv6(pj["kvc_t"], db))
        outs[3].append(kv6(pj["kvs_t"], db))
        outs[5].append(win_new.reshape(db, wb, KV_HEADS, 2, HEAD_DIM))
        outs[7].append(s_new)
        outs[9].append(pa)
        outs[11].append(conv_new.reshape(db, CONV_W - 1, f2))

    stacked = [jnp.stack(o) for o in outs]
    return (hp.reshape(b, t, d_model), hs.reshape(db, dt, d_model), *stacked)
```

```python
import functools

import jax
import jax.numpy as jnp
from jax import lax
from jax.experimental import pallas as pl
from jax.experimental.pallas import tpu as pltpu

F32 = jnp.float32
BF16 = jnp.bfloat16
HI = lax.Precision.HIGHEST

A_HEADS = 8
A_HEAD_DIM = 64
C_A = A_HEADS * A_HEAD_DIM
R_W = 64
R_A = 64
R_G = 128
C_SHIFT = 3 * C_A + R_W + R_A + R_G
LNX_EPS = 64e-5
Q_HEADS = 8
KV_HEADS = 2
GROUP = Q_HEADS // KV_HEADS
HEAD_DIM = 64
C_B = Q_HEADS * HEAD_DIM
C_KV = KV_HEADS * 2 * HEAD_DIM
BLK = 64
N_SEL = 16
WINDOW = 512
D_CMP = 128
FORCE_SCORE = 1e4
CONV_W = 3
NORM_EPS = 1e-6

LANES = 128
SUBLANES = 8
VMEM_BYTES_V7X = 64 * 1024 * 1024

NEG = -1e30
M_FLOOR = -1e29
KV_LANES = 2 * HEAD_DIM

GATE_PAD = LANES


def _vmem_limit(nbytes):
    return int(min(max(nbytes, 16 * 1024 * 1024), VMEM_BYTES_V7X - 8 * 1024 * 1024))


def _const_spec(shape):
    nd = len(shape)
    return pl.BlockSpec(shape, lambda *_: (0,) * nd, pipeline_mode=pl.Buffered(1))


def _rms(x, g):
    return x * lax.rsqrt(jnp.mean(x * x, axis=-1, keepdims=True) + NORM_EPS) * g


def _dot(a, b, **kw):
    return jnp.dot(a, b, preferred_element_type=F32, **kw)


def _dot_ones(x, ones):
    hi = x.astype(BF16)
    lo = (x - hi.astype(F32)).astype(BF16)
    return _dot(hi, ones) + _dot(lo, ones)


def _dot_nt(a, b):
    return lax.dot_general(a, b, (((1,), (1,)), ((), ())), preferred_element_type=F32)


def _masked_softmax(s, mask):
    s = jnp.where(mask, s, NEG)
    m = jnp.max(s, axis=-1, keepdims=True)
    m = jnp.where(m > M_FLOOR, m, 0.0)
    e = jnp.exp(s - m)
    return e / jnp.maximum(jnp.sum(e, axis=-1, keepdims=True), 1e-30)


def _slope_col(hk, rows_per_head, pad_rows=0):
    cols = [jnp.full((rows_per_head, 1), 2.0 ** (-(hk * GROUP + g + 1)), F32) for g in range(GROUP)]
    if pad_rows:
        cols.append(jnp.zeros((pad_rows, 1), F32))
    return jnp.concatenate(cols, axis=0)


def _proj_body(x_ref, g_ref, w_ref, *o_refs, segs):
    xb = _rms(x_ref[...], g_ref[...]).astype(BF16)
    it = iter(o_refs)
    for off, width, outs in segs:
        r = _dot(xb, w_ref[:, off:off + width])
        for scale, transposed in outs:
            o_ref = next(it)
            val = r * scale if scale != 1.0 else r
            o_ref[...] = (val.T if transposed else val).astype(o_ref.dtype)


def _proj(x, g, w_perm, tm, kv_rows):
    rows, d = x.shape
    d_model = d
    kv = lambda name: ([(name, F32, 1.0, False)] if kv_rows or name == "kvc" else []) + \
        [(name + "16", BF16, 1.0, False), (name + "_t", F32, 1.0, True)]
    groups = ((C_SHIFT, [("pa", F32, 1.0, False)]), (C_B, [("q", BF16, HEAD_DIM ** -0.5, False)]),
              (C_KV, kv("kvc")), (C_KV, kv("kvs")), (C_KV, kv("kvw")),
              (GATE_PAD, [("gates", F32, 1.0, False)]), (2 * d_model, [("mg", F32, 1.0, False)]))
    o = 0
    segs, names, shapes, specs = [], [], [], []
    for width, outs in groups:
        segs.append((o, width, tuple((s, tr) for _, _, s, tr in outs)))
        for name, dt, _, tr in outs:
            names.append(name)
            shapes.append(jax.ShapeDtypeStruct((width, rows) if tr else (rows, width), dt))
            specs.append(pl.BlockSpec((width, tm), lambda i: (0, i)) if tr else pl.BlockSpec((tm, width), lambda i: (i, 0)))
        o += width
    n_tot = o
    out_bytes = sum(tm * s.shape[0 if s.shape[1] == rows else 1] * s.dtype.itemsize for s in shapes)
    vmem = 2 * tm * d * 4 + d * n_tot * 2 + 2 * out_bytes + (8 << 20)
    outs = pl.pallas_call(
        functools.partial(_proj_body, segs=tuple(segs)),
        grid=(rows // tm,),
        in_specs=[pl.BlockSpec((tm, d), lambda i: (i, 0)), _const_spec((1, d)), _const_spec((d, n_tot))],
        out_specs=specs,
        out_shape=shapes,
        compiler_params=pltpu.CompilerParams(dimension_semantics=("arbitrary",), vmem_limit_bytes=_vmem_limit(vmem)),
        name="proj",
    )(x, g, w_perm)
    return dict(zip(names, outs))


def _prep_body(pa_ref, prev_ref, mu_ref, w0_ref, w2_ref, a0_ref, a2_ref, g2_ref, kk_ref, ka_ref, rk_ref, bd_ref,
               r_o, w_o, lw_o, k_o, v_o, a_o, b_o, g_o, bonus_o, *, seq_mode):
    pf = pa_ref[...]
    if seq_mode:
        i = pl.program_id(0)
        first = jnp.where(i > 0, prev_ref[SUBLANES - 1:SUBLANES, :], 0.0)
        rid = lax.broadcasted_iota(jnp.int32, pf.shape, 0)
        prev = jnp.where(rid == 0, first, pltpu.roll(pf, 1, 0))
    else:
        prev = prev_ref[...]
    xs = pf + (prev - pf) * mu_ref[...]
    r = xs[:, 0:C_A]
    k = xs[:, C_A:2 * C_A]
    v = xs[:, 2 * C_A:3 * C_A]
    o = 3 * C_A
    wd = xs[:, o:o + R_W]
    ad = xs[:, o + R_W:o + R_W + R_A]
    gd = xs[:, o + R_W + R_A:]
    nz = -(w0_ref[...] + _dot(jnp.tanh(wd), w2_ref[...], precision=HI))
    softplus = jnp.maximum(nz, 0.0) + jnp.log(1.0 + jnp.exp(-jnp.abs(nz)))
    w = -softplus - 0.5
    a = jax.nn.sigmoid(a0_ref[...] + _dot(ad, a2_ref[...], precision=HI))
    g = _dot(jax.nn.sigmoid(gd), g2_ref[...], precision=HI)
    kk = k * kk_ref[...]
    ss = _dot_ones(kk * kk, bd_ref[...])
    kkn = kk / jnp.maximum(jnp.sqrt(ss), 1e-12)
    k2 = k * (1.0 + (a - 1.0) * ka_ref[...])
    r_o[...] = r
    lw = -jnp.exp(w)
    lw_o[...] = lw
    w_o[...] = jnp.exp(lw)
    k_o[...] = k2
    v_o[...] = v
    a_o[...] = -kkn
    b_o[...] = kkn * a
    g_o[...] = g
    bonus_o[...] = _dot_ones(r * k2 * rk_ref[...], bd_ref[...]) * v


def _rwkv_prep(pa, prev, rw, bd, tm, seq_mode):
    rows = pa.shape[0]
    mu, w0, w2, a0, a2, g2, k_k, k_a, r_k = rw
    if seq_mode:
        tb = tm // SUBLANES
        prev_spec = pl.BlockSpec((SUBLANES, C_SHIFT), lambda i: (jnp.maximum(i * tb - 1, 0), 0))
        prev = pa
    else:
        prev_spec = pl.BlockSpec((tm, C_SHIFT), lambda i: (i, 0))
    row = lambda z: z.reshape(1, -1)
    consts = [row(mu), row(w0), w2, row(a0), a2, g2, row(k_k), row(k_a), row(r_k), bd]
    vmem = 4 * tm * C_SHIFT * 4 + 2 * 8 * tm * C_A * 4 + 16 * tm * C_A * 4 + (8 << 20)
    return pl.pallas_call(
        functools.partial(_prep_body, seq_mode=seq_mode),
        grid=(rows // tm,),
        in_specs=[pl.BlockSpec((tm, C_SHIFT), lambda i: (i, 0)), prev_spec] + [_const_spec(c.shape) for c in consts],
        out_specs=[pl.BlockSpec((tm, C_A), lambda i: (i, 0))] * 9,
        out_shape=[jax.ShapeDtypeStruct((rows, C_A), F32)] * 9,
        compiler_params=pltpu.CompilerParams(dimension_semantics=("arbitrary",), vmem_limit_bytes=_vmem_limit(vmem)),
        name="rwkv_prep",
    )(pa, prev, *consts)


N_PAIR = A_HEADS // 2


def _scan_body(r_ref, w_ref, k_ref, v_ref, a_ref, b_ref, s0_ref, y_ref, so_ref, st_ref, *, tc, nc, nb):
    c = pl.program_id(1)

    @pl.when(c == 0)
    def _():
        for q in range(nb):
            for p in range(N_PAIR):
                st_ref[q, p] = jnp.concatenate([s0_ref[q, 2 * p], s0_ref[q, 2 * p + 1]], axis=-1)

    shape = (A_HEAD_DIM, LANES)
    lane = lax.broadcasted_iota(jnp.int32, shape, 1)
    sub = lax.broadcasted_iota(jnp.int32, shape, 0)
    lo = lane < A_HEAD_DIM
    diag = (lane & (A_HEAD_DIM - 1)) == sub

    def seg_sum(x):
        s_lo = jnp.sum(jnp.where(lo, x, 0.0), axis=-1, keepdims=True)
        s_hi = jnp.sum(jnp.where(lo, 0.0, x), axis=-1, keepdims=True)
        return jnp.where(lo, s_lo, s_hi)

    grp = min(SUBLANES, tc)

    def token_group(gi, carry):
        base = pl.multiple_of(gi * grp, grp)
        for q in range(nb):
            for p in range(N_PAIR):
                sl = slice(LANES * p, LANES * (p + 1))
                rt, wt, kt, vt, at, bt = (ref[q, pl.ds(base, grp), sl]
                                          for ref in (r_ref, w_ref, k_ref, v_ref, a_ref, b_ref))
                s = st_ref[q, p]
                ys = []
                for j in range(grp):
                    row = lambda z: z[j:j + 1, :]
                    sa = seg_sum(s * row(at))
                    v_col = seg_sum(jnp.where(diag, row(vt), 0.0))
                    s = s * row(wt) + sa * row(bt) + v_col * row(kt)
                    y_col = seg_sum(s * row(rt))
                    ys.append(jnp.sum(jnp.where(diag, y_col, 0.0), axis=0, keepdims=True))
                st_ref[q, p] = s
                y_ref[q, pl.ds(base, grp), sl] = jnp.concatenate(ys, axis=0) if grp > 1 else ys[0]
        return carry

    lax.fori_loop(0, tc // grp, token_group, 0)

    @pl.when(c == nc - 1)
    def _():
        for q in range(nb):
            for p in range(N_PAIR):
                s = st_ref[q, p]
                so_ref[q, 2 * p] = s[:, :A_HEAD_DIM]
                so_ref[q, 2 * p + 1] = s[:, A_HEAD_DIM:]


def _rwkv_scan(seqs, s0, tc, nb):
    b, t, _ = seqs[0].shape
    nc = t // tc
    seq_spec = pl.BlockSpec((nb, tc, C_A), lambda i, c: (i, c, 0))
    st_spec = pl.BlockSpec((nb, A_HEADS, A_HEAD_DIM, A_HEAD_DIM), lambda i, c: (i, 0, 0, 0))
    return pl.pallas_call(
        functools.partial(_scan_body, tc=tc, nc=nc, nb=nb),
        grid=(b // nb, nc),
        in_specs=[seq_spec] * 6 + [st_spec],
        out_specs=[seq_spec, st_spec],
        out_shape=[jax.ShapeDtypeStruct((b, t, C_A), F32), jax.ShapeDtypeStruct(s0.shape, F32)],
        scratch_shapes=[pltpu.VMEM((nb, N_PAIR, A_HEAD_DIM, LANES), F32)],
        compiler_params=pltpu.CompilerParams(dimension_semantics=("arbitrary", "arbitrary")),
        name="rwkv_scan",
    )(*seqs, s0)


SCAN_CHUNK = 64
SCAN_CHUNKS_PER_STEP = 4


def _mm(a, b):
    return _dot(a.astype(BF16), b.astype(BF16))


def _chunk_scan_body(r_ref, lw_ref, k_ref, v_ref, a_ref, b_ref, s0_ref, y_ref, so_ref, st_ref, *, tc, nc):
    cidx = pl.program_id(1)
    c = SCAN_CHUNK
    d = A_HEAD_DIM

    @pl.when(cidx == 0)
    def _():
        for h in range(A_HEADS):
            st_ref[h] = s0_ref[0, h].T

    ri = lax.broadcasted_iota(jnp.int32, (c, c), 0)
    ci = lax.broadcasted_iota(jnp.int32, (c, c), 1)
    lower = ci <= ri
    strict = ci < ri
    eye = lax.broadcasted_iota(jnp.int32, (d, d), 0) == lax.broadcasted_iota(jnp.int32, (d, d), 1)
    n_double = (c - 1).bit_length()
    per = SCAN_CHUNKS_PER_STEP
    slab = per * c
    sr = lax.broadcasted_iota(jnp.int32, (slab, slab), 0)
    sc = lax.broadcasted_iota(jnp.int32, (slab, slab), 1)
    tri = ((sr // c == sc // c) & (sc <= sr)).astype(F32)

    def step_fn(step, carry):
        base = pl.multiple_of(step * slab, slab)
        r, lw, k, v, a, b = (ref[0, pl.ds(base, slab), :] for ref in (r_ref, lw_ref, k_ref, v_ref, a_ref, b_ref))
        cum = _dot(tri, lw, precision=HI)
        tots = [cum[(q + 1) * c - 1:(q + 1) * c, :] for q in range(per)]
        tot = jnp.concatenate([jnp.broadcast_to(t_, (c, C_A)) for t_ in tots], axis=0)
        e_inv = jnp.exp(-cum)
        e_rest = jnp.exp(tot - cum)
        at = a * jnp.exp(cum - lw)
        rt = r * jnp.exp(cum)
        bt = b * e_inv
        kt = k * e_inv
        bh = b * e_rest
        kh = k * e_rest
        g_tot = [jnp.exp(t_) for t_ in tots]
        units = [(q, h) for q in range(per) for h in range(A_HEADS)]
        cut = lambda z, u: z[u[0] * c:(u[0] + 1) * c, u[1] * d:(u[1] + 1) * d]
        g4 = [_dot_nt(jnp.concatenate([cut(at, u), cut(rt, u)], axis=0).astype(BF16),
                      jnp.concatenate([cut(bt, u), cut(kt, u)], axis=0).astype(BF16)) for u in units]
        lp = [jnp.where(strict, g[:c, :c], 0.0).astype(BF16) for g in g4]
        m_l = [jnp.where(strict, g[:c, c:], 0.0) for g in g4]
        p_b = [jnp.where(lower, g[c:, :c], 0.0) for g in g4]
        p_k = [jnp.where(lower, g[c:, c:], 0.0) for g in g4]
        vb = [cut(v, u).astype(BF16) for u in units]
        n_u = range(len(units))
        z = [jnp.concatenate([cut(at, units[i]), _mm(m_l[i], vb[i])], axis=-1) for i in n_u]
        for q in range(n_double):
            z = [z[i] + _mm(lp[i], z[i]) for i in n_u]
            if q < n_double - 1:
                lp = [_mm(lp[i], lp[i]).astype(BF16) for i in n_u]
        zb = [zz.astype(BF16) for zz in z]
        bz = [_mm(cut(bh, units[i]).T, zb[i]) for i in n_u]
        kv = [_mm(cut(kh, units[i]).T, vb[i]) for i in n_u]
        pz = [_mm(p_b[i], zb[i]) for i in n_u]
        pv = [_mm(p_k[i], vb[i]) for i in n_u]
        rows = []
        for q in range(per):
            ys = []
            for h in range(A_HEADS):
                i = q * A_HEADS + h
                st = st_ref[h]
                a_c = jnp.where(eye, g_tot[q][:, h * d:(h + 1) * d], 0.0) + bz[i][:, :d]
                ys.append(_dot(cut(rt, units[i]) + pz[i][:, :d], st, precision=HI) + pz[i][:, d:] + pv[i])
                st_ref[h] = _dot(a_c, st, precision=HI) + bz[i][:, d:] + kv[i]
            rows.append(jnp.concatenate(ys, axis=-1))
        y_ref[0, pl.ds(base, slab), :] = jnp.concatenate(rows, axis=0)
        return carry

    lax.fori_loop(0, tc // slab, step_fn, 0)

    @pl.when(cidx == nc - 1)
    def _():
        for h in range(A_HEADS):
            so_ref[0, h] = st_ref[h].T


def _rwkv_chunk_scan(seqs, s0, tc):
    b, t, _ = seqs[0].shape
    nc = t // tc
    seq_spec = pl.BlockSpec((1, tc, C_A), lambda i, c: (i, c, 0))
    st_spec = pl.BlockSpec((1, A_HEADS, A_HEAD_DIM, A_HEAD_DIM), lambda i, c: (i, 0, 0, 0))
    return pl.pallas_call(
        functools.partial(_chunk_scan_body, tc=tc, nc=nc),
        grid=(b, nc),
        in_specs=[seq_spec] * 6 + [st_spec],
        out_specs=[seq_spec, st_spec],
        out_shape=[jax.ShapeDtypeStruct((b, t, C_A), F32), jax.ShapeDtypeStruct(s0.shape, F32)],
        scratch_shapes=[pltpu.VMEM((A_HEADS, A_HEAD_DIM, A_HEAD_DIM), F32)],
        compiler_params=pltpu.CompilerParams(dimension_semantics=("arbitrary", "arbitrary")),
        name="rwkv_chunk_scan",
    )(*seqs, s0)


def _gelu_tanh(x):
    return 0.5 * x * (1.0 + jnp.tanh(0.7978845608028654 * (x + 0.044715 * (x * x * x))))


def _compress_body(x_ref, pe_ref, w_ref, b1_ref, w2_ref, o_ref, bias_ref, *, nb):
    d = HEAD_DIM
    bias = [_dot(pe_ref[:, c * BLK * d:(c + 1) * BLK * d].astype(BF16), w_ref[c * BLK * d:(c + 1) * BLK * d, :])[0:1]
            + b1_ref[c:c + 1, :] for c in range(2)]
    bias_ref[...] = jnp.concatenate([jnp.broadcast_to(b_, (SUBLANES, D_CMP)) for b_ in bias], axis=-1)
    xt = jnp.swapaxes(x_ref[...].reshape(nb, BLK, C_KV), 0, 1)
    acc = [jnp.zeros((nb, D_CMP), F32) for _ in range(2 * KV_HEADS)]
    for r in range(BLK):
        xr = xt[r].astype(BF16)
        for hc in range(2 * KV_HEADS):
            c = hc % 2
            acc[hc] = acc[hc] + _dot(xr[:, hc * d:(hc + 1) * d], w_ref[(c * BLK + r) * d:(c * BLK + r + 1) * d, :])
    o_ref[...] = jnp.concatenate(
        [_dot(_gelu_tanh(acc[hc] + bias[hc % 2]).astype(BF16), w2_ref[hc % 2]) for hc in range(2 * KV_HEADS)], axis=-1)


def _compress(kv_rows, pe_flat, w1_rows, b1, w2b, nb):
    t = kv_rows.shape[0]
    consts = [pe_flat, w1_rows, b1, w2b]
    return pl.pallas_call(
        functools.partial(_compress_body, nb=nb),
        grid=(t // (nb * BLK),),
        in_specs=[pl.BlockSpec((nb * BLK, C_KV), lambda i: (i, 0))] + [_const_spec(c.shape) for c in consts],
        out_specs=[pl.BlockSpec((nb, C_KV), lambda i: (i, 0)), pl.BlockSpec((SUBLANES, 2 * D_CMP), lambda i: (0, 0))],
        out_shape=[jax.ShapeDtypeStruct((t // BLK, C_KV), F32), jax.ShapeDtypeStruct((SUBLANES, 2 * D_CMP), F32)],
        compiler_params=pltpu.CompilerParams(dimension_semantics=("arbitrary",)),
        name="nsa_compress",
    )(kv_rows, *consts)


PAGE_ROWS = KV_HEADS * 2 * HEAD_DIM


def _compress_pool_body(pt_ref, cache_ref, w_ref, b_ref, w2_ref, o_ref, xbuf, sems, *, pg, nseq):
    n = pl.program_id(0)
    slot = n % 2

    def page_copy(seq, buf, jp):
        return pltpu.make_async_copy(cache_ref.at[pt_ref[seq, jp]], xbuf.at[buf, pl.ds(jp * KV_HEADS, KV_HEADS)],
                                     sems.at[buf])

    def start_all(seq, buf):
        def start(jp, carry):
            page_copy(seq, buf, jp).start()
            return carry
        lax.fori_loop(0, pg, start, 0)

    def wait(jp, carry):
        page_copy(n, slot, jp).wait()
        return carry

    @pl.when(n == 0)
    def _():
        start_all(0, 0)

    @pl.when(n + 1 < nseq)
    def _():
        start_all(n + 1, 1 - slot)

    lax.fori_loop(0, pg, wait, 0)

    m = pg * KV_HEADS
    for c in range(2):
        xt = jnp.swapaxes(xbuf[slot, :, c * HEAD_DIM:(c + 1) * HEAD_DIM, :], 0, 1)
        acc = jnp.zeros((m, 2 * D_CMP), F32)
        for d in range(0, HEAD_DIM, 2):
            r0 = c * HEAD_DIM + d
            x_pair = jnp.concatenate([xt[d], xt[d + 1]], axis=-1).astype(BF16)
            acc = acc + _dot(x_pair, w_ref[r0 * 2 * BLK:(r0 + 2) * 2 * BLK, :])
        h = acc + b_ref[c:c + 1, :]
        o_ref[:, c * 2 * HEAD_DIM:(c + 1) * 2 * HEAD_DIM] = _dot(_gelu_tanh(h).astype(BF16), w2_ref[c])


def _compress_pool(page_table, cache_t, w_tok, bias2, w2_pair):
    nseq, pg = page_table.shape
    m = pg * KV_HEADS
    vmem = 3 * pg * PAGE_ROWS * 2 * BLK * 4 + int(w_tok.size) * 2 + 8 * m * 2 * D_CMP * 4 + (4 << 20)
    const = lambda z: pl.BlockSpec(z.shape, lambda i, pt: (0,) * z.ndim, pipeline_mode=pl.Buffered(1))
    grid_spec = pltpu.PrefetchScalarGridSpec(
        num_scalar_prefetch=1,
        grid=(nseq,),
        in_specs=[pl.BlockSpec(memory_space=pl.ANY), const(w_tok), const(bias2), const(w2_pair)],
        out_specs=pl.BlockSpec((m, C_KV), lambda i, pt: (i, 0)),
        scratch_shapes=[pltpu.VMEM((2, m, 2 * HEAD_DIM, 2 * BLK), F32), pltpu.SemaphoreType.DMA((2,))],
    )
    return pl.pallas_call(
        functools.partial(_compress_pool_body, pg=pg, nseq=nseq),
        grid_spec=grid_spec,
        out_shape=jax.ShapeDtypeStruct((nseq * m, C_KV), F32),
        compiler_params=pltpu.CompilerParams(dimension_semantics=("arbitrary",), vmem_limit_bytes=_vmem_limit(vmem)),
        name="nsa_compress_pool",
    )(page_table, cache_t, w_tok, bias2, w2_pair)


def _topk_mask(s, ids, k, axis):
    sel = jnp.zeros(s.shape, F32)
    for _ in range(k):
        m = jnp.max(s, axis=axis, keepdims=True)
        pick = jnp.min(jnp.where(s == m, ids, 1e9), axis=axis, keepdims=True)
        hit = ids == pick
        sel = jnp.where(hit, 1.0, sel)
        s = jnp.where(hit, -jnp.inf, s)
    return sel


def _head_rows(q_ref, hk, pad_rows=0, pad_lanes=True):
    parts = [q_ref[:, (hk * GROUP + g) * HEAD_DIM:(hk * GROUP + g + 1) * HEAD_DIM] for g in range(GROUP)]
    if pad_rows:
        parts.append(jnp.zeros((pad_rows, HEAD_DIM), parts[0].dtype))
    qh = jnp.concatenate(parts, axis=0)
    return jnp.concatenate([qh, jnp.zeros_like(qh)], axis=-1) if pad_lanes else qh


AUG_HI = HEAD_DIM
AUG_LO = HEAD_DIM + 1
AUG_ONE = HEAD_DIM + 2
AUG_BLK = HEAD_DIM + 3
POS_SPLIT = LANES


def _nsa_select_body(q_ref, kc_ref, kct_ref, oc_ref, ns_ref, flag_ref, *, tq, nbc):
    i = pl.program_id(0)
    t0 = i * tq
    qrow1 = t0 + lax.broadcasted_iota(jnp.int32, (1, tq), 1)
    qrow = jnp.concatenate([qrow1] * GROUP, axis=1)
    cur_row = qrow1 // BLK
    blkc = lax.broadcasted_iota(jnp.int32, (nbc, 1), 0)
    blkcf = blkc.astype(F32)
    pieces, masks, flags = [], [], []
    for hk in range(KV_HEADS):
        hl = slice(hk * KV_LANES, (hk + 1) * KV_LANES)
        qh = _head_rows(q_ref, hk)
        slope_row = jnp.concatenate([jnp.full((1, tq), 2.0 ** (-(hk * GROUP + g + 1)), F32) for g in range(GROUP)],
                                    axis=1)
        kcb = kc_ref[:, hl].astype(BF16)
        cposc = blkc * BLK + (BLK - 1)
        s_t = _dot_nt(kcb, qh) - slope_row * (qrow - cposc).astype(F32)
        s_t = jnp.where(cposc <= qrow, s_t, NEG)
        m_c = jnp.max(s_t, axis=0, keepdims=True)
        m_c = jnp.where(m_c > M_FLOOR, m_c, 0.0)
        e_c = jnp.exp(s_t - m_c)
        p_t = e_c / jnp.maximum(jnp.sum(e_c, axis=0, keepdims=True), 1e-30)
        o_c = _dot(kct_ref[hl, :].astype(BF16), p_t.astype(BF16))[HEAD_DIM:, :].T
        score = p_t[:, 0:tq]
        for g in range(1, GROUP):
            score = score + p_t[:, g * tq:(g + 1) * tq]
        forced = (blkc == 0) | (blkc == cur_row) | (blkc == cur_row - 1)
        sc = jnp.where(forced, FORCE_SCORE, score)
        sc = jnp.where(blkc > cur_row, -1.0, sc)
        sel_t = _topk_mask(sc, blkcf, min(N_SEL, nbc), axis=0)
        sel_q = jnp.where(blkc <= cur_row, sel_t, 0.0).T
        masks.append(1.0 - sel_q)
        flags.append(jnp.max(sel_q, axis=0, keepdims=True))
        pieces += [o_c[g * tq:(g + 1) * tq] for g in range(GROUP)]
    oc_ref[...] = jnp.concatenate(pieces, axis=-1)
    ns_ref[...] = jnp.concatenate(masks, axis=-1).astype(BF16)
    flag_ref[0] = jnp.concatenate(flags, axis=0)


def _nsa_select(q, kvc, tq):
    t = q.shape[0]
    nbc = kvc.shape[0]
    kct = kvc.T
    return pl.pallas_call(
        functools.partial(_nsa_select_body, tq=tq, nbc=nbc),
        grid=(t // tq,),
        in_specs=[pl.BlockSpec((tq, C_B), lambda i: (i, 0)), _const_spec(kvc.shape), _const_spec(kct.shape)],
        out_specs=[pl.BlockSpec((tq, C_B), lambda i: (i, 0)), pl.BlockSpec((tq, KV_HEADS * nbc), lambda i: (i, 0)),
                   pl.BlockSpec((1, KV_HEADS, nbc), lambda i: (i, 0, 0))],
        out_shape=[jax.ShapeDtypeStruct((t, C_B), F32), jax.ShapeDtypeStruct((t, KV_HEADS * nbc), BF16),
                   jax.ShapeDtypeStruct((t // tq, KV_HEADS, nbc), F32)],
        compiler_params=pltpu.CompilerParams(dimension_semantics=("arbitrary",)),
        name="nsa_select",
    )(q, kvc, kct)


def _nsa_prompt_body(cnt_ref, lst_ref, q_ref, gt_ref, oc_ref, ns_ref, ka_ref, vat_ref, kw_ref, em_ref, o_ref,
                     *, tq, tk, nbc, max_tiles):
    i = pl.program_id(0)
    t0 = i * tq
    rows = GROUP * tq
    qpos1 = t0 + lax.broadcasted_iota(jnp.int32, (tq, 1), 0)
    qpos = jnp.concatenate([qpos1] * GROUP, axis=0)
    qposf = qpos.astype(F32)
    qrow = jnp.concatenate([t0 + lax.broadcasted_iota(jnp.int32, (1, tq), 1)] * GROUP, axis=1)
    blk_row = lax.broadcasted_iota(jnp.int32, (1, nbc), 1)
    lane = lax.broadcasted_iota(jnp.int32, (1, KV_LANES), 1)
    gt = jax.nn.sigmoid(gt_ref[...])
    bpt = tk // BLK
    t0f = t0.astype(F32)
    o_ws, not_sels, q_augs = [], [], []
    for hk in range(KV_HEADS):
        hl = slice(hk * KV_LANES, (hk + 1) * KV_LANES)
        qh = _head_rows(q_ref, hk)
        slope = _slope_col(hk, tq)
        not_sels.append(ns_ref[:, hk * nbc:(hk + 1) * nbc])

        wl = WINDOW + tq
        ws = pl.multiple_of(jnp.maximum(t0 - WINDOW, 0), tq)
        kvw = kw_ref[pl.ds(ws, wl), hl]
        kposw = ws + lax.broadcasted_iota(jnp.int32, (1, wl), 1)
        s = _dot_nt(qh, kvw) - slope * (qposf - kposw.astype(F32))
        p_w = _masked_softmax(s, (kposw <= qpos) & (qpos - kposw < WINDOW))
        o_ws.append(_dot(p_w.astype(BF16), kvw)[:, HEAD_DIM:])

        q_augs.append(qh.astype(F32) + jnp.where(lane == AUG_HI, slope * POS_SPLIT, 0.0)
                      + jnp.where(lane == AUG_LO, slope, 0.0) + jnp.where(lane == AUG_ONE, -slope * t0f, 0.0))

    def sel_step(tiles, active, carry, causal):
        out = []
        for hk in range(KV_HEADS):
            m, acc = carry[hk]
            j = tiles[hk]
            k0 = pl.multiple_of(j * tk, tk)
            ns = not_sels[hk] if active[hk] is None else jnp.where(active[hk], not_sels[hk], jnp.ones_like(not_sels[hk]))
            mk = _dot(jnp.where(blk_row // bpt == j, ns, jnp.zeros_like(ns)), em_ref[...])
            q_j = (q_augs[hk] + jnp.concatenate([mk] * GROUP, axis=0)).astype(BF16)
            s = _dot_nt(ka_ref[hk, pl.ds(k0, tk), :], q_j)
            if causal:
                kpos = k0 + lax.broadcasted_iota(jnp.int32, (tk, 1), 0)
                s = jnp.where(kpos <= qrow, s, NEG)
            m_new = jnp.maximum(m, jnp.max(s, axis=0, keepdims=True))
            p = jnp.exp(s - m_new).astype(BF16)
            out.append((m_new, jnp.exp(m - m_new) * acc + _dot(vat_ref[hk, j], p)))
        return tuple(out)

    cnts = [cnt_ref[i * KV_HEADS + hk] for hk in range(KV_HEADS)]

    def skip_step(n, carry):
        active = [n < cnts[hk] - 1 for hk in range(KV_HEADS)]
        tiles = [jnp.where(active[hk], lst_ref[(i * KV_HEADS + hk) * max_tiles + n], 0) for hk in range(KV_HEADS)]
        return sel_step(tiles, active, carry, False)

    init = tuple((jnp.full((1, rows), M_FLOOR, F32), jnp.zeros((KV_LANES, rows), F32)) for _ in range(KV_HEADS))
    n_steps = functools.reduce(jnp.maximum, cnts) - 1
    carry = lax.fori_loop(0, n_steps, skip_step, init)
    last = (t0 + tq + tk - 1) // tk - 1
    carry = sel_step([last] * KV_HEADS, [None] * KV_HEADS, carry, True)

    pieces = []
    for hk in range(KV_HEADS):
        acc = carry[hk][1]
        o_s = (acc[:HEAD_DIM] / jnp.maximum(acc[HEAD_DIM:HEAD_DIM + 1], 1e-30)).T
        for g in range(GROUP):
            c0 = (hk * GROUP + g) * 3
            cs = slice((hk * GROUP + g) * HEAD_DIM, (hk * GROUP + g + 1) * HEAD_DIM)
            rs = slice(g * tq, (g + 1) * tq)
            pieces.append(gt[:, c0:c0 + 1] * oc_ref[:, cs] + gt[:, c0 + 1:c0 + 2] * o_s[rs]
                          + gt[:, c0 + 2:c0 + 3] * o_ws[hk][rs])
    o_ref[...] = jnp.concatenate(pieces, axis=-1)


def _nsa_prompt(q, gates, kvc, ks, kw, tq, tk):
    t = q.shape[0]
    nbc = kvc.shape[0]
    bpt = tk // BLK
    max_tiles = nbc // bpt
    assert t <= POS_SPLIT * 256 and AUG_BLK + bpt <= KV_LANES and t % tk == 0
    oc, not_sel, blk_any = _nsa_select(q, kvc, tq)
    tile_any = jnp.max(blk_any.reshape(t // tq, KV_HEADS, max_tiles, bpt), axis=-1) > 0.0
    rank = jnp.cumsum(tile_any.astype(jnp.int32), axis=-1) - 1
    slot = jnp.arange(max_tiles, dtype=jnp.int32)
    hit = tile_any[..., :, None] & (rank[..., :, None] == slot)
    lst = jnp.sum(jnp.where(hit, slot[:, None], 0), axis=-2).astype(jnp.int32).reshape(-1)
    cnt = jnp.sum(tile_any, axis=-1).astype(jnp.int32).reshape(-1)
    em = jnp.where(jnp.arange(KV_LANES)[None, :] == AUG_BLK + jnp.arange(nbc)[:, None] % bpt, NEG, 0.0).astype(BF16)
    pos = jnp.arange(t, dtype=jnp.int32)[:, None]
    aug_lane = jnp.arange(HEAD_DIM, dtype=jnp.int32)[None, :] + HEAD_DIM
    k_aug = (jnp.where(aug_lane == AUG_HI, pos // POS_SPLIT, 0) + jnp.where(aug_lane == AUG_LO, pos % POS_SPLIT, 0)
             + jnp.where(aug_lane == AUG_ONE, 1, 0)
             + jnp.where(aug_lane == AUG_BLK + (pos // BLK) % bpt, 1, 0)).astype(BF16)
    v_aug = jnp.broadcast_to(jnp.where(aug_lane == HEAD_DIM, 1, 0).astype(BF16), (t, HEAD_DIM))
    ka = jnp.stack([jnp.concatenate([ks[:, h * KV_LANES:h * KV_LANES + HEAD_DIM], k_aug], axis=1)
                    for h in range(KV_HEADS)])
    va = jnp.stack([jnp.concatenate([ks[:, h * KV_LANES + HEAD_DIM:(h + 1) * KV_LANES], v_aug], axis=1)
                    for h in range(KV_HEADS)])
    va = jnp.transpose(va.reshape(KV_HEADS, t // tk, tk, KV_LANES), (0, 1, 3, 2))
    vmem = 3 * t * C_KV * 2 + 24 * GROUP * tq * max(tk, WINDOW + tq) * 4 + (8 << 20)
    tile = lambda w: pl.BlockSpec((tq, w), lambda i, c, l: (i, 0))
    const = lambda z: pl.BlockSpec(z.shape, lambda i, c, l: (0,) * z.ndim, pipeline_mode=pl.Buffered(1))
    grid_spec = pltpu.PrefetchScalarGridSpec(
        num_scalar_prefetch=2,
        grid=(t // tq,),
        in_specs=[tile(C_B), tile(GATE_PAD), tile(C_B), tile(KV_HEADS * nbc), const(ka), const(va), const(kw), const(em)],
        out_specs=tile(C_B),
    )
    return pl.pallas_call(
        functools.partial(_nsa_prompt_body, tq=tq, tk=tk, nbc=nbc, max_tiles=max_tiles),
        grid_spec=grid_spec,
        out_shape=jax.ShapeDtypeStruct((t, C_B), F32),
        compiler_params=pltpu.CompilerParams(dimension_semantics=("arbitrary",), vmem_limit_bytes=_vmem_limit(vmem)),
        name="nsa_prompt",
    )(cnt, lst, q, gates, oc, not_sel, ka, va, kw, em)


PAGE_BLOCKS = 2
Q_ROWS = SUBLANES


def _samp_cmp_body(q_ref, kc_ref, oc_ref, sc_ref, *, n_pages, past):
    kbuf = kc_ref.at[0]
    nb_past = n_pages * PAGE_BLOCKS
    n_all = nb_past + 1
    qpos = past
    cur = qpos // BLK
    width = PAGE_BLOCKS * n_pages
    lane = lax.broadcasted_iota(jnp.int32, (1, width), 1)
    bid = jnp.where(lane < n_pages, PAGE_BLOCKS * lane, PAGE_BLOCKS * (lane - n_pages) + 1)
    cpos = bid * BLK + (BLK - 1)
    bid2 = jnp.concatenate([bid, nb_past + lane], axis=-1)
    scores = []
    for hk in range(KV_HEADS):
        q8 = _head_rows(q_ref.at[0], hk, pad_rows=Q_ROWS - GROUP, pad_lanes=False)
        z8 = jnp.zeros_like(q8)
        slope = _slope_col(hk, 1, pad_rows=Q_ROWS - GROUP)
        kmat = kbuf[:, hk * C_KV:hk * C_KV + KV_LANES].astype(BF16)
        vmat = kbuf[:, hk * C_KV + KV_LANES:(hk + 1) * C_KV].astype(BF16)
        s = jnp.concatenate([_dot_nt(jnp.concatenate([q8, z8], axis=-1), kmat),
                             _dot_nt(jnp.concatenate([z8, q8], axis=-1), kmat)], axis=-1)
        s = s - slope * (qpos - cpos).astype(F32)
        p_c = _masked_softmax(s, cpos <= qpos)
        o_even = _dot(p_c[:, :n_pages].astype(BF16), vmat)
        o_odd = _dot(p_c[:, n_pages:].astype(BF16), vmat)
        oc_ref[0, hk] = o_even[:, :HEAD_DIM] + o_odd[:, HEAD_DIM:]
        score = jnp.sum(p_c[0:GROUP], axis=0, keepdims=True)
        sc = jnp.concatenate([score, jnp.zeros((1, width), F32)], axis=-1)
        forced = (bid2 == 0) | (bid2 == cur) | (bid2 == cur - 1)
        sc = jnp.where(forced, FORCE_SCORE, sc)
        sc = jnp.where(bid2 > cur, -1.0, sc)
        scores.append(jnp.where(bid2 < n_all, sc, -jnp.inf))
    sc_ref[0] = jnp.concatenate(scores, axis=0)


def _samp_cmp(q3, kc3, past):
    db, n_pages, _ = kc3.shape
    wide = 2 * PAGE_BLOCKS * n_pages
    return pl.pallas_call(
        functools.partial(_samp_cmp_body, n_pages=n_pages, past=past),
        grid=(db,),
        in_specs=[pl.BlockSpec((1, 1, C_B), lambda n: (n, 0, 0)),
                  pl.BlockSpec((1, n_pages, KV_HEADS * C_KV), lambda n: (n, 0, 0))],
        out_specs=[pl.BlockSpec((1, KV_HEADS, Q_ROWS, HEAD_DIM), lambda n: (n, 0, 0, 0)),
                   pl.BlockSpec((1, KV_HEADS, wide), lambda n: (n, 0, 0))],
        out_shape=[jax.ShapeDtypeStruct((db, KV_HEADS, Q_ROWS, HEAD_DIM), F32),
                   jax.ShapeDtypeStruct((db, KV_HEADS, wide), F32)],
        compiler_params=pltpu.CompilerParams(dimension_semantics=("arbitrary",)),
        name="nsa_sample_cmp",
    )(q3, kc3)


def _samp_topk_body(sc_ref, idx_ref, *, n_pages, nb_past):
    s_t = sc_ref[...].T
    npos, nrow = s_t.shape
    width = PAGE_BLOCKS * n_pages
    pos = lax.broadcasted_iota(jnp.int32, (npos, 1), 0)
    ids = jnp.where(pos < n_pages, PAGE_BLOCKS * pos,
                    jnp.where(pos < width, PAGE_BLOCKS * (pos - n_pages) + 1, nb_past + pos - width)).astype(F32)
    picks = []
    for _ in range(N_SEL):
        m = jnp.max(s_t, axis=0, keepdims=True)
        pick = jnp.min(jnp.where(s_t == m, ids, 1e9), axis=0, keepdims=True)
        picks.append(pick)
        s_t = jnp.where(ids == pick, -jnp.inf, s_t)
    res = jnp.concatenate(picks + [jnp.zeros((LANES - N_SEL, nrow), F32)], axis=0)
    idx_ref[...] = res.T.astype(jnp.int32)


def _samp_topk(scores2, n_pages):
    rows = scores2.shape[0]
    return pl.pallas_call(
        functools.partial(_samp_topk_body, n_pages=n_pages, nb_past=n_pages * PAGE_BLOCKS),
        out_shape=jax.ShapeDtypeStruct((rows, LANES), jnp.int32),
        name="nsa_sample_topk",
    )(scores2)


def _samp_sel_body(pt_ref, idx_ref, q_ref, gt_ref, oc_ref, ksc_ref, kwr_ref, kwc_ref, win_ref, cache_ref,
                   o_ref, wino_ref, sbuf, sems, *, nb_past, past, wb, db):
    n = pl.program_id(0)
    qpos = past

    page_tok = PAGE_BLOCKS * BLK
    tok = lax.broadcasted_iota(jnp.int32, (HEAD_DIM, page_tok), 1)

    buf = n % 2

    def page_copy(sample, b, hk, s, ib):
        page = pt_ref[sample, ib // PAGE_BLOCKS]
        return pltpu.make_async_copy(cache_ref.at[page, hk], sbuf.at[b, hk, :, :, pl.ds(s * page_tok, page_tok)],
                                     sems.at[b, hk * N_SEL + s])

    def start_pages(sample, b):
        for hk in range(KV_HEADS):
            for s in range(N_SEL):
                ib = jnp.minimum(idx_ref[sample, hk * N_SEL + s], nb_past - 1)
                page_copy(sample, b, hk, s, ib).start()

    @pl.when(n == 0)
    def _():
        start_pages(0, 0)

    @pl.when(n + 1 < db)
    def _():
        start_pages(n + 1, 1 - buf)

    wtok = lax.broadcasted_iota(jnp.int32, (HEAD_DIM, wb), 1)
    for hk in range(KV_HEADS):
        for c in range(2):
            wino_ref[0, hk, c] = jnp.where(wtok == wb - 1, kwc_ref[0, 2 * hk + c],
                                           pltpu.roll(win_ref[0, hk, c], wb - 1, 1))

    for hk in range(KV_HEADS):
        for s in range(N_SEL):
            ib = idx_ref[n, hk * N_SEL + s]
            page_copy(n, buf, hk, s, jnp.minimum(ib, nb_past - 1)).wait()
            for c in range(2):
                cols = slice(s * page_tok, (s + 1) * page_tok)
                sbuf[buf, hk, c, :, cols] = jnp.where(ib >= nb_past, jnp.where(tok == 0, ksc_ref[0, 2 * hk + c], 0.0),
                                                      sbuf[buf, hk, c, :, cols])

    gt = jax.nn.sigmoid(gt_ref[0])
    nk = N_SEL * page_tok
    lane = lax.broadcasted_iota(jnp.int32, (1, nk), 1)
    kposw = past - wb + lax.broadcasted_iota(jnp.int32, (1, wb), 1)
    wmask = (kposw <= qpos) & (qpos - kposw < WINDOW) & (kposw >= 0)
    pieces = []
    for hk in range(KV_HEADS):
        q8 = _head_rows(q_ref.at[0], hk, pad_rows=Q_ROWS - GROUP, pad_lanes=False)
        slope = _slope_col(hk, 1, pad_rows=Q_ROWS - GROUP)
        ibv = jnp.zeros((1, nk), jnp.int32)
        for s in range(N_SEL):
            ibv = jnp.where(lane // page_tok == s, idx_ref[n, hk * N_SEL + s], ibv)
        spos = (ibv - ibv % PAGE_BLOCKS) * BLK + lane % page_tok
        s_s = _dot(q8, sbuf[buf, hk, 0].astype(BF16)) - slope * (qpos - spos).astype(F32)
        p_s = _masked_softmax(s_s, (spos // BLK == ibv) & (spos <= qpos))
        o_s = _dot_nt(p_s.astype(BF16), sbuf[buf, hk, 1].astype(BF16))
        kwn = kwr_ref[0]
        k_new = kwn[:, hk * KV_LANES:hk * KV_LANES + HEAD_DIM]
        v_new = kwn[:, hk * KV_LANES + HEAD_DIM:(hk + 1) * KV_LANES]
        s_w = _dot(q8, win_ref[0, hk, 0].astype(BF16)) - slope * (qpos - kposw).astype(F32)
        s_w = jnp.where(wmask, s_w, NEG)
        s_n = jnp.sum(q8.astype(F32) * k_new, axis=-1, keepdims=True)
        m_w = jnp.maximum(jnp.max(s_w, axis=-1, keepdims=True), s_n)
        e_w = jnp.exp(s_w - m_w)
        e_n = jnp.exp(s_n - m_w)
        den = jnp.sum(e_w, axis=-1, keepdims=True) + e_n
        o_w = (_dot_nt(e_w.astype(BF16), win_ref[0, hk, 1].astype(BF16)) + e_n * v_new) / den
        o_c = oc_ref[0, hk]
        for g in range(GROUP):
            c0 = (hk * GROUP + g) * 3
            pieces.append(gt[:, c0:c0 + 1] * o_c[g:g + 1] + gt[:, c0 + 1:c0 + 2] * o_s[g:g + 1]
                          + gt[:, c0 + 2:c0 + 3] * o_w[g:g + 1])
    o_ref[0] = jnp.concatenate(pieces, axis=-1)


def _samp_sel(page_table, idx, q3, gates3, oc, ks_cols, kw_row, kw_cols, win_t, cache_t, past):
    db, n_pages = page_table.shape
    wb = win_t.shape[-1]
    nb_past = n_pages * PAGE_BLOCKS
    row3 = lambda w: pl.BlockSpec((1, 1, w), lambda n, pt, ix: (n, 0, 0))
    col4 = pl.BlockSpec((1, 2 * KV_HEADS, HEAD_DIM, 1), lambda n, pt, ix: (n, 0, 0, 0))
    win_spec = pl.BlockSpec((1, KV_HEADS, 2, HEAD_DIM, wb), lambda n, pt, ix: (n, 0, 0, 0, 0))
    grid_spec = pltpu.PrefetchScalarGridSpec(
        num_scalar_prefetch=2,
        grid=(db,),
        in_specs=[row3(C_B), row3(GATE_PAD),
                  pl.BlockSpec((1, KV_HEADS, Q_ROWS, HEAD_DIM), lambda n, pt, ix: (n, 0, 0, 0)),
                  col4, row3(C_KV), col4, win_spec, pl.BlockSpec(memory_space=pl.ANY)],
        out_specs=[row3(C_B), win_spec],
        scratch_shapes=[pltpu.VMEM((2, KV_HEADS, 2, HEAD_DIM, N_SEL * PAGE_BLOCKS * BLK), F32),
                        pltpu.SemaphoreType.DMA((2, KV_HEADS * N_SEL))],
    )
    return pl.pallas_call(
        functools.partial(_samp_sel_body, nb_past=nb_past, past=past, wb=wb, db=db),
        grid_spec=grid_spec,
        out_shape=[jax.ShapeDtypeStruct((db, 1, C_B), F32), jax.ShapeDtypeStruct(win_t.shape, F32)],
        compiler_params=pltpu.CompilerParams(dimension_semantics=("arbitrary",)),
        name="nsa_sample_sel",
    )(page_table, idx, q3, gates3, oc, ks_cols, kw_row, kw_cols, win_t, cache_t)


FF_CHUNK = 2816


def _merge_body(x_ref, y_ref, bonus_ref, g_ref, yb_ref, mg_ref, p_ref, cbuf_ref,
                lng_ref, lnb_ref, bd_ref, woa_ref, wob_ref, wout_ref, n2_ref, wup_ref, cw_ref, cb_ref, wdn_ref,
                n3_ref, wpe_ref, wpg_ref, fg_ref, o_ref, cnew_ref, carry_ref, *, seq_mode, final, d_ff, tm):
    i = pl.program_id(0)
    d_model = x_ref.shape[1]
    y = y_ref[...]
    inv = 1.0 / A_HEAD_DIM
    mean = _dot_ones(y, bd_ref[...]) * inv
    d = y - mean
    var = _dot_ones(d * d, bd_ref[...]) * inv
    ya = (d * lax.rsqrt(var + LNX_EPS) * lng_ref[...] + lnb_ref[...] + bonus_ref[...]) * g_ref[...]
    mg = mg_ref[...]
    m = (jax.nn.sigmoid(mg[:, :d_model]) * _dot(ya.astype(BF16), woa_ref[...])
         + jax.nn.sigmoid(mg[:, d_model:]) * _dot(yb_ref[...].astype(BF16), wob_ref[...]))
    h = x_ref[...] + _dot(m.astype(BF16), wout_ref[...])
    xn = _rms(h, n2_ref[...]).astype(BF16)

    if seq_mode:
        @pl.when(i == 0)
        def _():
            carry_ref[...] = jnp.zeros_like(carry_ref)
        rid = lax.broadcasted_iota(jnp.int32, (tm, FF_CHUNK), 0)

    acc = jnp.zeros((tm, d_model), F32)
    for c in range(d_ff // FF_CHUNK):
        parts = []
        for half in range(2):
            cs = slice(half * d_ff + c * FF_CHUNK, half * d_ff + (c + 1) * FF_CHUNK)
            up = _dot(xn, wup_ref[:, cs])
            if seq_mode:
                t1 = carry_ref[SUBLANES - 1:SUBLANES, cs]
                t2 = carry_ref[SUBLANES - 2:SUBLANES - 1, cs]
                up1 = jnp.where(rid == 0, t1, pltpu.roll(up, 1, 0))
                up2 = jnp.where(rid == 0, t2, jnp.where(rid == 1, t1, pltpu.roll(up, 2, 0)))
                carry_ref[:, cs] = up[tm - SUBLANES:, :]
            else:
                up2 = cbuf_ref[:, cs]
                up1 = cbuf_ref[:, 2 * d_ff + cs.start:2 * d_ff + cs.stop]
                cnew_ref[:, cs] = up1
                cnew_ref[:, 2 * d_ff + cs.start:2 * d_ff + cs.stop] = up
            parts.append(cb_ref[:, cs] + cw_ref[0:1, cs] * up2 + cw_ref[1:2, cs] * up1 + cw_ref[2:3, cs] * up)
        a, gate = parts
        act = (a * jax.nn.sigmoid(a) * gate).astype(BF16)
        acc = acc + _dot(act, wdn_ref[c * FF_CHUNK:(c + 1) * FF_CHUNK, :])
    if seq_mode:
        cnew_ref[...] = carry_ref[...]
    h = h + acc
    pe = _dot(p_ref[...].astype(BF16), wpe_ref[...])
    h = h + pe * jax.nn.sigmoid(_dot(_rms(h, n3_ref[...]).astype(BF16), wpg_ref[...]))
    o_ref[...] = _rms(h, fg_ref[...]) if final else h


def _merge(x, y, bonus, g, yb, mg, p, cbuf, consts, tm, seq_mode, final):
    rows, d_model = x.shape
    d_ff = consts[10].shape[0]
    f2 = 2 * d_ff
    rowspec = lambda w: pl.BlockSpec((tm, w), lambda i: (i, 0))
    if seq_mode:
        cbuf_spec = _const_spec(cbuf.shape)
        cnew_shape, cnew_spec = (SUBLANES, f2), pl.BlockSpec((SUBLANES, f2), lambda i: (0, 0))
    else:
        cbuf_spec = rowspec(2 * f2)
        cnew_shape, cnew_spec = (rows, 2 * f2), rowspec(2 * f2)
    wbytes = sum(int(c.size) * c.dtype.itemsize for c in consts)
    act = tm * (d_model * 3 + C_A * 4 + 256) * 4 + (0 if seq_mode else 2 * tm * 2 * f2 * 4)
    vmem = wbytes + 2 * act + 8 * tm * d_model * 4 + (8 << 20)
    return pl.pallas_call(
        functools.partial(_merge_body, seq_mode=seq_mode, final=final, d_ff=d_ff, tm=tm),
        grid=(rows // tm,),
        in_specs=[rowspec(d_model), rowspec(C_A), rowspec(C_A), rowspec(C_A), rowspec(C_B), rowspec(2 * d_model),
                  rowspec(p.shape[1]), cbuf_spec] + [_const_spec(c.shape) for c in consts],
        out_specs=[rowspec(d_model), cnew_spec],
        out_shape=[jax.ShapeDtypeStruct((rows, d_model), F32), jax.ShapeDtypeStruct(cnew_shape, F32)],
        scratch_shapes=[pltpu.VMEM((SUBLANES, f2), F32)],
        compiler_params=pltpu.CompilerParams(dimension_semantics=("arbitrary",), vmem_limit_bytes=_vmem_limit(vmem)),
        name="merge_ffn",
    )(x, y, bonus, g, yb, mg, p, cbuf, *consts)


def _pick_tile(n, target):
    t = min(n, target)
    while n % t:
        t //= 2
    return t


def _head_block_diag():
    h = jnp.arange(C_A) // A_HEAD_DIM
    return (h[:, None] == h[None, :]).astype(BF16)


def kernel(x_prompt, x_sample, p_prompt, p_sample, cache_cmp_kv, cache_sel_kv, page_table, state_win_kv, state_wkv, state_shift, state_ffn_conv, norm1_g, w_in, shift_mu, rwkv_w0, rwkv_w2, rwkv_a0, rwkv_a2, rwkv_g2, rwkv_k_k, rwkv_k_a, rwkv_r_k, lnx_g, lnx_b, cmp_pe, cmp_w1, cmp_b1, cmp_w2, w_oa, w_ob, w_out, norm2_g, w_up, conv_w, conv_b, w_down, norm3_g, w_pe, w_pg, final_g):
    depth = w_in.shape[0]
    b, t, d_model = x_prompt.shape
    db, dt, _ = x_sample.shape
    n_pool, page = cache_cmp_kv.shape[1], cache_cmp_kv.shape[2]
    n_pages = page_table.shape[1]
    past = n_pages * page
    d_ff = w_down.shape[1]
    f2 = 2 * d_ff
    wb = state_win_kv.shape[2]
    assert b == 1 and dt == 1 and page == PAGE_BLOCKS * BLK
    assert t % 512 == 0 and t >= WINDOW + 256 and d_ff % FF_CHUNK == 0 and wb == WINDOW and past >= WINDOW
    assert N_SEL <= LANES and n_pages * PAGE_BLOCKS + 1 >= N_SEL

    bd = _head_block_diag()
    hp = x_prompt.reshape(t, d_model)
    hs = x_sample.reshape(db, d_model)
    outs = [[] for _ in range(12)]
    for i in range(depth):
        o_g = C_SHIFT + C_B + 3 * C_KV
        w_perm = jnp.concatenate(
            [w_in[i][:, :o_g], jnp.pad(w_in[i][:, o_g:o_g + 3 * Q_HEADS], ((0, 0), (0, GATE_PAD - 3 * Q_HEADS))),
             w_in[i][:, o_g + 3 * Q_HEADS:]], axis=1).astype(BF16)
        g1 = norm1_g[i].reshape(1, -1)
        rw = (shift_mu[i], rwkv_w0[i], rwkv_w2[i], rwkv_a0[i], rwkv_a2[i], rwkv_g2[i], rwkv_k_k[i], rwkv_k_a[i],
              rwkv_r_k[i].reshape(-1))
        pe_flat = jnp.pad(cmp_pe[i].reshape(1, -1), ((0, SUBLANES - 1), (0, 0)))
        w1_rows = cmp_w1[i].reshape(2 * BLK * HEAD_DIM, D_CMP).astype(BF16)
        eye_b = jnp.eye(PAGE_BLOCKS, dtype=F32)
        w_tok = jnp.einsum('crdf,eg->cdergf', cmp_w1[i], eye_b).reshape(2 * HEAD_DIM * PAGE_BLOCKS * BLK,
                                                                        PAGE_BLOCKS * D_CMP).astype(BF16)
        w2_pair = jnp.einsum('cfd,eg->cefgd', cmp_w2[i], eye_b).reshape(2, PAGE_BLOCKS * D_CMP,
                                                                       PAGE_BLOCKS * HEAD_DIM).astype(BF16)
        row = lambda z: z.reshape(1, -1)
        mconsts = [row(lnx_g[i]), row(lnx_b[i]), bd, w_oa[i].astype(BF16), w_ob[i].astype(BF16),
                   w_out[i].astype(BF16), row(norm2_g[i]), w_up[i].astype(BF16), conv_w[i], row(conv_b[i]),
                   w_down[i].astype(BF16), row(norm3_g[i]), w_pe[i].astype(BF16), w_pg[i].astype(BF16), row(final_g)]
        final = i == depth - 1

        pj = _proj(hp, g1, w_perm, _pick_tile(t, 256), False)
        pa, q, kvc, kvs16, kvw16, gates, mg = (pj[k_] for k_ in ("pa", "q", "kvc", "kvs16", "kvw16", "gates", "mg"))
        seqs = _rwkv_prep(pa, None, rw, bd, _pick_tile(t, 512), True)
        r_, w_, lw_, k_, v_, a_, b_, g_, bonus = seqs
        y, s_new = _rwkv_chunk_scan([z.reshape(1, t, C_A) for z in (r_, lw_, k_, v_, a_, b_)],
                                    jnp.zeros((1, A_HEADS, A_HEAD_DIM, A_HEAD_DIM), F32), _pick_tile(t, 512))
        kvc_blocks, cmp_bias = _compress(kvc, pe_flat, w1_rows, cmp_b1[i], cmp_w2[i].astype(BF16),
                                         _pick_tile(t // BLK, 64))
        bias_pair = jnp.tile(cmp_bias[0].reshape(2, D_CMP), (1, PAGE_BLOCKS))
        yb = _nsa_prompt(q, gates, kvc_blocks, kvs16, kvw16, 256, 512)
        hp, conv_new = _merge(hp, y.reshape(t, C_A), bonus, g_, yb, mg, p_prompt[i].reshape(t, -1),
                              jnp.zeros((SUBLANES, LANES), F32), mconsts, _pick_tile(t, 256), True, final)
        kv6 = lambda z_t, n_: jnp.transpose(z_t.reshape(KV_HEADS, 2, HEAD_DIM, z_t.shape[1]),
                                            (3, 0, 1, 2)).reshape(n_, -1, KV_HEADS, 2, HEAD_DIM)
        outs[0].append(kv6(pj["kvc_t"], 1))
        outs[2].append(kv6(pj["kvs_t"], 1))
        outs[4].append(kv6(pj["kvw_t"][:, t - min(WINDOW, t):], 1))
        outs[6].append(s_new)
        outs[8].append(pa[t - 1:t])
        outs[10].append(conv_new[SUBLANES - (CONV_W - 1):].reshape(1, CONV_W - 1, f2))

        pj = _proj(hs, g1, w_perm, _pick_tile(db, 128), True)
        pa, q, kvs, kvw, gates, mg = (pj[k_] for k_ in ("pa", "q", "kvs", "kvw", "gates", "mg"))
        seqs = _rwkv_prep(pa, state_shift[i], rw, bd, _pick_tile(db, 128), False)
        r_, w_, lw_, k_, v_, a_, b_, g_, bonus = seqs
        y, s_new = _rwkv_scan([z.reshape(db, 1, C_A) for z in (r_, w_, k_, v_, a_, b_)], state_wkv[i], 1,
                              _pick_tile(db, SUBLANES))
        to_tok_minor = lambda z: jnp.transpose(z, (0, 2, 3, 4, 1))
        kc_pages = _compress_pool(page_table, to_tok_minor(cache_cmp_kv[i]).reshape(n_pool, KV_HEADS, 2 * HEAD_DIM, page),
                                  w_tok, bias_pair, w2_pair)
        q3 = q.reshape(db, 1, C_B)
        oc, sel_scores = _samp_cmp(q3, kc_pages.reshape(db, n_pages, KV_HEADS * C_KV), past)
        idx = _samp_topk(sel_scores.reshape(db * KV_HEADS, -1), n_pages)
        idx2 = idx[:, :N_SEL].reshape(db, KV_HEADS * N_SEL)
        cols = lambda z: z.reshape(db, 2 * KV_HEADS, HEAD_DIM, 1)
        yb, win_new_t = _samp_sel(page_table, idx2, q3, gates.reshape(db, 1, GATE_PAD), oc, cols(kvs),
                                  kvw.reshape(db, 1, C_KV), cols(kvw), to_tok_minor(state_win_kv[i]),
                                  to_tok_minor(cache_sel_kv[i]), past)
        win_new = jnp.transpose(win_new_t, (0, 4, 1, 2, 3))
        hs, conv_new = _merge(hs, y.reshape(db, C_A), bonus, g_, yb.reshape(db, C_B), mg, p_sample[i].reshape(db, -1),
                              state_ffn_conv[i].reshape(db, 2 * f2), mconsts, _pick_tile(db, 128), False, final)
        outs[1].append(kv6(pj["kvc_t"], db))
        outs[3].append(kv6(pj["kvs_t"], db))
        outs[5].append(win_new.reshape(db, wb, KV_HEADS, 2, HEAD_DIM))
        outs[7].append(s_new)
        outs[9].append(pa)
        outs[11].append(conv_new.reshape(db, CONV_W - 1, f2))

    stacked = [jnp.stack(o) for o in outs]
    return (hp.reshape(b, t, d_model), hs.reshape(db, dt, d_model), *stacked)
```
